```python
import math
import jax, jax.numpy as jnp
from jax import lax
import numpy as np

D_MODEL = 1024
BATCH = 8
SEQ = 2048
DEPTH = 4

CHUNK = 64
HEAD_DIM = 64
N_MIX_HEADS = D_MODEL // HEAD_DIM
MEM_HEADS = 4
SELF_HEADS = N_MIX_HEADS - MEM_HEADS
SWA_KV_HEADS = 3
SWA_GROUP = SELF_HEADS // SWA_KV_HEADS
WINDOW = 128
WINDOW_CHUNKS = WINDOW // CHUNK
ROT_DIM = HEAD_DIM // 4
ROT_HALF = ROT_DIM // 2
ROPE_THETA = 500000.0
FOX_BLOCK = 128
MEM_LEN = 256
D_FF = ((8 * D_MODEL // 3 + 127) // 128) * 128
N_EXPERTS = 8
TOP_K = 2
EXPERT_BLOCK = 128
LN_EPS = 1e-5
NEG = -1e30
ATTN_SCALE = HEAD_DIM ** -0.5
DN_ALPHA = (2.0 * DEPTH) ** 0.25
DN_BETA = (8.0 * DEPTH) ** -0.25
N_SWA_LAYERS = (DEPTH + 1) // 2
N_FOX_LAYERS = DEPTH // 2
SWA_IN = (SELF_HEADS + 2 * SWA_KV_HEADS + MEM_HEADS) * HEAD_DIM
FOX_IN = (3 * SELF_HEADS + MEM_HEADS) * HEAD_DIM + SELF_HEADS

kernel_name = "hybrid_swa_fox_memxattn_deepnorm_moe"


def layer_norm(x, g, b):
    xf = x.astype(jnp.float32)
    mu = jnp.mean(xf, axis=-1, keepdims=True)
    var = jnp.mean(jnp.square(xf - mu), axis=-1, keepdims=True)
    return ((xf - mu) * lax.rsqrt(var + LN_EPS) * g + b).astype(x.dtype)


def rope_tables(positions):
    inv_freq = ROPE_THETA ** (-jnp.arange(0, ROT_DIM, 2, dtype=jnp.float32) / ROT_DIM)
    ang = positions.astype(jnp.float32)[..., None] * inv_freq
    return jnp.cos(ang)[:, :, None, :], jnp.sin(ang)[:, :, None, :]


def partial_rope(t, cos, sin):
    r = t[..., :ROT_DIM].astype(jnp.float32)
    r1, r2 = r[..., :ROT_HALF], r[..., ROT_HALF:]
    rot = jnp.concatenate([r1 * cos - r2 * sin, r2 * cos + r1 * sin], axis=-1)
    return jnp.concatenate([rot.astype(t.dtype), t[..., ROT_DIM:]], axis=-1)


def swa_sink_attention(q, k, v, sinks):
    B, S = q.shape[0], q.shape[1]
    n_c = S // CHUNK
    pad = WINDOW_CHUNKS * CHUNK
    n_keys = (WINDOW_CHUNKS + 1) * CHUNK
    qb = q.reshape(B, n_c, CHUNK, SWA_KV_HEADS, SWA_GROUP, HEAD_DIM)

    def band(t):
        tp = jnp.pad(t, ((0, 0), (pad, 0), (0, 0), (0, 0)))
        tp = tp.reshape(B, n_c + WINDOW_CHUNKS, CHUNK, SWA_KV_HEADS, HEAD_DIM)
        return jnp.concatenate([tp[:, m:m + n_c] for m in range(WINDOW_CHUNKS + 1)], axis=2)

    kb, vb = band(k), band(v)
    s = jnp.einsum('bnqhgd,bnkhd->bnhgqk', qb, kb).astype(jnp.float32) * ATTN_SCALE
    key_chunk = jnp.arange(n_c)[:, None] - WINDOW_CHUNKS + (jnp.arange(n_keys) // CHUNK)[None, :]
    valid = key_chunk >= 0
    s = jnp.where(valid[None, :, None, None, None, :], s, NEG)
    sk = sinks.astype(jnp.float32).reshape(SWA_KV_HEADS, SWA_GROUP)[None, None, :, :, None]
    m = jnp.maximum(jnp.max(s, axis=-1), sk)
    p = jnp.exp(s - m[..., None])
    den = jnp.sum(p, axis=-1) + jnp.exp(sk - m)
    probs = (p / den[..., None]).astype(v.dtype)
    o = jnp.einsum('bnhgqk,bnkhd->bnqhgd', probs, vb)
    return o.reshape(B, S, SELF_HEADS * HEAD_DIM)


def forgetting_attention(q, k, v, log_f):
    B, S = q.shape[0], q.shape[1]
    cum = jnp.cumsum(log_f, axis=1).transpose(0, 2, 1)
    outs = []
    for i in range(S // FOX_BLOCK):
        lo, hi = i * FOX_BLOCK, (i + 1) * FOX_BLOCK
        s = jnp.einsum('bqhd,bkhd->bhqk', q[:, lo:hi], k[:, :hi]).astype(jnp.float32) * ATTN_SCALE
        bias = cum[:, :, lo:hi, None] - cum[:, :, None, :hi]
        causal = jnp.arange(lo, hi)[:, None] >= jnp.arange(hi)[None, :]
        s = jnp.where(causal, s + bias, NEG)
        p = jax.nn.softmax(s, axis=-1).astype(v.dtype)
        outs.append(jnp.einsum('bhqk,bkhd->bqhd', p, v[:, :hi]))
    return jnp.concatenate(outs, axis=1).reshape(B, S, SELF_HEADS * HEAD_DIM)


def memory_attention(qc, km, vm):
    B, S = qc.shape[0], qc.shape[1]
    s = jnp.einsum('bshd,bmhd->bhsm', qc, km).astype(jnp.float32) * ATTN_SCALE
    p = jax.nn.softmax(s, axis=-1).astype(vm.dtype)
    return jnp.einsum('bhsm,bmhd->bshd', p, vm).reshape(B, S, MEM_HEADS * HEAD_DIM)


def swiglu(x, w_in, w_out):
    h = x @ w_in
    a, b = h[..., :D_FF], h[..., D_FF:]
    return (jax.nn.silu(a) * b) @ w_out


def moe_swiglu(x, w_router, b_router, w_in, w_out):
    shp = x.shape
    x2 = x.reshape(-1, shp[-1])
    n_tok = x2.shape[0]
    logits = (x2 @ w_router).astype(jnp.float32) + b_router.astype(jnp.float32)
    top_vals, top_idx = lax.top_k(logits, TOP_K)
    gates = jax.nn.softmax(top_vals, axis=-1)
    n_asg = n_tok * TOP_K
    e_flat = top_idx.reshape(-1)
    t_flat = jnp.repeat(jnp.arange(n_tok, dtype=jnp.int32), TOP_K)
    g_flat = gates.reshape(-1)
    order = jnp.argsort(e_flat)
    se, st, sg = e_flat[order], t_flat[order], g_flat[order]
    counts = jnp.bincount(e_flat, length=N_EXPERTS)
    starts = jnp.cumsum(counts) - counts
    pcounts = (counts + EXPERT_BLOCK - 1) // EXPERT_BLOCK * EXPERT_BLOCK
    pends = jnp.cumsum(pcounts)
    pstarts = pends - pcounts
    dest = pstarts[se] + (jnp.arange(n_asg) - starts[se])
    n_blocks = -(-n_asg // EXPERT_BLOCK) + N_EXPERTS
    n_slots = n_blocks * EXPERT_BLOCK
    slot_tok = jnp.zeros((n_slots,), jnp.int32).at[dest].set(st)
    slot_gate = jnp.zeros((n_slots,), x2.dtype).at[dest].set(sg.astype(x2.dtype))
    block_exp = jnp.minimum(
        jnp.searchsorted(pends, jnp.arange(n_blocks) * EXPERT_BLOCK, side='right'), N_EXPERTS - 1)
    xs = x2[slot_tok].reshape(n_blocks, EXPERT_BLOCK, shp[-1])

    def expert_block(args):
        xb, e = args
        h = xb @ w_in[e]
        return (jax.nn.silu(h[:, :D_FF]) * h[:, D_FF:]) @ w_out[e]

    ys = lax.map(expert_block, (xs, block_exp)).reshape(n_slots, shp[-1])
    y = jnp.zeros_like(x2).at[slot_tok].add((ys * slot_gate[:, None]).astype(x2.dtype))
    return y.reshape(shp)


def setup_inputs(seed: int = 0) -> dict:
    key = jax.random.key(seed)
    ks = jax.random.split(key, 20)
    nrm = jax.random.normal
    f32 = jnp.float32
    x = nrm(ks[0], (BATCH, SEQ, D_MODEL), f32)
    mem = nrm(ks[1], (BATCH, MEM_LEN, D_MODEL), f32)
    offset = jax.random.randint(ks[2], (BATCH, 1), 0, 1 << 16, dtype=jnp.int32)
    positions = offset + jnp.arange(SEQ, dtype=jnp.int32)[None, :]
    sd = D_MODEL ** -0.5
    sf = D_FF ** -0.5
    return {
        "x": x,
        "mem": mem,
        "positions": positions,
        "w_in_swa": nrm(ks[3], (N_SWA_LAYERS, D_MODEL, SWA_IN), f32) * sd,
        "attn_sinks": nrm(ks[4], (N_SWA_LAYERS, SELF_HEADS), f32) * 0.5,
        "w_in_fox": nrm(ks[5], (N_FOX_LAYERS, D_MODEL, FOX_IN), f32) * sd,
        "b_forget": jax.random.uniform(ks[6], (N_FOX_LAYERS, SELF_HEADS), f32, 1.0, 4.0),
        "w_mem_kv": nrm(ks[7], (DEPTH, D_MODEL, 2 * MEM_HEADS * HEAD_DIM), f32) * sd,
        "w_out": nrm(ks[8], (DEPTH, D_MODEL, D_MODEL), f32) * sd * DN_BETA,
        "ln_attn_g": 1.0 + 0.02 * nrm(ks[9], (DEPTH, D_MODEL), f32),
        "ln_attn_b": 0.02 * nrm(ks[10], (DEPTH, D_MODEL), f32),
        "w_ffn_in": nrm(ks[11], (N_SWA_LAYERS, D_MODEL, 2 * D_FF), f32) * sd,
        "w_ffn_out": nrm(ks[12], (N_SWA_LAYERS, D_FF, D_MODEL), f32) * sf * DN_BETA,
        "w_router": nrm(ks[13], (N_FOX_LAYERS, D_MODEL, N_EXPERTS), f32) * sd,
        "b_router": 0.01 * nrm(ks[14], (N_FOX_LAYERS, N_EXPERTS), f32),
        "w_exp_in": nrm(ks[15], (N_FOX_LAYERS, N_EXPERTS, D_MODEL, 2 * D_FF), f32) * sd,
        "w_exp_out": nrm(ks[16], (N_FOX_LAYERS, N_EXPERTS, D_FF, D_MODEL), f32) * sf * DN_BETA,
        "ln_ffn_g": 1.0 + 0.02 * nrm(ks[17], (DEPTH, D_MODEL), f32),
        "ln_ffn_b": 0.02 * nrm(ks[18], (DEPTH, D_MODEL), f32),
    }


def reference(x, mem, positions, w_in_swa, attn_sinks, w_in_fox, b_forget, w_mem_kv, w_out,
              ln_attn_g, ln_attn_b, w_ffn_in, w_ffn_out, w_router, b_router, w_exp_in,
              w_exp_out, ln_ffn_g, ln_ffn_b):
    B, S = x.shape[0], x.shape[1]
    cos, sin = rope_tables(positions)
    hq = SELF_HEADS * HEAD_DIM
    hkv = SWA_KV_HEADS * HEAD_DIM
    hm = MEM_HEADS * HEAD_DIM
    for i in range(DEPTH):
        j = i // 2
        kvm = mem @ w_mem_kv[i]
        km = kvm[..., :hm].reshape(B, MEM_LEN, MEM_HEADS, HEAD_DIM)
        vm = kvm[..., hm:].reshape(B, MEM_LEN, MEM_HEADS, HEAD_DIM)
        if i % 2 == 0:
            proj = x @ w_in_swa[j]
            q = proj[..., :hq].reshape(B, S, SELF_HEADS, HEAD_DIM)
            k = proj[..., hq:hq + hkv].reshape(B, S, SWA_KV_HEADS, HEAD_DIM)
            v = proj[..., hq + hkv:hq + 2 * hkv].reshape(B, S, SWA_KV_HEADS, HEAD_DIM)
            qc = proj[..., hq + 2 * hkv:].reshape(B, S, MEM_HEADS, HEAD_DIM)
            q, k = partial_rope(q, cos, sin), partial_rope(k, cos, sin)
            self_out = swa_sink_attention(q, k, v, attn_sinks[j])
        else:
            proj = x @ w_in_fox[j]
            q = proj[..., :hq].reshape(B, S, SELF_HEADS, HEAD_DIM)
            k = proj[..., hq:2 * hq].reshape(B, S, SELF_HEADS, HEAD_DIM)
            v = proj[..., 2 * hq:3 * hq].reshape(B, S, SELF_HEADS, HEAD_DIM)
            qc = proj[..., 3 * hq:3 * hq + hm].reshape(B, S, MEM_HEADS, HEAD_DIM)
            f_logit = proj[..., 3 * hq + hm:].astype(jnp.float32) + b_forget[j].astype(jnp.float32)
            log_f = jax.nn.log_sigmoid(f_logit)
            self_out = forgetting_attention(q, k, v, log_f)
        cross_out = memory_attention(qc, km, vm)
        mix = jnp.concatenate([self_out, cross_out], axis=-1) @ w_out[i]
        x = layer_norm(DN_ALPHA * x + mix, ln_attn_g[i], ln_attn_b[i])
        if i % 2 == 0:
            ffn = swiglu(x, w_ffn_in[j], w_ffn_out[j])
        else:
            ffn = moe_swiglu(x, w_router[j], b_router[j], w_exp_in[j], w_exp_out[j])
        x = layer_norm(DN_ALPHA * x + ffn, ln_ffn_g[i], ln_ffn_b[i])
    return x
```

```python
import functools

import jax
import jax.numpy as jnp
from jax import lax
from jax.experimental import pallas as pl
from jax.experimental.pallas import tpu as pltpu

D_MODEL = 1024
DEPTH = 4
CHUNK = 64
HEAD_DIM = 64
MEM_HEADS = 4
SELF_HEADS = 12
SWA_KV_HEADS = 3
SWA_GROUP = SELF_HEADS // SWA_KV_HEADS
WINDOW_CHUNKS = 2
ROT_DIM = HEAD_DIM // 4
ROT_HALF = ROT_DIM // 2
ROPE_THETA = 500000.0
MEM_LEN = 256
D_FF = 2816
N_EXPERTS = 8
LN_EPS = 1e-5
NEG = -1e30
ATTN_SCALE = HEAD_DIM ** -0.5
DN_ALPHA = (2.0 * DEPTH) ** 0.25

HQ = SELF_HEADS * HEAD_DIM
HM = MEM_HEADS * HEAD_DIM
HKV2 = 2 * SWA_KV_HEADS * HEAD_DIM
SWA_COLS = HQ + 2 * HKV2 + HM
FOX_COLS = 3 * HQ + HM + 128

LANES = 128
VMEM_LIMIT_BYTES = 56 * 1024 * 1024

PROJ_TM = 512
FOX_PROJ_TM = 256
SWA_TQ = 256
SWA_SUB = 128
FOX_TQ = 256
MEM_TQ = 512
FFN_TM = 256
FF_CHUNK = D_FF // 2
ROUTER_TM = 512
MOE_TM = 256
ROW_TM = 256

F32 = jnp.float32
BF16 = jnp.bfloat16


def _cparams(*sem):
    return pltpu.CompilerParams(dimension_semantics=sem, vmem_limit_bytes=VMEM_LIMIT_BYTES)


def _layer_norm(y, g, b):
    mu = jnp.mean(y, axis=-1, keepdims=True)
    d = y - mu
    var = jnp.mean(d * d, axis=-1, keepdims=True)
    return d * lax.rsqrt(var + LN_EPS) * g + b


def _split3(a):
    hi = a.astype(BF16)
    r1 = a - hi.astype(F32)
    mid = r1.astype(BF16)
    lo = (r1 - mid.astype(F32)).astype(BF16)
    return hi, mid, lo


def _dot(a, b):
    return jnp.dot(a, b, preferred_element_type=F32)


def _dot_nt(a, b):
    return lax.dot_general(a, b, (((1,), (1,)), ((), ())), preferred_element_type=F32)


def _rope_table_kernel(pos_ref, freq_ref, cos_ref, sin_ref):
    ang = pos_ref[...].astype(F32) * freq_ref[...]
    sub = lax.broadcasted_iota(jnp.int32, ang.shape, 1) & (HEAD_DIM - 1)
    c, s = jnp.cos(ang), jnp.sin(ang)
    cos_ref[...] = jnp.where(sub < ROT_DIM, c, 1.0)
    sin_ref[...] = jnp.where(sub < ROT_HALF, -s, jnp.where(sub < ROT_DIM, s, 0.0))


def _rope_tables(positions):
    t = positions.size
    tm = min(t, 2048)
    inv_freq = ROPE_THETA ** (-jnp.arange(0, ROT_DIM, 2, dtype=F32) / ROT_DIM)
    lane = jnp.arange(LANES) % HEAD_DIM % ROT_HALF
    freq = inv_freq[lane][None, :]
    out = jax.ShapeDtypeStruct((t, LANES), F32)
    return pl.pallas_call(
        _rope_table_kernel,
        out_shape=(out, out),
        grid=(t // tm,),
        in_specs=[pl.BlockSpec((tm, 1), lambda i: (i, 0)),
                  pl.BlockSpec((1, LANES), lambda i: (0, 0))],
        out_specs=(pl.BlockSpec((tm, LANES), lambda i: (i, 0)),
                   pl.BlockSpec((tm, LANES), lambda i: (i, 0))),
        compiler_params=_cparams("parallel"),
        name="rope_tables",
    )(positions.reshape(t, 1), freq)


def _mem_kv_kernel(m_ref, w_ref, o_ref):
    o_ref[...] = _dot(m_ref[...].astype(BF16), w_ref[...]).astype(BF16)


def _mem_kv(mem2d, w):
    n = mem2d.shape[0]
    tm = min(n, PROJ_TM)
    return pl.pallas_call(
        _mem_kv_kernel,
        out_shape=jax.ShapeDtypeStruct((n, 2 * HM), BF16),
        grid=(n // tm,),
        in_specs=[pl.BlockSpec((tm, D_MODEL), lambda i: (i, 0)),
                  pl.BlockSpec((D_MODEL, 2 * HM), lambda i: (0, 0))],
        out_specs=pl.BlockSpec((tm, 2 * HM), lambda i: (i, 0)),
        compiler_params=_cparams("parallel"),
        name="mem_kv_proj",
    )(mem2d, w)


def _swa_proj_kernel(x_ref, w_ref, cos_ref, sin_ref, q_ref, kk_ref, vv_ref, qc_ref):
    acc = _dot(x_ref[...].astype(BF16), w_ref[...])
    cos, sin = cos_ref[...], sin_ref[...]
    first = (lax.broadcasted_iota(jnp.int32, cos.shape, 1) & (HEAD_DIM - 1)) < ROT_HALF

    def rope(t):
        fwd = pltpu.roll(t, LANES - ROT_HALF, axis=1)
        bwd = pltpu.roll(t, ROT_HALF, axis=1)
        return t * cos + jnp.where(first, fwd, bwd) * sin

    for g in range(HQ // LANES):
        sl = slice(g * LANES, (g + 1) * LANES)
        q_ref[:, sl] = (rope(acc[:, sl]) * ATTN_SCALE).astype(BF16)
    for g in range(HKV2 // LANES):
        src = slice(HQ + g * LANES, HQ + (g + 1) * LANES)
        kk_ref[:, g * LANES:(g + 1) * LANES] = rope(acc[:, src]).astype(BF16)
    vv_ref[...] = acc[:, HQ + HKV2:HQ + 2 * HKV2].astype(BF16)
    qc_ref[...] = (acc[:, HQ + 2 * HKV2:] * ATTN_SCALE).astype(BF16)


def _swa_proj(xf, w, cos_t, sin_t):
    t = xf.shape[0]
    tm = min(t, PROJ_TM)
    row = lambda n: pl.BlockSpec((tm, n), lambda i: (i, 0))
    return pl.pallas_call(
        _swa_proj_kernel,
        out_shape=(jax.ShapeDtypeStruct((t, HQ), BF16), jax.ShapeDtypeStruct((t, HKV2), BF16),
                   jax.ShapeDtypeStruct((t, HKV2), BF16), jax.ShapeDtypeStruct((t, HM), BF16)),
        grid=(t // tm,),
        in_specs=[row(D_MODEL), pl.BlockSpec((D_MODEL, SWA_COLS), lambda i: (0, 0)),
                  row(LANES), row(LANES)],
        out_specs=(row(HQ), row(HKV2), row(HKV2), row(HM)),
        compiler_params=_cparams("parallel"),
        name="swa_proj",
    )(xf, w, cos_t, sin_t)


def _swa_attn_kernel(sink_ref, q_ref, kp_ref, kc_ref, vp_ref, vc_ref, o_ref):
    first_tile = pl.program_id(1) == 0
    half = SWA_TQ - SWA_SUB
    lane = lax.broadcasted_iota(jnp.int32, (SWA_SUB, LANES), 1)
    shift = CHUNK.bit_length() - 1
    rc = lax.broadcasted_iota(jnp.int32, (SWA_SUB, 2 * SWA_SUB), 0) >> shift
    cc = lax.broadcasted_iota(jnp.int32, (SWA_SUB, 2 * SWA_SUB), 1) >> shift
    band = (cc >= rc) & (cc <= rc + WINDOW_CHUNKS)
    band_prev = band & ((cc >= SWA_SUB // CHUNK) | jnp.logical_not(first_tile))
    for sub in range(SWA_TQ // SWA_SUB):
        rows = slice(sub * SWA_SUB, (sub + 1) * SWA_SUB)
        mask = band_prev if sub == 0 else band
        for h in range(SWA_KV_HEADS):
            cols = slice(h * LANES, (h + 1) * LANES)
            if sub == 0:
                kcat = jnp.concatenate([kp_ref[0, half:, cols], kc_ref[0, :SWA_SUB, cols]], axis=0)
                vcat = jnp.concatenate([vp_ref[0, half:, cols], vc_ref[0, :SWA_SUB, cols]], axis=0)
            else:
                kcat = kc_ref[0, :, cols]
                vcat = vc_ref[0, :, cols]
            for pair in range(SWA_GROUP // 2):
                g = h * (SWA_GROUP // 2) + pair
                qcols = slice(g * LANES, (g + 1) * LANES)
                qf = q_ref[0, rows, qcols].astype(F32)
                outs = []
                for a in range(2):
                    keep = (lane < HEAD_DIM) if a == 0 else (lane >= HEAD_DIM)
                    qa = jnp.where(keep, qf, 0.0).astype(BF16)
                    s = jnp.where(mask, _dot_nt(qa, kcat), NEG)
                    sink = sink_ref[2 * g + a]
                    m = jnp.maximum(jnp.max(s, axis=-1, keepdims=True), sink)
                    p = jnp.exp(s - m)
                    den = jnp.sum(p, axis=-1, keepdims=True) + jnp.exp(sink - m)
                    probs = (p * (1.0 / den)).astype(BF16)
                    outs.append(_dot(probs, vcat))
                o = jnp.where(lane < HEAD_DIM, outs[0], outs[1])
                o_ref[0, rows, qcols] = o.astype(BF16)


def _swa_attn(q, kk, vv, sinks):
    b, s, _ = q.shape
    nt = s // SWA_TQ
    cur = lambda n: pl.BlockSpec((1, SWA_TQ, n), lambda bi, i: (bi, i, 0))
    prev = lambda n: pl.BlockSpec((1, SWA_TQ, n), lambda bi, i: (bi, jnp.maximum(i - 1, 0), 0))
    return pl.pallas_call(
        _swa_attn_kernel,
        out_shape=jax.ShapeDtypeStruct((b, s, HQ), BF16),
        grid=(b, nt),
        in_specs=[pl.BlockSpec(memory_space=pltpu.SMEM),
                  cur(HQ), prev(HKV2), cur(HKV2), prev(HKV2), cur(HKV2)],
        out_specs=cur(HQ),
        compiler_params=_cparams("parallel", "parallel"),
        name="swa_attn",
    )(sinks, q, kk, kk, vv, vv)


def _fox_proj_kernel(x_ref, w_ref, bf_ref, q_ref, k_ref, v_ref, qc_ref, d_ref, carry_ref):
    @pl.when(pl.program_id(1) == 0)
    def _():
        carry_ref[...] = jnp.zeros_like(carry_ref)

    acc = _dot(x_ref[0].astype(BF16), w_ref[...])
    q_ref[0] = (acc[:, :HQ] * ATTN_SCALE).astype(BF16)
    k_ref[0] = acc[:, HQ:2 * HQ].astype(BF16)
    v_ref[0] = acc[:, 2 * HQ:3 * HQ].astype(BF16)
    qc_ref[0] = (acc[:, 3 * HQ:3 * HQ + HM] * ATTN_SCALE).astype(BF16)
    z = acc[:, 3 * HQ + HM:] + bf_ref[...]
    log_f = jnp.minimum(z, 0.0) - jnp.log(1.0 + jnp.exp(-jnp.abs(z)))
    tm = z.shape[0]
    tri = (lax.broadcasted_iota(jnp.int32, (tm, tm), 0)
           >= lax.broadcasted_iota(jnp.int32, (tm, tm), 1)).astype(F32).astype(BF16)
    hi, mid, lo = _split3(log_f)
    cum = (_dot(tri, lo) + _dot(tri, mid)) + _dot(tri, hi) + carry_ref[...]
    d_ref[0] = cum
    carry_ref[...] = cum[tm - 1:tm, :]


def _fox_proj(x3, w, bf_row):
    b, s, _ = x3.shape
    tm = min(s, FOX_PROJ_TM)
    row = lambda n: pl.BlockSpec((1, tm, n), lambda bi, i: (bi, i, 0))
    sds = lambda n, dt: jax.ShapeDtypeStruct((b, s, n), dt)
    return pl.pallas_call(
        _fox_proj_kernel,
        out_shape=(sds(HQ, BF16), sds(HQ, BF16), sds(HQ, BF16), sds(HM, BF16), sds(LANES, F32)),
        grid=(b, s // tm),
        in_specs=[row(D_MODEL), pl.BlockSpec((D_MODEL, FOX_COLS), lambda bi, i: (0, 0)),
                  pl.BlockSpec((1, LANES), lambda bi, i: (0, 0))],
        out_specs=(row(HQ), row(HQ), row(HQ), row(HM), row(LANES)),
        scratch_shapes=[pltpu.VMEM((1, LANES), F32)],
        compiler_params=_cparams("arbitrary", "arbitrary"),
        name="fox_proj",
    )(x3, w, bf_row)


def _fox_attn_kernel(q_ref, k_ref, v_ref, dcol_ref, drow_ref, o_ref):
    qi = pl.program_id(2)
    tq = FOX_TQ
    lane = lax.broadcasted_iota(jnp.int32, (tq, LANES), 1)
    qf = q_ref[0].astype(F32)
    qa = (jnp.where(lane < HEAD_DIM, qf, 0.0).astype(BF16),
          jnp.where(lane >= HEAD_DIM, qf, 0.0).astype(BF16))
    dcol = (dcol_ref[0, 0, 0], dcol_ref[0, 0, 1])
    causal = (lax.broadcasted_iota(jnp.int32, (tq, tq), 0)
              >= lax.broadcasted_iota(jnp.int32, (tq, tq), 1))

    def tile(kb, carry, masked):
        start = pl.multiple_of(kb * tq, tq)
        kblk = k_ref[0, pl.ds(start, tq), :]
        vblk = v_ref[0, pl.ds(start, tq), :]
        out = []
        for a in range(2):
            m, l, acc = carry[a]
            drow = drow_ref[0, 0, a, pl.ds(kb, 1), :]
            s = _dot_nt(qa[a], kblk) + (dcol[a] - drow)
            if masked:
                s = jnp.where(causal, s, NEG)
            m_new = jnp.maximum(m, jnp.max(s, axis=-1, keepdims=True))
            alpha = jnp.exp(m - m_new)
            p = jnp.exp(s - m_new)
            l = alpha * l + jnp.sum(p, axis=-1, keepdims=True)
            acc = alpha * acc + _dot(p.astype(BF16), vblk)
            out.append((m_new, l, acc))
        return tuple(out)

    init = tuple((jnp.full((tq, 1), NEG, F32), jnp.zeros((tq, 1), F32),
                  jnp.zeros((tq, LANES), F32)) for _ in range(2))
    carry = lax.fori_loop(0, qi, lambda kb, c: tile(kb, c, False), init)
    (_, l0, acc0), (_, l1, acc1) = tile(qi, carry, True)
    o = jnp.where(lane < HEAD_DIM, acc0 * (1.0 / l0), acc1 * (1.0 / l1))
    o_ref[0] = o.astype(BF16)


def _fox_attn(q, k, v, dcol, drow):
    b, s, _ = q.shape
    npair = HQ // LANES
    nt = s // FOX_TQ
    return pl.pallas_call(
        _fox_attn_kernel,
        out_shape=jax.ShapeDtypeStruct((b, s, HQ), BF16),
        grid=(b, npair, nt),
        in_specs=[pl.BlockSpec((1, FOX_TQ, LANES), lambda bi, j, i: (bi, i, j)),
                  pl.BlockSpec((1, s, LANES), lambda bi, j, i: (bi, 0, j)),
                  pl.BlockSpec((1, s, LANES), lambda bi, j, i: (bi, 0, j)),
                  pl.BlockSpec((1, 1, 2, FOX_TQ, 1), lambda bi, j, i: (bi, j, 0, i, 0)),
                  pl.BlockSpec((1, 1, 2, nt, FOX_TQ), lambda bi, j, i: (bi, j, 0, 0, 0))],
        out_specs=pl.BlockSpec((1, FOX_TQ, LANES), lambda bi, j, i: (bi, i, j)),
        compiler_params=_cparams("parallel", "parallel", "arbitrary"),
        name="fox_attn",
    )(q, k, v, dcol, drow)


def _mem_attn_kernel(q_ref, kv_ref, o_ref):
    lane = lax.broadcasted_iota(jnp.int32, (MEM_TQ, LANES), 1)
    for pair in range(HM // LANES):
        cols = slice(pair * LANES, (pair + 1) * LANES)
        kblk = kv_ref[0, :, cols]
        vblk = kv_ref[0, :, HM + pair * LANES:HM + (pair + 1) * LANES]
        qf = q_ref[0, :, cols].astype(F32)
        outs = []
        for a in range(2):
            keep = (lane < HEAD_DIM) if a == 0 else (lane >= HEAD_DIM)
            s = _dot_nt(jnp.where(keep, qf, 0.0).astype(BF16), kblk)
            p = jnp.exp(s - jnp.max(s, axis=-1, keepdims=True))
            l = jnp.sum(p, axis=-1, keepdims=True)
            outs.append(_dot(p.astype(BF16), vblk) * (1.0 / l))
        o_ref[0, :, cols] = jnp.where(lane < HEAD_DIM, outs[0], outs[1]).astype(BF16)


def _mem_attn(qc, kvm):
    b, s, _ = qc.shape
    return pl.pallas_call(
        _mem_attn_kernel,
        out_shape=jax.ShapeDtypeStruct((b, s, HM), BF16),
        grid=(b, s // MEM_TQ),
        in_specs=[pl.BlockSpec((1, MEM_TQ, HM), lambda bi, i: (bi, i, 0)),
                  pl.BlockSpec((1, MEM_LEN, 2 * HM), lambda bi, i: (bi, 0, 0))],
        out_specs=pl.BlockSpec((1, MEM_TQ, HM), lambda bi, i: (bi, i, 0)),
        compiler_params=_cparams("parallel", "parallel"),
        name="mem_attn",
    )(qc, kvm)


def _out_proj_kernel(s_ref, c_ref, x_ref, w_ref, g_ref, b_ref, o_ref):
    mix = _dot(s_ref[...], w_ref[:HQ, :]) + _dot(c_ref[...], w_ref[HQ:, :])
    o_ref[...] = _layer_norm(DN_ALPHA * x_ref[...] + mix, g_ref[...], b_ref[...])


def _out_proj_ln(self_out, cross_out, xf, w, g, b):
    t = xf.shape[0]
    tm = min(t, PROJ_TM)
    row = lambda n: pl.BlockSpec((tm, n), lambda i: (i, 0))
    const = lambda r, c: pl.BlockSpec((r, c), lambda i: (0, 0))
    return pl.pallas_call(
        _out_proj_kernel,
        out_shape=jax.ShapeDtypeStruct((t, D_MODEL), F32),
        grid=(t // tm,),
        in_specs=[row(HQ), row(HM), row(D_MODEL), const(D_MODEL, D_MODEL),
                  const(1, D_MODEL), const(1, D_MODEL)],
        out_specs=row(D_MODEL),
        compiler_params=_cparams("parallel"),
        name="out_proj_ln",
    )(self_out, cross_out, xf, w, g, b)


def _swiglu(xb, w_in_ref, w_out_ref):
    acc = None
    for c in range(D_FF // FF_CHUNK):
        lo = c * FF_CHUNK
        ha = _dot(xb, w_in_ref[:, lo:lo + FF_CHUNK])
        hb = _dot(xb, w_in_ref[:, D_FF + lo:D_FF + lo + FF_CHUNK])
        act = (ha * (1.0 / (1.0 + jnp.exp(-ha))) * hb).astype(BF16)
        part = _dot(act, w_out_ref[lo:lo + FF_CHUNK, :])
        acc = part if acc is None else acc + part
    return acc


def _dense_ffn_kernel(x_ref, w_in_ref, w_out_ref, g_ref, b_ref, o_ref):
    x = x_ref[...]
    ffn = _swiglu(x.astype(BF16), w_in_ref, w_out_ref)
    o_ref[...] = _layer_norm(DN_ALPHA * x + ffn, g_ref[...], b_ref[...])


def _dense_ffn_ln(xf, w_in, w_out, g, b):
    t = xf.shape[0]
    tm = min(t, FFN_TM)
    row = pl.BlockSpec((tm, D_MODEL), lambda i: (i, 0))
    const = lambda r, c: pl.BlockSpec((r, c), lambda i: (0, 0))
    return pl.pallas_call(
        _dense_ffn_kernel,
        out_shape=jax.ShapeDtypeStruct((t, D_MODEL), F32),
        grid=(t // tm,),
        in_specs=[row, const(D_MODEL, 2 * D_FF), const(D_FF, D_MODEL),
                  const(1, D_MODEL), const(1, D_MODEL)],
        out_specs=row,
        compiler_params=_cparams("parallel"),
        name="dense_ffn_ln",
    )(xf, w_in, w_out, g, b)


def _expert_kernel(be_ref, bv_ref, xs_ref, w_in_ref, w_out_ref, ys_ref):
    del be_ref
    valid = bv_ref[pl.program_id(0)] > 0

    @pl.when(valid)
    def _():
        ys_ref[...] = _swiglu(xs_ref[...].astype(BF16), w_in_ref.at[0], w_out_ref.at[0])

    @pl.when(jnp.logical_not(valid))
    def _():
        ys_ref[...] = jnp.zeros_like(ys_ref)


def _expert_ffn(xs, w_in, w_out, block_expert, block_valid):
    n_slots = xs.shape[0]
    row = pl.BlockSpec((MOE_TM, D_MODEL), lambda i, be, bv: (i, 0))
    return pl.pallas_call(
        _expert_kernel,
        out_shape=jax.ShapeDtypeStruct((n_slots, D_MODEL), F32),
        grid_spec=pltpu.PrefetchScalarGridSpec(
            num_scalar_prefetch=2,
            grid=(n_slots // MOE_TM,),
            in_specs=[row,
                      pl.BlockSpec((1, D_MODEL, 2 * D_FF), lambda i, be, bv: (be[i], 0, 0)),
                      pl.BlockSpec((1, D_FF, D_MODEL), lambda i, be, bv: (be[i], 0, 0))],
            out_specs=row),
        compiler_params=_cparams("arbitrary"),
        name="expert_ffn",
    )(block_expert, block_valid, xs, w_in, w_out)


ROUTE_E1, ROUTE_E2, ROUTE_R1, ROUTE_R2, ROUTE_G1, ROUTE_G2 = range(6)


def _router_kernel(x_ref, w_ref, b_ref, route_ref, count_ref, carry_ref):
    @pl.when(pl.program_id(0) == 0)
    def _():
        carry_ref[...] = jnp.zeros_like(carry_ref)

    xs, ws = _split3(x_ref[...]), _split3(w_ref[...])
    logits = None
    for i, j in ((2, 0), (1, 1), (0, 2), (1, 0), (0, 1), (0, 0)):
        term = _dot(xs[i], ws[j])
        logits = term if logits is None else logits + term
    tm = logits.shape[0]
    lane = lax.broadcasted_iota(jnp.int32, (tm, LANES), 1).astype(F32)
    logits = jnp.where(lane < N_EXPERTS, logits + b_ref[...], -jnp.inf)
    m1 = jnp.max(logits, axis=-1, keepdims=True)
    e1 = jnp.min(jnp.where(logits == m1, lane, float(LANES)), axis=-1, keepdims=True)
    rest = jnp.where(lane == e1, -jnp.inf, logits)
    m2 = jnp.max(rest, axis=-1, keepdims=True)
    e2 = jnp.min(jnp.where(rest == m2, lane, float(LANES)), axis=-1, keepdims=True)
    ex = jnp.exp(m2 - m1)
    g1 = 1.0 / (1.0 + ex)
    g2 = ex * g1
    sel1, sel2 = lane == e1, lane == e2
    onehot = jnp.where(sel1 | sel2, 1.0, 0.0)
    tri = (lax.broadcasted_iota(jnp.int32, (tm, tm), 0)
           > lax.broadcasted_iota(jnp.int32, (tm, tm), 1)).astype(F32).astype(BF16)
    rank = _dot(tri, onehot.astype(BF16)) + carry_ref[...]
    r1 = jnp.sum(jnp.where(sel1, rank, 0.0), axis=-1, keepdims=True)
    r2 = jnp.sum(jnp.where(sel2, rank, 0.0), axis=-1, keepdims=True)
    total = carry_ref[...] + jnp.sum(onehot, axis=0, keepdims=True)
    carry_ref[...] = total
    count_ref[...] = total
    route = jnp.zeros((tm, LANES), F32)
    for ln, val in ((ROUTE_E1, e1), (ROUTE_E2, e2), (ROUTE_R1, r1), (ROUTE_R2, r2),
                    (ROUTE_G1, g1), (ROUTE_G2, g2)):
        route = jnp.where(lane == float(ln), val, route)
    route_ref[...] = route


def _router(xf, w_pad, b_pad):
    t = xf.shape[0]
    tm = min(t, ROUTER_TM)
    return pl.pallas_call(
        _router_kernel,
        out_shape=(jax.ShapeDtypeStruct((t, LANES), F32), jax.ShapeDtypeStruct((1, LANES), F32)),
        grid=(t // tm,),
        in_specs=[pl.BlockSpec((tm, D_MODEL), lambda i: (i, 0)),
                  pl.BlockSpec((D_MODEL, LANES), lambda i: (0, 0)),
                  pl.BlockSpec((1, LANES), lambda i: (0, 0))],
        out_specs=(pl.BlockSpec((tm, LANES), lambda i: (i, 0)),
                   pl.BlockSpec((1, LANES), lambda i: (0, 0))),
        scratch_shapes=[pltpu.VMEM((1, LANES), F32)],
        compiler_params=_cparams("arbitrary"),
        name="router",
    )(xf, w_pad, b_pad)


def _row_copies(n, make_copy):
    def start(r, c):
        make_copy(r, 0).start()
        make_copy(r, 1).start()
        return c

    def wait(r, c):
        make_copy(r, 0).wait()
        make_copy(r, 1).wait()
        return c

    lax.fori_loop(0, n, start, 0)
    lax.fori_loop(0, n, wait, 0)


def _dispatch_kernel(dest_ref, x_hbm, xs_in_hbm, xs_hbm, sem):
    del xs_in_hbm
    base = pl.program_id(0) * ROW_TM

    def make_copy(r, k):
        tok = base + r
        return pltpu.make_async_copy(x_hbm.at[pl.ds(tok, 1)],
                                     xs_hbm.at[pl.ds(dest_ref[2 * tok + k], 1)], sem)

    _row_copies(ROW_TM, make_copy)


def _dispatch(xf, dest, n_slots):
    t = xf.shape[0]
    xs0 = jnp.zeros((n_slots, D_MODEL), F32)
    return pl.pallas_call(
        _dispatch_kernel,
        out_shape=jax.ShapeDtypeStruct((n_slots, D_MODEL), F32),
        grid_spec=pltpu.PrefetchScalarGridSpec(
            num_scalar_prefetch=1,
            grid=(t // ROW_TM,),
            in_specs=[pl.BlockSpec(memory_space=pl.ANY), pl.BlockSpec(memory_space=pl.ANY)],
            out_specs=pl.BlockSpec(memory_space=pl.ANY),
            scratch_shapes=[pltpu.SemaphoreType.DMA(())]),
        input_output_aliases={2: 0},
        compiler_params=_cparams("arbitrary"),
        name="moe_dispatch",
    )(dest, xf, xs0)


def _combine_kernel(dest_ref, x_ref, route_ref, g_ref, b_ref, ys_hbm, o_ref, ya, yb, sem):
    base = pl.program_id(0) * ROW_TM

    def make_copy(r, k):
        dst = ya if k == 0 else yb
        return pltpu.make_async_copy(ys_hbm.at[pl.ds(dest_ref[2 * (base + r) + k], 1)],
                                     dst.at[pl.ds(r, 1)], sem)

    _row_copies(ROW_TM, make_copy)
    route = route_ref[...]
    g1 = route[:, ROUTE_G1:ROUTE_G1 + 1]
    g2 = route[:, ROUTE_G2:ROUTE_G2 + 1]
    y = ya[...] * g1 + yb[...] * g2
    o_ref[...] = _layer_norm(DN_ALPHA * x_ref[...] + y, g_ref[...], b_ref[...])


def _combine_ln(xf, route, ys, dest, g, b):
    t = xf.shape[0]
    row = lambda n: pl.BlockSpec((ROW_TM, n), lambda i, d: (i, 0))
    const = pl.BlockSpec((1, D_MODEL), lambda i, d: (0, 0))
    return pl.pallas_call(
        _combine_kernel,
        out_shape=jax.ShapeDtypeStruct((t, D_MODEL), F32),
        grid_spec=pltpu.PrefetchScalarGridSpec(
            num_scalar_prefetch=1,
            grid=(t // ROW_TM,),
            in_specs=[row(D_MODEL), row(LANES), const, const, pl.BlockSpec(memory_space=pl.ANY)],
            out_specs=row(D_MODEL),
            scratch_shapes=[pltpu.VMEM((ROW_TM, D_MODEL), F32), pltpu.VMEM((ROW_TM, D_MODEL), F32),
                            pltpu.SemaphoreType.DMA(())]),
        compiler_params=_cparams("arbitrary"),
        name="moe_combine_ln",
    )(dest, xf, route, g, b, ys)


def _moe_ffn_ln(xf, w_router, b_router, w_in, w_out, g, b):
    t = xf.shape[0]
    w_pad = jnp.zeros((D_MODEL, LANES), F32).at[:, :N_EXPERTS].set(w_router)
    b_pad = jnp.zeros((1, LANES), F32).at[0, :N_EXPERTS].set(b_router)
    route, counts = _router(xf, w_pad, b_pad)
    counts = counts[0, :N_EXPERTS].astype(jnp.int32)
    pcounts = (counts + MOE_TM - 1) // MOE_TM * MOE_TM
    pends = jnp.cumsum(pcounts)
    pstarts = pends - pcounts
    e = route[:, ROUTE_E1:ROUTE_E2 + 1].astype(jnp.int32)
    r = route[:, ROUTE_R1:ROUTE_R2 + 1].astype(jnp.int32)
    dest = (pstarts[e] + r).reshape(-1)
    n_blocks = 2 * t // MOE_TM + N_EXPERTS
    blk_start = jnp.arange(n_blocks, dtype=jnp.int32) * MOE_TM
    block_expert = jnp.minimum(jnp.searchsorted(pends, blk_start, side="right"),
                               N_EXPERTS - 1).astype(jnp.int32)
    block_valid = (blk_start < pends[-1]).astype(jnp.int32)
    xs = _dispatch(xf, dest, n_blocks * MOE_TM)
    ys = _expert_ffn(xs, w_in, w_out, block_expert, block_valid)
    return _combine_ln(xf, route, ys, dest, g, b)


def _dup_heads(w):
    d, n = w.shape
    w = w.reshape(d, n // HEAD_DIM, 1, HEAD_DIM)
    return jnp.broadcast_to(w, (d, n // HEAD_DIM, 2, HEAD_DIM)).reshape(d, 2 * n)


def kernel(x, mem, positions, w_in_swa, attn_sinks, w_in_fox, b_forget, w_mem_kv, w_out,
           ln_attn_g, ln_attn_b, w_ffn_in, w_ffn_out, w_router, b_router, w_exp_in,
           w_exp_out, ln_ffn_g, ln_ffn_b):
    b, s, d = x.shape
    t = b * s
    hkv = SWA_KV_HEADS * HEAD_DIM
    cos_t, sin_t = _rope_tables(positions)
    mem2d = mem.reshape(b * MEM_LEN, d)
    xf = x.reshape(t, d)
    for i in range(DEPTH):
        j = i // 2
        kvm = _mem_kv(mem2d, w_mem_kv[i].astype(BF16)).reshape(b, MEM_LEN, 2 * HM)
        if i % 2 == 0:
            w = w_in_swa[j]
            w = jnp.concatenate([w[:, :HQ], _dup_heads(w[:, HQ:HQ + hkv]),
                                 _dup_heads(w[:, HQ + hkv:HQ + 2 * hkv]), w[:, HQ + 2 * hkv:]],
                                axis=1).astype(BF16)
            q, kk, vv, qc = _swa_proj(xf, w, cos_t, sin_t)
            self_out = _swa_attn(q.reshape(b, s, HQ), kk.reshape(b, s, HKV2),
                                 vv.reshape(b, s, HKV2), attn_sinks[j])
            qc = qc.reshape(b, s, HM)
        else:
            w = jnp.pad(w_in_fox[j], ((0, 0), (0, FOX_COLS - w_in_fox.shape[-1]))).astype(BF16)
            bf_row = jnp.pad(b_forget[j], (0, LANES - SELF_HEADS))[None, :]
            q, k, v, qc, dcum = _fox_proj(xf.reshape(b, s, d), w, bf_row)
            dh = dcum[:, :, :SELF_HEADS].transpose(0, 2, 1)
            npair = SELF_HEADS // 2
            dcol = dh.reshape(b, npair, 2, s, 1)
            drow = dh.reshape(b, npair, 2, s // FOX_TQ, FOX_TQ)
            self_out = _fox_attn(q, k, v, dcol, drow)
        cross_out = _mem_attn(qc, kvm)
        xf = _out_proj_ln(self_out.reshape(t, HQ), cross_out.reshape(t, HM), xf,
                          w_out[i].astype(BF16), ln_attn_g[i][None, :], ln_attn_b[i][None, :])
        g, bb = ln_ffn_g[i][None, :], ln_ffn_b[i][None, :]
        if i % 2 == 0:
            xf = _dense_ffn_ln(xf, w_ffn_in[j].astype(BF16), w_ffn_out[j].astype(BF16), g, bb)
        else:
            xf = _moe_ffn_ln(xf, w_router[j], b_router[j], w_exp_in[j].astype(BF16),
                             w_exp_out[j].astype(BF16), g, bb)
    return xf.reshape(b, s, d)
```

```python
import functools

import jax
import jax.numpy as jnp
from jax import lax
from jax.experimental import pallas as pl
from jax.experimental.pallas import tpu as pltpu

D_MODEL = 1024
DEPTH = 4
CHUNK = 64
HEAD_DIM = 64
MEM_HEADS = 4
SELF_HEADS = 12
SWA_KV_HEADS = 3
SWA_GROUP = SELF_HEADS // SWA_KV_HEADS
WINDOW_CHUNKS = 2
ROT_DIM = HEAD_DIM // 4
ROT_HALF = ROT_DIM // 2
ROPE_THETA = 500000.0
MEM_LEN = 256
D_FF = 2816
N_EXPERTS = 8
LN_EPS = 1e-5
NEG = -1e30
ATTN_SCALE = HEAD_DIM ** -0.5
DN_ALPHA = (2.0 * DEPTH) ** 0.25

HQ = SELF_HEADS * HEAD_DIM
HM = MEM_HEADS * HEAD_DIM
HKV2 = 2 * SWA_KV_HEADS * HEAD_DIM
SWA_COLS = HQ + 2 * HKV2 + HM
FOX_COLS = 3 * HQ + HM + 128

LANES = 128
VMEM_LIMIT_BYTES = 56 * 1024 * 1024

PROJ_TM = 512
FOX_PROJ_TM = 256
SWA_TQ = 256
SWA_SUB = 128
FOX_TQ = 256
MEM_TQ = 512
FFN_TM = 256
FF_CHUNK = D_FF // 2
ROUTER_TM = 512
MOE_TM = 256
ROW_TM = 256

F32 = jnp.float32
BF16 = jnp.bfloat16


def _cparams(*sem):
    return pltpu.CompilerParams(dimension_semantics=sem, vmem_limit_bytes=VMEM_LIMIT_BYTES)


def _layer_norm(y, g, b):
    mu = jnp.mean(y, axis=-1, keepdims=True)
    d = y - mu
    var = jnp.mean(d * d, axis=-1, keepdims=True)
    return d * lax.rsqrt(var + LN_EPS) * g + b


def _split3(a):
    hi = a.astype(BF16)
    r1 = a - hi.astype(F32)
    mid = r1.astype(BF16)
    lo = (r1 - mid.astype(F32)).astype(BF16)
    return hi, mid, lo


def _dot(a, b):
    return jnp.dot(a, b, preferred_element_type=F32)


def _dot_nt(a, b):
    return lax.dot_general(a, b, (((1,), (1,)), ((), ())), preferred_element_type=F32)


def _rope_table_kernel(pos_ref, freq_ref, cos_ref, sin_ref):
    ang = pos_ref[...].astype(F32) * freq_ref[...]
    sub = lax.broadcasted_iota(jnp.int32, ang.shape, 1) & (HEAD_DIM - 1)
    c, s = jnp.cos(ang), jnp.sin(ang)
    cos_ref[...] = jnp.where(sub < ROT_DIM, c, 1.0)
    sin_ref[...] = jnp.where(sub < ROT_HALF, -s, jnp.where(sub < ROT_DIM, s, 0.0))


def _rope_tables(positions):
    t = positions.size
    tm = min(t, 2048)
    inv_freq = ROPE_THETA ** (-jnp.arange(0, ROT_DIM, 2, dtype=F32) / ROT_DIM)
    lane = jnp.arange(LANES) % HEAD_DIM % ROT_HALF
    freq = inv_freq[lane][None, :]
    out = jax.ShapeDtypeStruct((t, LANES), F32)
    return pl.pallas_call(
        _rope_table_kernel,
        out_shape=(out, out),
        grid=(t // tm,),
        in_specs=[pl.BlockSpec((tm, 1), lambda i: (i, 0)),
                  pl.BlockSpec((1, LANES), lambda i: (0, 0))],
        out_specs=(pl.BlockSpec((tm, LANES), lambda i: (i, 0)),
                   pl.BlockSpec((tm, LANES), lambda i: (i, 0))),
        compiler_params=_cparams("parallel"),
        name="rope_tables",
    )(positions.reshape(t, 1), freq)


def _mem_kv_kernel(m_ref, w_ref, o_ref):
    o_ref[...] = _dot(m_ref[...].astype(BF16), w_ref[...]).astype(BF16)


def _mem_kv(mem2d, w):
    n = mem2d.shape[0]
    tm = min(n, PROJ_TM)
    return pl.pallas_call(
        _mem_kv_kernel,
        out_shape=jax.ShapeDtypeStruct((n, 2 * HM), BF16),
        grid=(n // tm,),
        in_specs=[pl.BlockSpec((tm, D_MODEL), lambda i: (i, 0)),
                  pl.BlockSpec((D_MODEL, 2 * HM), lambda i: (0, 0))],
        out_specs=pl.BlockSpec((tm, 2 * HM), lambda i: (i, 0)),
        compiler_params=_cparams("parallel"),
        name="mem_kv_proj",
    )(mem2d, w)


def _swa_proj_kernel(x_ref, w_ref, cos_ref, sin_ref, q_ref, kk_ref, vv_ref, qc_ref):
    acc = _dot(x_ref[...].astype(BF16), w_ref[...])
    cos, sin = cos_ref[...], sin_ref[...]
    first = (lax.broadcasted_iota(jnp.int32, cos.shape, 1) & (HEAD_DIM - 1)) < ROT_HALF

    def rope(t):
        fwd = pltpu.roll(t, LANES - ROT_HALF, axis=1)
        bwd = pltpu.roll(t, ROT_HALF, axis=1)
        return t * cos + jnp.where(first, fwd, bwd) * sin

    for g in range(HQ // LANES):
        sl = slice(g * LANES, (g + 1) * LANES)
        q_ref[:, sl] = (rope(acc[:, sl]) * ATTN_SCALE).astype(BF16)
    for g in range(HKV2 // LANES):
        src = slice(HQ + g * LANES, HQ + (g + 1) * LANES)
        kk_ref[:, g * LANES:(g + 1) * LANES] = rope(acc[:, src]).astype(BF16)
    vv_ref[...] = acc[:, HQ + HKV2:HQ + 2 * HKV2].astype(BF16)
    qc_ref[...] = (acc[:, HQ + 2 * HKV2:] * ATTN_SCALE).astype(BF16)


def _swa_proj(xf, w, cos_t, sin_t):
    t = xf.shape[0]
    tm = min(t, PROJ_TM)
    row = lambda n: pl.BlockSpec((tm, n), lambda i: (i, 0))
    return pl.pallas_call(
        _swa_proj_kernel,
        out_shape=(jax.ShapeDtypeStruct((t, HQ), BF16), jax.ShapeDtypeStruct((t, HKV2), BF16),
                   jax.ShapeDtypeStruct((t, HKV2), BF16), jax.ShapeDtypeStruct((t, HM), BF16)),
        grid=(t // tm,),
        in_specs=[row(D_MODEL), pl.BlockSpec((D_MODEL, SWA_COLS), lambda i: (0, 0)),
                  row(LANES), row(LANES)],
        out_specs=(row(HQ), row(HKV2), row(HKV2), row(HM)),
        compiler_params=_cparams("parallel"),
        name="swa_proj",
    )(xf, w, cos_t, sin_t)


def _swa_attn_kernel(sink_ref, q_ref, kp_ref, kc_ref, vp_ref, vc_ref, o_ref):
    first_tile = pl.program_id(1) == 0
    half = SWA_TQ - SWA_SUB
    lane = lax.broadcasted_iota(jnp.int32, (SWA_SUB, LANES), 1)
    shift = CHUNK.bit_length() - 1
    rc = lax.broadcasted_iota(jnp.int32, (SWA_SUB, 2 * SWA_SUB), 0) >> shift
    cc = lax.broadcasted_iota(jnp.int32, (SWA_SUB, 2 * SWA_SUB), 1) >> shift
    band = (cc >= rc) & (cc <= rc + WINDOW_CHUNKS)
    band_prev = band & ((cc >= SWA_SUB // CHUNK) | jnp.logical_not(first_tile))
    for sub in range(SWA_TQ // SWA_SUB):
        rows = slice(sub * SWA_SUB, (sub + 1) * SWA_SUB)
        mask = band_prev if sub == 0 else band
        for h in range(SWA_KV_HEADS):
            cols = slice(h * LANES, (h + 1) * LANES)
            if sub == 0:
                kcat = jnp.concatenate([kp_ref[0, half:, cols], kc_ref[0, :SWA_SUB, cols]], axis=0)
                vcat = jnp.concatenate([vp_ref[0, half:, cols], vc_ref[0, :SWA_SUB, cols]], axis=0)
            else:
                kcat = kc_ref[0, :, cols]
                vcat = vc_ref[0, :, cols]
            for pair in range(SWA_GROUP // 2):
                g = h * (SWA_GROUP // 2) + pair
                qcols = slice(g * LANES, (g + 1) * LANES)
                qf = q_ref[0, rows, qcols].astype(F32)
                outs = []
                for a in range(2):
                    keep = (lane < HEAD_DIM) if a == 0 else (lane >= HEAD_DIM)
                    qa = jnp.where(keep, qf, 0.0).astype(BF16)
                    s = jnp.where(mask, _dot_nt(qa, kcat), NEG)
                    sink = sink_ref[2 * g + a]
                    m = jnp.maximum(jnp.max(s, axis=-1, keepdims=True), sink)
                    p = jnp.exp(s - m)
                    den = jnp.sum(p, axis=-1, keepdims=True) + jnp.exp(sink - m)
                    probs = (p * (1.0 / den)).astype(BF16)
                    outs.append(_dot(probs, vcat))
                o = jnp.where(lane < HEAD_DIM, outs[0], outs[1])
                o_ref[0, rows, qcols] = o.astype(BF16)


def _swa_attn(q, kk, vv, sinks):
    b, s, _ = q.shape
    nt = s // SWA_TQ
    cur = lambda n: pl.BlockSpec((1, SWA_TQ, n), lambda bi, i: (bi, i, 0))
    prev = lambda n: pl.BlockSpec((1, SWA_TQ, n), lambda bi, i: (bi, jnp.maximum(i - 1, 0), 0))
    return pl.pallas_call(
        _swa_attn_kernel,
        out_shape=jax.ShapeDtypeStruct((b, s, HQ), BF16),
        grid=(b, nt),
        in_specs=[pl.BlockSpec(memory_space=pltpu.SMEM),
                  cur(HQ), prev(HKV2), cur(HKV2), prev(HKV2), cur(HKV2)],
        out_specs=cur(HQ),
        compiler_params=_cparams("parallel", "parallel"),
        name="swa_attn",
    )(sinks, q, kk, kk, vv, vv)


def _fox_proj_kernel(x_ref, w_ref, bf_ref, q_ref, k_ref, v_ref, qc_ref, d_ref, carry_ref):
    @pl.when(pl.program_id(1) == 0)
    def _():
        carry_ref[...] = jnp.zeros_like(carry_ref)

    acc = _dot(x_ref[0].astype(BF16), w_ref[...])
    q_ref[0] = (acc[:, :HQ] * ATTN_SCALE).astype(BF16)
    k_ref[0] = acc[:, HQ:2 * HQ].astype(BF16)
    v_ref[0] = acc[:, 2 * HQ:3 * HQ].astype(BF16)
    qc_ref[0] = (acc[:, 3 * HQ:3 * HQ + HM] * ATTN_SCALE).astype(BF16)
    z = acc[:, 3 * HQ + HM:] + bf_ref[...]
    log_f = jnp.minimum(z, 0.0) - jnp.log(1.0 + jnp.exp(-jnp.abs(z)))
    tm = z.shape[0]
    tri = (lax.broadcasted_iota(jnp.int32, (tm, tm), 0)
           >= lax.broadcasted_iota(jnp.int32, (tm, tm), 1)).astype(F32).astype(BF16)
    hi, mid, lo = _split3(log_f)
    cum = (_dot(tri, lo) + _dot(tri, mid)) + _dot(tri, hi) + carry_ref[...]
    d_ref[0] = cum
    carry_ref[...] = cum[tm - 1:tm, :]


def _fox_proj(x3, w, bf_row):
    b, s, _ = x3.shape
    tm = min(s, FOX_PROJ_TM)
    row = lambda n: pl.BlockSpec((1, tm, n), lambda bi, i: (bi, i, 0))
    sds = lambda n, dt: jax.ShapeDtypeStruct((b, s, n), dt)
    return pl.pallas_call(
        _fox_proj_kernel,
        out_shape=(sds(HQ, BF16), sds(HQ, BF16), sds(HQ, BF16), sds(HM, BF16), sds(LANES, F32)),
        grid=(b, s // tm),
        in_specs=[row(D_MODEL), pl.BlockSpec((D_MODEL, FOX_COLS), lambda bi, i: (0, 0)),
                  pl.BlockSpec((1, LANES), lambda bi, i: (0, 0))],
        out_specs=(row(HQ), row(HQ), row(HQ), row(HM), row(LANES)),
        scratch_shapes=[pltpu.VMEM((1, LANES), F32)],
        compiler_params=_cparams("arbitrary", "arbitrary"),
        name="fox_proj",
    )(x3, w, bf_row)


def _fox_attn_kernel(q_ref, k_ref, v_ref, dcol_ref, drow_ref, o_ref):
    qi = pl.program_id(2)
    tq = FOX_TQ
    lane = lax.broadcasted_iota(jnp.int32, (tq, LANES), 1)
    qf = q_ref[0].astype(F32)
    qa = (jnp.where(lane < HEAD_DIM, qf, 0.0).astype(BF16),
          jnp.where(lane >= HEAD_DIM, qf, 0.0).astype(BF16))
    dcol = (dcol_ref[0, 0, 0], dcol_ref[0, 0, 1])
    causal = (lax.broadcasted_iota(jnp.int32, (tq, tq), 0)
              >= lax.broadcasted_iota(jnp.int32, (tq, tq), 1))

    def tile(kb, carry, masked):
        start = pl.multiple_of(kb * tq, tq)
        kblk = k_ref[0, pl.ds(start, tq), :]
        vblk = v_ref[0, pl.ds(start, tq), :]
        out = []
        for a in range(2):
            m, l, acc = carry[a]
            drow = drow_ref[0, 0, a, pl.ds(kb, 1), :]
            s = _dot_nt(qa[a], kblk) + (dcol[a] - drow)
            if masked:
                s = jnp.where(causal, s, NEG)
            m_new = jnp.maximum(m, jnp.max(s, axis=-1, keepdims=True))
            alpha = jnp.exp(m - m_new)
            p = jnp.exp(s - m_new)
            l = alpha * l + jnp.sum(p, axis=-1, keepdims=True)
            acc = alpha * acc + _dot(p.astype(BF16), vblk)
            out.append((m_new, l, acc))
        return tuple(out)

    init = tuple((jnp.full((tq, 1), NEG, F32), jnp.zeros((tq, 1), F32),
                  jnp.zeros((tq, LANES), F32)) for _ in range(2))
    carry = lax.fori_loop(0, qi, lambda kb, c: tile(kb, c, False), init)
    (_, l0, acc0), (_, l1, acc1) = tile(qi, carry, True)
    o = jnp.where(lane < HEAD_DIM, acc0 * (1.0 / l0), acc1 * (1.0 / l1))
    o_ref[0] = o.astype(BF16)


def _fox_attn(q, k, v, dcol, drow):
    b, s, _ = q.shape
    npair = HQ // LANES
    nt = s // FOX_TQ
    return pl.pallas_call(
        _fox_attn_kernel,
        out_shape=jax.ShapeDtypeStruct((b, s, HQ), BF16),
        grid=(b, npair, nt),
        in_specs=[pl.BlockSpec((1, FOX_TQ, LANES), lambda bi, j, i: (bi, i, j)),
                  pl.BlockSpec((1, s, LANES), lambda bi, j, i: (bi, 0, j)),
                  pl.BlockSpec((1, s, LANES), lambda bi, j, i: (bi, 0, j)),
                  pl.BlockSpec((1, 1, 2, FOX_TQ, 1), lambda bi, j, i: (bi, j, 0, i, 0)),
                  pl.BlockSpec((1, 1, 2, nt, FOX_TQ), lambda bi, j, i: (bi, j, 0, 0, 0))],
        out_specs=pl.BlockSpec((1, FOX_TQ, LANES), lambda bi, j, i: (bi, i, j)),
        compiler_params=_cparams("parallel", "parallel", "arbitrary"),
        name="fox_attn",
    )(q, k, v, dcol, drow)


def _mem_attn_kernel(q_ref, kv_ref, o_ref):
    lane = lax.broadcasted_iota(jnp.int32, (MEM_TQ, LANES), 1)
    for pair in range(HM // LANES):
        cols = slice(pair * LANES, (pair + 1) * LANES)
        kblk = kv_ref[0, :, cols]
        vblk = kv_ref[0, :, HM + pair * LANES:HM + (pair + 1) * LANES]
        qf = q_ref[0, :, cols].astype(F32)
        outs = []
        for a in range(2):
            keep = (lane < HEAD_DIM) if a == 0 else (lane >= HEAD_DIM)
            s = _dot_nt(jnp.where(keep, qf, 0.0).astype(BF16), kblk)
            p = jnp.exp(s - jnp.max(s, axis=-1, keepdims=True))
            l = jnp.sum(p, axis=-1, keepdims=True)
            outs.append(_dot(p.astype(BF16), vblk) * (1.0 / l))
        o_ref[0, :, cols] = jnp.where(lane < HEAD_DIM, outs[0], outs[1]).astype(BF16)


def _mem_attn(qc, kvm):
    b, s, _ = qc.shape
    return pl.pallas_call(
        _mem_attn_kernel,
        out_shape=jax.ShapeDtypeStruct((b, s, HM), BF16),
        grid=(b, s // MEM_TQ),
        in_specs=[pl.BlockSpec((1, MEM_TQ, HM), lambda bi, i: (bi, i, 0)),
                  pl.BlockSpec((1, MEM_LEN, 2 * HM), lambda bi, i: (bi, 0, 0))],
        out_specs=pl.BlockSpec((1, MEM_TQ, HM), lambda bi, i: (bi, i, 0)),
        compiler_params=_cparams("parallel", "parallel"),
        name="mem_attn",
    )(qc, kvm)


def _out_proj_kernel(s_ref, c_ref, x_ref, w_ref, g_ref, b_ref, o_ref):
    mix = _dot(s_ref[...], w_ref[:HQ, :]) + _dot(c_ref[...], w_ref[HQ:, :])
    o_ref[...] = _layer_norm(DN_ALPHA * x_ref[...] + mix, g_ref[...], b_ref[...])


def _out_proj_ln(self_out, cross_out, xf, w, g, b):
    t = xf.shape[0]
    tm = min(t, PROJ_TM)
    row = lambda n: pl.BlockSpec((tm, n), lambda i: (i, 0))
    const = lambda r, c: pl.BlockSpec((r, c), lambda i: (0, 0))
    return pl.pallas_call(
        _out_proj_kernel,
        out_shape=jax.ShapeDtypeStruct((t, D_MODEL), F32),
        grid=(t // tm,),
        in_specs=[row(HQ), row(HM), row(D_MODEL), const(D_MODEL, D_MODEL),
                  const(1, D_MODEL), const(1, D_MODEL)],
        out_specs=row(D_MODEL),
        compiler_params=_cparams("parallel"),
        name="out_proj_ln",
    )(self_out, cross_out, xf, w, g, b)


def _swiglu(xb, w_in_ref, w_out_ref):
    acc = None
    for c in range(D_FF // FF_CHUNK):
        lo = c * FF_CHUNK
        ha = _dot(xb, w_in_ref[:, lo:lo + FF_CHUNK])
        hb = _dot(xb, w_in_ref[:, D_FF + lo:D_FF + lo + FF_CHUNK])
        act = (ha * (1.0 / (1.0 + jnp.exp(-ha))) * hb).astype(BF16)
        part = _dot(act, w_out_ref[lo:lo + FF_CHUNK, :])
        acc = part if acc is None else acc + part
    return acc


def _dense_ffn_kernel(x_ref, w_in_ref, w_out_ref, g_ref, b_ref, o_ref):
    x = x_ref[...]
    ffn = _swiglu(x.astype(BF16), w_in_ref, w_out_ref)
    o_ref[...] = _layer_norm(DN_ALPHA * x + ffn, g_ref[...], b_ref[...])


def _dense_ffn_ln(xf, w_in, w_out, g, b):
    t = xf.shape[0]
    tm = min(t, FFN_TM)
    row = pl.BlockSpec((tm, D_MODEL), lambda i: (i, 0))
    const = lambda r, c: pl.BlockSpec((r, c), lambda i: (0, 0))
    return pl.pallas_call(
        _dense_ffn_kernel,
        out_shape=jax.ShapeDtypeStruct((t, D_MODEL), F32),
        grid=(t // tm,),
        in_specs=[row, const(D_MODEL, 2 * D_FF), const(D_FF, D_MODEL),
                  const(1, D_MODEL), const(1, D_MODEL)],
        out_specs=row,
        compiler_params=_cparams("parallel"),
        name="dense_ffn_ln",
    )(xf, w_in, w_out, g, b)


def _expert_kernel(be_ref, bv_ref, xs_ref, w_in_ref, w_out_ref, ys_ref):
    del be_ref
    valid = bv_ref[pl.program_id(0)] > 0

    @pl.when(valid)
    def _():
        ys_ref[...] = _swiglu(xs_ref[...].astype(BF16), w_in_ref.at[0], w_out_ref.at[0])

    @pl.when(jnp.logical_not(valid))
    def _():
        ys_ref[...] = jnp.zeros_like(ys_ref)


def _expert_ffn(xs, w_in, w_out, block_expert, block_valid):
    n_slots = xs.shape[0]
    row = pl.BlockSpec((MOE_TM, D_MODEL), lambda i, be, bv: (i, 0))
    return pl.pallas_call(
        _expert_kernel,
        out_shape=jax.ShapeDtypeStruct((n_slots, D_MODEL), F32),
        grid_spec=pltpu.PrefetchScalarGridSpec(
            num_scalar_prefetch=2,
            grid=(n_slots // MOE_TM,),
            in_specs=[row,
                      pl.BlockSpec((1, D_MODEL, 2 * D_FF), lambda i, be, bv: (be[i], 0, 0)),
                      pl.BlockSpec((1, D_FF, D_MODEL), lambda i, be, bv: (be[i], 0, 0))],
            out_specs=row),
        compiler_params=_cparams("arbitrary"),
        name="expert_ffn",
    )(block_expert, block_valid, xs, w_in, w_out)


ROUTE_E1, ROUTE_E2, ROUTE_R1, ROUTE_R2, ROUTE_G1, ROUTE_G2 = range(6)


def _router_kernel(x_ref, w_ref, b_ref, route_ref, count_ref, carry_ref):
    @pl.when(pl.program_id(0) == 0)
    def _():
        carry_ref[...] = jnp.zeros_like(carry_ref)

    xs, ws = _split3(x_ref[...]), _split3(w_ref[...])
    logits = None
    for i, j in ((2, 0), (1, 1), (0, 2), (1, 0), (0, 1), (0, 0)):
        term = _dot(xs[i], ws[j])
        logits = term if logits is None else logits + term
    tm = logits.shape[0]
    lane = lax.broadcasted_iota(jnp.int32, (tm, LANES), 1).astype(F32)
    logits = jnp.where(lane < N_EXPERTS, logits + b_ref[...], -jnp.inf)
    m1 = jnp.max(logits, axis=-1, keepdims=True)
    e1 = jnp.min(jnp.where(logits == m1, lane, float(LANES)), axis=-1, keepdims=True)
    rest = jnp.where(lane == e1, -jnp.inf, logits)
    m2 = jnp.max(rest, axis=-1, keepdims=True)
    e2 = jnp.min(jnp.where(rest == m2, lane, float(LANES)), axis=-1, keepdims=True)
    ex = jnp.exp(m2 - m1)
    g1 = 1.0 / (1.0 + ex)
    g2 = ex * g1
    sel1, sel2 = lane == e1, lane == e2
    onehot = jnp.where(sel1 | sel2, 1.0, 0.0)
    tri = (lax.broadcasted_iota(jnp.int32, (tm, tm), 0)
           > lax.broadcasted_iota(jnp.int32, (tm, tm), 1)).astype(F32).astype(BF16)
    rank = _dot(tri, onehot.astype(BF16)) + carry_ref[...]
    r1 = jnp.sum(jnp.where(sel1, rank, 0.0), axis=-1, keepdims=True)
    r2 = jnp.sum(jnp.where(sel2, rank, 0.0), axis=-1, keepdims=True)
    total = carry_ref[...] + jnp.sum(onehot, axis=0, keepdims=True)
    carry_ref[...] = total
    count_ref[...] = total
    route = jnp.zeros((tm, LANES), F32)
    for ln, val in ((ROUTE_E1, e1), (ROUTE_E2, e2), (ROUTE_R1, r1), (ROUTE_R2, r2),
                    (ROUTE_G1, g1), (ROUTE_G2, g2)):
        route = jnp.where(lane == float(ln), val, route)
    route_ref[...] = route


def _router(xf, w_pad, b_pad):
    t = xf.shape[0]
    tm = min(t, ROUTER_TM)
    return pl.pallas_call(
        _router_kernel,
        out_shape=(jax.ShapeDtypeStruct((t, LANES), F32), jax.ShapeDtypeStruct((1, LANES), F32)),
        grid=(t // tm,),
        in_specs=[pl.BlockSpec((tm, D_MODEL), lambda i: (i, 0)),
                  pl.BlockSpec((D_MODEL, LANES), lambda i: (0, 0)),
                  pl.BlockSpec((1, LANES), lambda i: (0, 0))],
        out_specs=(pl.BlockSpec((tm, LANES), lambda i: (i, 0)),
                   pl.BlockSpec((1, LANES), lambda i: (0, 0))),
        scratch_shapes=[pltpu.VMEM((1, LANES), F32)],
        compiler_params=_cparams("arbitrary"),
        name="router",
    )(xf, w_pad, b_pad)


ROW_UNROLL = 8


def _row_copies(n, make_copy):
    def start(r, c):
        make_copy(r, 0).start()
        make_copy(r, 1).start()
        return c

    def wait(r, c):
        make_copy(r, 0).wait()
        make_copy(r, 1).wait()
        return c

    lax.fori_loop(0, n, start, 0, unroll=ROW_UNROLL)
    lax.fori_loop(0, n, wait, 0, unroll=ROW_UNROLL)


def _dispatch_kernel(dest_ref, x_ref, xs_in_hbm, xs_hbm, sem):
    del xs_in_hbm
    base = pl.program_id(0) * ROW_TM

    def make_copy(r, k):
        return pltpu.make_async_copy(x_ref.at[pl.ds(r, 1)],
                                     xs_hbm.at[pl.ds(dest_ref[2 * (base + r) + k], 1)], sem)

    _row_copies(ROW_TM, make_copy)


def _dispatch(xf, dest, n_slots):
    t = xf.shape[0]
    xs0 = jnp.zeros((n_slots, D_MODEL), F32)
    return pl.pallas_call(
        _dispatch_kernel,
        out_shape=jax.ShapeDtypeStruct((n_slots, D_MODEL), F32),
        grid_spec=pltpu.PrefetchScalarGridSpec(
            num_scalar_prefetch=1,
            grid=(t // ROW_TM,),
            in_specs=[pl.BlockSpec((ROW_TM, D_MODEL), lambda i, d: (i, 0)),
                      pl.BlockSpec(memory_space=pl.ANY)],
            out_specs=pl.BlockSpec(memory_space=pl.ANY),
            scratch_shapes=[pltpu.SemaphoreType.DMA(())]),
        input_output_aliases={2: 0},
        compiler_params=_cparams("arbitrary"),
        name="moe_dispatch",
    )(dest, xf, xs0)


def _combine_kernel(dest_ref, x_ref, route_ref, g_ref, b_ref, ys_hbm, o_ref, ya, yb, sem):
    base = pl.program_id(0) * ROW_TM

    def make_copy(r, k):
        dst = ya if k == 0 else yb
        return pltpu.make_async_copy(ys_hbm.at[pl.ds(dest_ref[2 * (base + r) + k], 1)],
                                     dst.at[pl.ds(r, 1)], sem)

    _row_copies(ROW_TM, make_copy)
    route = route_ref[...]
    g1 = route[:, ROUTE_G1:ROUTE_G1 + 1]
    g2 = route[:, ROUTE_G2:ROUTE_G2 + 1]
    y = ya[...] * g1 + yb[...] * g2
    o_ref[...] = _layer_norm(DN_ALPHA * x_ref[...] + y, g_ref[...], b_ref[...])


def _combine_ln(xf, route, ys, dest, g, b):
    t = xf.shape[0]
    row = lambda n: pl.BlockSpec((ROW_TM, n), lambda i, d: (i, 0))
    const = pl.BlockSpec((1, D_MODEL), lambda i, d: (0, 0))
    return pl.pallas_call(
        _combine_kernel,
        out_shape=jax.ShapeDtypeStruct((t, D_MODEL), F32),
        grid_spec=pltpu.PrefetchScalarGridSpec(
            num_scalar_prefetch=1,
            grid=(t // ROW_TM,),
            in_specs=[row(D_MODEL), row(LANES), const, const, pl.BlockSpec(memory_space=pl.ANY)],
            out_specs=row(D_MODEL),
            scratch_shapes=[pltpu.VMEM((ROW_TM, D_MODEL), F32), pltpu.VMEM((ROW_TM, D_MODEL), F32),
                            pltpu.SemaphoreType.DMA(())]),
        compiler_params=_cparams("arbitrary"),
        name="moe_combine_ln",
    )(dest, xf, route, g, b, ys)


def _moe_ffn_ln(xf, w_router, b_router, w_in, w_out, g, b):
    t = xf.shape[0]
    w_pad = jnp.zeros((D_MODEL, LANES), F32).at[:, :N_EXPERTS].set(w_router)
    b_pad = jnp.zeros((1, LANES), F32).at[0, :N_EXPERTS].set(b_router)
    route, counts = _router(xf, w_pad, b_pad)
    counts = counts[0, :N_EXPERTS].astype(jnp.int32)
    pcounts = (counts + MOE_TM - 1) // MOE_TM * MOE_TM
    pends = jnp.cumsum(pcounts)
    pstarts = pends - pcounts
    e = route[:, ROUTE_E1:ROUTE_E2 + 1].astype(jnp.int32)
    r = route[:, ROUTE_R1:ROUTE_R2 + 1].astype(jnp.int32)
    dest = (pstarts[e] + r).reshape(-1)
    n_blocks = 2 * t // MOE_TM + N_EXPERTS
    blk_start = jnp.arange(n_blocks, dtype=jnp.int32) * MOE_TM
    block_expert = jnp.minimum(jnp.sum(blk_start[:, None] >= pends[None, :], axis=1),
                               N_EXPERTS - 1).astype(jnp.int32)
    block_valid = (blk_start < pends[-1]).astype(jnp.int32)
    xs = _dispatch(xf, dest, n_blocks * MOE_TM)
    ys = _expert_ffn(xs, w_in, w_out, block_expert, block_valid)
    return _combine_ln(xf, route, ys, dest, g, b)


def _dup_heads(w):
    d, n = w.shape
    w = w.reshape(d, n // HEAD_DIM, 1, HEAD_DIM)
    return jnp.broadcast_to(w, (d, n // HEAD_DIM, 2, HEAD_DIM)).reshape(d, 2 * n)


def kernel(x, mem, positions, w_in_swa, attn_sinks, w_in_fox, b_forget, w_mem_kv, w_out,
           ln_attn_g, ln_attn_b, w_ffn_in, w_ffn_out, w_router, b_router, w_exp_in,
           w_exp_out, ln_ffn_g, ln_ffn_b):
    b, s, d = x.shape
    t = b * s
    hkv = SWA_KV_HEADS * HEAD_DIM
    cos_t, sin_t = _rope_tables(positions)
    mem2d = mem.reshape(b * MEM_LEN, d)
    xf = x.reshape(t, d)
    for i in range(DEPTH):
        j = i // 2
        kvm = _mem_kv(mem2d, w_mem_kv[i].astype(BF16)).reshape(b, MEM_LEN, 2 * HM)
        if i % 2 == 0:
            w = w_in_swa[j]
            w = jnp.concatenate([w[:, :HQ], _dup_heads(w[:, HQ:HQ + hkv]),
                                 _dup_heads(w[:, HQ + hkv:HQ + 2 * hkv]), w[:, HQ + 2 * hkv:]],
                                axis=1).astype(BF16)
            q, kk, vv, qc = _swa_proj(xf, w, cos_t, sin_t)
            self_out = _swa_attn(q.reshape(b, s, HQ), kk.reshape(b, s, HKV2),
                                 vv.reshape(b, s, HKV2), attn_sinks[j])
            qc = qc.reshape(b, s, HM)
        else:
            w = jnp.pad(w_in_fox[j], ((0, 0), (0, FOX_COLS - w_in_fox.shape[-1]))).astype(BF16)
            bf_row = jnp.pad(b_forget[j], (0, LANES - SELF_HEADS))[None, :]
            q, k, v, qc, dcum = _fox_proj(xf.reshape(b, s, d), w, bf_row)
            dh = dcum[:, :, :SELF_HEADS].transpose(0, 2, 1)
            npair = SELF_HEADS // 2
            dcol = dh.reshape(b, npair, 2, s, 1)
            drow = dh.reshape(b, npair, 2, s // FOX_TQ, FOX_TQ)
            self_out = _fox_attn(q, k, v, dcol, drow)
        cross_out = _mem_attn(qc, kvm)
        xf = _out_proj_ln(self_out.reshape(t, HQ), cross_out.reshape(t, HM), xf,
                          w_out[i].astype(BF16), ln_attn_g[i][None, :], ln_attn_b[i][None, :])
        g, bb = ln_ffn_g[i][None, :], ln_ffn_b[i][None, :]
        if i % 2 == 0:
            xf = _dense_ffn_ln(xf, w_ffn_in[j].astype(BF16), w_ffn_out[j].astype(BF16), g, bb)
        else:
            xf = _moe_ffn_ln(xf, w_router[j], b_router[j], w_exp_in[j].astype(BF16),
                             w_exp_out[j].astype(BF16), g, bb)
    return xf.reshape(b, s, d)
```

```python
import functools

import jax
import jax.numpy as jnp
from jax import lax
from jax.experimental import pallas as pl
from jax.experimental.pallas import tpu as pltpu

D_MODEL = 1024
DEPTH = 4
CHUNK = 64
HEAD_DIM = 64
MEM_HEADS = 4
SELF_HEADS = 12
SWA_KV_HEADS = 3
SWA_GROUP = SELF_HEADS // SWA_KV_HEADS
WINDOW_CHUNKS = 2
ROT_DIM = HEAD_DIM // 4
ROT_HALF = ROT_DIM // 2
ROPE_THETA = 500000.0
MEM_LEN = 256
D_FF = 2816
N_EXPERTS = 8
LN_EPS = 1e-5
NEG = -1e30
ATTN_SCALE = HEAD_DIM ** -0.5
DN_ALPHA = (2.0 * DEPTH) ** 0.25

HQ = SELF_HEADS * HEAD_DIM
HM = MEM_HEADS * HEAD_DIM
HKV2 = 2 * SWA_KV_HEADS * HEAD_DIM
SWA_COLS = HQ + 2 * HKV2 + HM
FOX_COLS = 3 * HQ + HM + 128

LANES = 128
VMEM_LIMIT_BYTES = 56 * 1024 * 1024

PROJ_TM = 512
FOX_PROJ_TM = 256
SWA_TQ = 256
SWA_SUB = 128
FOX_TQ = 512
FOX_TK = 256
MEM_TQ = 512
FFN_TM = 256
FF_CHUNK = D_FF // 2
ROUTER_TM = 512
MOE_TM = 256
ROW_TM = 256

F32 = jnp.float32
BF16 = jnp.bfloat16


def _cparams(*sem):
    return pltpu.CompilerParams(dimension_semantics=sem, vmem_limit_bytes=VMEM_LIMIT_BYTES)


def _layer_norm(y, g, b):
    mu = jnp.mean(y, axis=-1, keepdims=True)
    d = y - mu
    var = jnp.mean(d * d, axis=-1, keepdims=True)
    return d * lax.rsqrt(var + LN_EPS) * g + b


def _split3(a):
    hi = a.astype(BF16)
    r1 = a - hi.astype(F32)
    mid = r1.astype(BF16)
    lo = (r1 - mid.astype(F32)).astype(BF16)
    return hi, mid, lo


def _dot(a, b):
    return jnp.dot(a, b, preferred_element_type=F32)


def _dot_nt(a, b):
    return lax.dot_general(a, b, (((1,), (1,)), ((), ())), preferred_element_type=F32)


def _rope_table_kernel(pos_ref, freq_ref, cos_ref, sin_ref):
    ang = pos_ref[...].astype(F32) * freq_ref[...]
    sub = lax.broadcasted_iota(jnp.int32, ang.shape, 1) & (HEAD_DIM - 1)
    c, s = jnp.cos(ang), jnp.sin(ang)
    cos_ref[...] = jnp.where(sub < ROT_DIM, c, 1.0)
    sin_ref[...] = jnp.where(sub < ROT_HALF, -s, jnp.where(sub < ROT_DIM, s, 0.0))


def _rope_tables(positions):
    t = positions.size
    tm = min(t, 2048)
    inv_freq = ROPE_THETA ** (-jnp.arange(0, ROT_DIM, 2, dtype=F32) / ROT_DIM)
    lane = jnp.arange(LANES) % HEAD_DIM % ROT_HALF
    freq = inv_freq[lane][None, :]
    out = jax.ShapeDtypeStruct((t, LANES), F32)
    return pl.pallas_call(
        _rope_table_kernel,
        out_shape=(out, out),
        grid=(t // tm,),
        in_specs=[pl.BlockSpec((tm, 1), lambda i: (i, 0)),
                  pl.BlockSpec((1, LANES), lambda i: (0, 0))],
        out_specs=(pl.BlockSpec((tm, LANES), lambda i: (i, 0)),
                   pl.BlockSpec((tm, LANES), lambda i: (i, 0))),
        compiler_params=_cparams("parallel"),
        name="rope_tables",
    )(positions.reshape(t, 1), freq)


def _mem_kv_kernel(m_ref, w_ref, o_ref):
    o_ref[...] = _dot(m_ref[...].astype(BF16), w_ref[...]).astype(BF16)


def _mem_kv(mem2d, w):
    n = mem2d.shape[0]
    tm = min(n, PROJ_TM)
    return pl.pallas_call(
        _mem_kv_kernel,
        out_shape=jax.ShapeDtypeStruct((n, 2 * HM), BF16),
        grid=(n // tm,),
        in_specs=[pl.BlockSpec((tm, D_MODEL), lambda i: (i, 0)),
                  pl.BlockSpec((D_MODEL, 2 * HM), lambda i: (0, 0))],
        out_specs=pl.BlockSpec((tm, 2 * HM), lambda i: (i, 0)),
        compiler_params=_cparams("parallel"),
        name="mem_kv_proj",
    )(mem2d, w)


def _swa_proj_kernel(x_ref, w_ref, cos_ref, sin_ref, q_ref, kk_ref, vv_ref, qc_ref):
    acc = _dot(x_ref[...].astype(BF16), w_ref[...])
    cos, sin = cos_ref[...], sin_ref[...]
    first = (lax.broadcasted_iota(jnp.int32, cos.shape, 1) & (HEAD_DIM - 1)) < ROT_HALF

    def rope(t):
        fwd = pltpu.roll(t, LANES - ROT_HALF, axis=1)
        bwd = pltpu.roll(t, ROT_HALF, axis=1)
        return t * cos + jnp.where(first, fwd, bwd) * sin

    for g in range(HQ // LANES):
        sl = slice(g * LANES, (g + 1) * LANES)
        q_ref[:, sl] = (rope(acc[:, sl]) * ATTN_SCALE).astype(BF16)
    for g in range(HKV2 // LANES):
        src = slice(HQ + g * LANES, HQ + (g + 1) * LANES)
        kk_ref[:, g * LANES:(g + 1) * LANES] = rope(acc[:, src]).astype(BF16)
    vv_ref[...] = acc[:, HQ + HKV2:HQ + 2 * HKV2].astype(BF16)
    qc_ref[...] = (acc[:, HQ + 2 * HKV2:] * ATTN_SCALE).astype(BF16)


def _swa_proj(xf, w, cos_t, sin_t):
    t = xf.shape[0]
    tm = min(t, PROJ_TM)
    row = lambda n: pl.BlockSpec((tm, n), lambda i: (i, 0))
    return pl.pallas_call(
        _swa_proj_kernel,
        out_shape=(jax.ShapeDtypeStruct((t, HQ), BF16), jax.ShapeDtypeStruct((t, HKV2), BF16),
                   jax.ShapeDtypeStruct((t, HKV2), BF16), jax.ShapeDtypeStruct((t, HM), BF16)),
        grid=(t // tm,),
        in_specs=[row(D_MODEL), pl.BlockSpec((D_MODEL, SWA_COLS), lambda i: (0, 0)),
                  row(LANES), row(LANES)],
        out_specs=(row(HQ), row(HKV2), row(HKV2), row(HM)),
        compiler_params=_cparams("parallel"),
        name="swa_proj",
    )(xf, w, cos_t, sin_t)


def _swa_attn_kernel(sink_ref, q_ref, kp_ref, kc_ref, vp_ref, vc_ref, o_ref):
    first_tile = pl.program_id(1) == 0
    half = SWA_TQ - SWA_SUB
    lane = lax.broadcasted_iota(jnp.int32, (SWA_SUB, LANES), 1)
    shift = CHUNK.bit_length() - 1
    rc = lax.broadcasted_iota(jnp.int32, (SWA_SUB, 2 * SWA_SUB), 0) >> shift
    cc = lax.broadcasted_iota(jnp.int32, (SWA_SUB, 2 * SWA_SUB), 1) >> shift
    band = (cc >= rc) & (cc <= rc + WINDOW_CHUNKS)
    band_prev = band & ((cc >= SWA_SUB // CHUNK) | jnp.logical_not(first_tile))
    for sub in range(SWA_TQ // SWA_SUB):
        rows = slice(sub * SWA_SUB, (sub + 1) * SWA_SUB)
        mask = band_prev if sub == 0 else band
        for h in range(SWA_KV_HEADS):
            cols = slice(h * LANES, (h + 1) * LANES)
            if sub == 0:
                kcat = jnp.concatenate([kp_ref[0, half:, cols], kc_ref[0, :SWA_SUB, cols]], axis=0)
                vcat = jnp.concatenate([vp_ref[0, half:, cols], vc_ref[0, :SWA_SUB, cols]], axis=0)
            else:
                kcat = kc_ref[0, :, cols]
                vcat = vc_ref[0, :, cols]
            for pair in range(SWA_GROUP // 2):
                g = h * (SWA_GROUP // 2) + pair
                qcols = slice(g * LANES, (g + 1) * LANES)
                qf = q_ref[0, rows, qcols].astype(F32)
                outs = []
                for a in range(2):
                    keep = (lane < HEAD_DIM) if a == 0 else (lane >= HEAD_DIM)
                    qa = jnp.where(keep, qf, 0.0).astype(BF16)
                    s = jnp.where(mask, _dot_nt(qa, kcat), NEG)
                    sink = sink_ref[2 * g + a]
                    m = jnp.maximum(jnp.max(s, axis=-1, keepdims=True), sink)
                    p = jnp.exp(s - m)
                    den = jnp.sum(p, axis=-1, keepdims=True) + jnp.exp(sink - m)
                    probs = (p * (1.0 / den)).astype(BF16)
                    outs.append(_dot(probs, vcat))
                o = jnp.where(lane < HEAD_DIM, outs[0], outs[1])
                o_ref[0, rows, qcols] = o.astype(BF16)


def _swa_attn(q, kk, vv, sinks):
    b, s, _ = q.shape
    nt = s // SWA_TQ
    cur = lambda n: pl.BlockSpec((1, SWA_TQ, n), lambda bi, i: (bi, i, 0))
    prev = lambda n: pl.BlockSpec((1, SWA_TQ, n), lambda bi, i: (bi, jnp.maximum(i - 1, 0), 0))
    return pl.pallas_call(
        _swa_attn_kernel,
        out_shape=jax.ShapeDtypeStruct((b, s, HQ), BF16),
        grid=(b, nt),
        in_specs=[pl.BlockSpec(memory_space=pltpu.SMEM),
                  cur(HQ), prev(HKV2), cur(HKV2), prev(HKV2), cur(HKV2)],
        out_specs=cur(HQ),
        compiler_params=_cparams("parallel", "parallel"),
        name="swa_attn",
    )(sinks, q, kk, kk, vv, vv)


def _fox_proj_kernel(x_ref, w_ref, bf_ref, q_ref, k_ref, v_ref, qc_ref, d_ref, carry_ref):
    @pl.when(pl.program_id(1) == 0)
    def _():
        carry_ref[...] = jnp.zeros_like(carry_ref)

    acc = _dot(x_ref[0].astype(BF16), w_ref[...])
    q_ref[0] = (acc[:, :HQ] * ATTN_SCALE).astype(BF16)
    k_ref[0] = acc[:, HQ:2 * HQ].astype(BF16)
    v_ref[0] = acc[:, 2 * HQ:3 * HQ].astype(BF16)
    qc_ref[0] = (acc[:, 3 * HQ:3 * HQ + HM] * ATTN_SCALE).astype(BF16)
    z = acc[:, 3 * HQ + HM:] + bf_ref[...]
    log_f = jnp.minimum(z, 0.0) - jnp.log(1.0 + jnp.exp(-jnp.abs(z)))
    tm = z.shape[0]
    tri = (lax.broadcasted_iota(jnp.int32, (tm, tm), 0)
           >= lax.broadcasted_iota(jnp.int32, (tm, tm), 1)).astype(F32).astype(BF16)
    hi, mid, lo = _split3(log_f)
    cum = (_dot(tri, lo) + _dot(tri, mid)) + _dot(tri, hi) + carry_ref[...]
    d_ref[0] = cum
    carry_ref[...] = cum[tm - 1:tm, :]


def _fox_proj(x3, w, bf_row):
    b, s, _ = x3.shape
    tm = min(s, FOX_PROJ_TM)
    row = lambda n: pl.BlockSpec((1, tm, n), lambda bi, i: (bi, i, 0))
    sds = lambda n, dt: jax.ShapeDtypeStruct((b, s, n), dt)
    return pl.pallas_call(
        _fox_proj_kernel,
        out_shape=(sds(HQ, BF16), sds(HQ, BF16), sds(HQ, BF16), sds(HM, BF16), sds(LANES, F32)),
        grid=(b, s // tm),
        in_specs=[row(D_MODEL), pl.BlockSpec((D_MODEL, FOX_COLS), lambda bi, i: (0, 0)),
                  pl.BlockSpec((1, LANES), lambda bi, i: (0, 0))],
        out_specs=(row(HQ), row(HQ), row(HQ), row(HM), row(LANES)),
        scratch_shapes=[pltpu.VMEM((1, LANES), F32)],
        compiler_params=_cparams("arbitrary", "arbitrary"),
        name="fox_proj",
    )(x3, w, bf_row)


def _fox_attn_kernel(q_ref, k_ref, v_ref, d_ref, o_ref, kaug_ref, qaug_ref, vt_ref, qt_ref):
    s_len = q_ref.shape[1]
    tq, tk = FOX_TQ, FOX_TK
    pair = pl.program_id(1)
    lane = lax.broadcasted_iota(jnp.int32, (s_len, LANES), 1)
    prow = lax.broadcasted_iota(jnp.int32, (LANES, LANES), 0)
    pcol = lax.broadcasted_iota(jnp.int32, (LANES, LANES), 1)
    kf = k_ref[0].astype(F32)
    dsplit = _split3(d_ref[0])
    for a in range(2):
        head = 2 * pair + a
        q_extra = jnp.where((lane >= 3) & (lane < 6), 1.0, 0.0)
        k_extra = jnp.where(lane < 3, 1.0, 0.0)
        for t, term in enumerate(dsplit):
            place_q = jnp.where((prow == head) & (pcol == t), 1.0, 0.0).astype(BF16)
            place_k = jnp.where((prow == head) & (pcol == 3 + t), -1.0, 0.0).astype(BF16)
            q_extra = q_extra + _dot(term, place_q)
            k_extra = k_extra + _dot(term, place_k)
        keep = (lane < HEAD_DIM) if a == 0 else (lane >= HEAD_DIM)
        kaug_ref[a, :, :LANES] = jnp.where(keep, kf, 0.0).astype(BF16)
        kaug_ref[a, :, LANES:] = k_extra.astype(BF16)
        qaug_ref[a, :, :LANES] = q_ref[0]
        qaug_ref[a, :, LANES:] = q_extra.astype(BF16)
    for j in range(s_len // tk):
        vt_ref[j] = v_ref[0, j * tk:(j + 1) * tk, :].astype(F32).T.astype(BF16)

    nq = s_len // tq
    for qi in range(nq):
        for a in range(2):
            qt_ref[a, qi] = qaug_ref[a, qi * tq:(qi + 1) * tq, :].astype(F32).T.astype(BF16)

    def tile(kb, qi, a, state, masked):
        m, l, acc = state
        s = _dot(kaug_ref[a, kb * tk:(kb + 1) * tk, :], qt_ref[a, qi])
        if masked:
            k_pos = kb * tk + lax.broadcasted_iota(jnp.int32, (tk, tq), 0)
            q_pos = qi * tq + lax.broadcasted_iota(jnp.int32, (tk, tq), 1)
            s = jnp.where(k_pos <= q_pos, s, NEG)
        m_new = jnp.maximum(m, jnp.max(s, axis=0, keepdims=True))
        alpha = jnp.exp(m - m_new)
        p = jnp.exp(s - m_new)
        l = alpha * l + jnp.sum(p, axis=0, keepdims=True)
        acc = alpha * acc + _dot(vt_ref[kb], p.astype(BF16))
        return m_new, l, acc

    state = {(qi, a): (jnp.full((1, tq), NEG, F32), jnp.zeros((1, tq), F32),
                       jnp.zeros((LANES, tq), F32)) for qi in range(nq) for a in range(2)}
    for kb in range(s_len // tk):
        for qi in range(kb * tk // tq, nq):
            masked = (kb + 1) * tk - 1 > qi * tq
            for a in range(2):
                state[qi, a] = tile(kb, qi, a, state[qi, a], masked)
    orow = lax.broadcasted_iota(jnp.int32, (LANES, tq), 0)
    for qi in range(nq):
        (_, l0, acc0), (_, l1, acc1) = state[qi, 0], state[qi, 1]
        o_t = jnp.where(orow < HEAD_DIM, acc0 * (1.0 / l0), acc1 * (1.0 / l1))
        o_ref[0, qi * tq:(qi + 1) * tq, :] = o_t.T.astype(BF16)


def _fox_attn(q, k, v, dcum):
    b, s, _ = q.shape
    seq = lambda: pl.BlockSpec((1, s, LANES), lambda bi, j: (bi, 0, j))
    return pl.pallas_call(
        _fox_attn_kernel,
        out_shape=jax.ShapeDtypeStruct((b, s, HQ), BF16),
        grid=(b, HQ // LANES),
        in_specs=[seq(), seq(), seq(), pl.BlockSpec((1, s, LANES), lambda bi, j: (bi, 0, 0))],
        out_specs=seq(),
        scratch_shapes=[pltpu.VMEM((2, s, 2 * LANES), BF16), pltpu.VMEM((2, s, 2 * LANES), BF16),
                        pltpu.VMEM((s // FOX_TK, LANES, FOX_TK), BF16),
                        pltpu.VMEM((2, s // FOX_TQ, 2 * LANES, FOX_TQ), BF16)],
        compiler_params=_cparams("parallel", "parallel"),
        name="fox_attn",
    )(q, k, v, dcum)


def _mem_attn_kernel(q_ref, kv_ref, o_ref):
    lane = lax.broadcasted_iota(jnp.int32, (MEM_TQ, LANES), 1)
    for pair in range(HM // LANES):
        cols = slice(pair * LANES, (pair + 1) * LANES)
        kblk = kv_ref[0, :, cols]
        vblk = kv_ref[0, :, HM + pair * LANES:HM + (pair + 1) * LANES]
        qf = q_ref[0, :, cols].astype(F32)
        outs = []
        for a in range(2):
            keep = (lane < HEAD_DIM) if a == 0 else (lane >= HEAD_DIM)
            s = _dot_nt(jnp.where(keep, qf, 0.0).astype(BF16), kblk)
            p = jnp.exp(s - jnp.max(s, axis=-1, keepdims=True))
            l = jnp.sum(p, axis=-1, keepdims=True)
            outs.append(_dot(p.astype(BF16), vblk) * (1.0 / l))
        o_ref[0, :, cols] = jnp.where(lane < HEAD_DIM, outs[0], outs[1]).astype(BF16)


def _mem_attn(qc, kvm):
    b, s, _ = qc.shape
    return pl.pallas_call(
        _mem_attn_kernel,
        out_shape=jax.ShapeDtypeStruct((b, s, HM), BF16),
        grid=(b, s // MEM_TQ),
        in_specs=[pl.BlockSpec((1, MEM_TQ, HM), lambda bi, i: (bi, i, 0)),
                  pl.BlockSpec((1, MEM_LEN, 2 * HM), lambda bi, i: (bi, 0, 0))],
        out_specs=pl.BlockSpec((1, MEM_TQ, HM), lambda bi, i: (bi, i, 0)),
        compiler_params=_cparams("parallel", "parallel"),
        name="mem_attn",
    )(qc, kvm)


def _out_proj_kernel(s_ref, c_ref, x_ref, w_ref, g_ref, b_ref, o_ref):
    mix = _dot(s_ref[...], w_ref[:HQ, :]) + _dot(c_ref[...], w_ref[HQ:, :])
    o_ref[...] = _layer_norm(DN_ALPHA * x_ref[...] + mix, g_ref[...], b_ref[...])


def _out_proj_ln(self_out, cross_out, xf, w, g, b):
    t = xf.shape[0]
    tm = min(t, PROJ_TM)
    row = lambda n: pl.BlockSpec((tm, n), lambda i: (i, 0))
    const = lambda r, c: pl.BlockSpec((r, c), lambda i: (0, 0))
    return pl.pallas_call(
        _out_proj_kernel,
        out_shape=jax.ShapeDtypeStruct((t, D_MODEL), F32),
        grid=(t // tm,),
        in_specs=[row(HQ), row(HM), row(D_MODEL), const(D_MODEL, D_MODEL),
                  const(1, D_MODEL), const(1, D_MODEL)],
        out_specs=row(D_MODEL),
        compiler_params=_cparams("parallel"),
        name="out_proj_ln",
    )(self_out, cross_out, xf, w, g, b)


def _swiglu(xb, w_in_ref, w_out_ref):
    acc = None
    for c in range(D_FF // FF_CHUNK):
        lo = c * FF_CHUNK
        ha = _dot(xb, w_in_ref[:, lo:lo + FF_CHUNK])
        hb = _dot(xb, w_in_ref[:, D_FF + lo:D_FF + lo + FF_CHUNK])
        act = (ha * (1.0 / (1.0 + jnp.exp(-ha))) * hb).astype(BF16)
        part = _dot(act, w_out_ref[lo:lo + FF_CHUNK, :])
        acc = part if acc is None else acc + part
    return acc


def _dense_ffn_kernel(x_ref, w_in_ref, w_out_ref, g_ref, b_ref, o_ref):
    x = x_ref[...]
    ffn = _swiglu(x.astype(BF16), w_in_ref, w_out_ref)
    o_ref[...] = _layer_norm(DN_ALPHA * x + ffn, g_ref[...], b_ref[...])


def _dense_ffn_ln(xf, w_in, w_out, g, b):
    t = xf.shape[0]
    tm = min(t, FFN_TM)
    row = pl.BlockSpec((tm, D_MODEL), lambda i: (i, 0))
    const = lambda r, c: pl.BlockSpec((r, c), lambda i: (0, 0))
    return pl.pallas_call(
        _dense_ffn_kernel,
        out_shape=jax.ShapeDtypeStruct((t, D_MODEL), F32),
        grid=(t // tm,),
        in_specs=[row, const(D_MODEL, 2 * D_FF), const(D_FF, D_MODEL),
                  const(1, D_MODEL), const(1, D_MODEL)],
        out_specs=row,
        compiler_params=_cparams("parallel"),
        name="dense_ffn_ln",
    )(xf, w_in, w_out, g, b)


def _expert_kernel(be_ref, bx_ref, xs_ref, w_in_ref, w_out_ref, ys_ref):
    del be_ref
    i = pl.program_id(0)
    used = bx_ref[i] == i

    @pl.when(used)
    def _():
        ys_ref[...] = _swiglu(xs_ref[...].astype(BF16), w_in_ref.at[0], w_out_ref.at[0])

    @pl.when(jnp.logical_not(used))
    def _():
        ys_ref[...] = jnp.zeros_like(ys_ref)


def _expert_ffn(xs, w_in, w_out, block_expert, block_src):
    n_slots = xs.shape[0]
    return pl.pallas_call(
        _expert_kernel,
        out_shape=jax.ShapeDtypeStruct((n_slots, D_MODEL), F32),
        grid_spec=pltpu.PrefetchScalarGridSpec(
            num_scalar_prefetch=2,
            grid=(n_slots // MOE_TM,),
            in_specs=[pl.BlockSpec((MOE_TM, D_MODEL), lambda i, be, bx: (bx[i], 0)),
                      pl.BlockSpec((1, D_MODEL, 2 * D_FF), lambda i, be, bx: (be[i], 0, 0)),
                      pl.BlockSpec((1, D_FF, D_MODEL), lambda i, be, bx: (be[i], 0, 0))],
            out_specs=pl.BlockSpec((MOE_TM, D_MODEL), lambda i, be, bx: (i, 0))),
        compiler_params=_cparams("arbitrary"),
        name="expert_ffn",
    )(block_expert, block_src, xs, w_in, w_out)


ROUTE_E1, ROUTE_E2, ROUTE_R1, ROUTE_R2, ROUTE_G1, ROUTE_G2 = range(6)


def _router_kernel(x_ref, w_ref, b_ref, route_ref, count_ref, carry_ref):
    @pl.when(pl.program_id(0) == 0)
    def _():
        carry_ref[...] = jnp.zeros_like(carry_ref)

    xs, ws = _split3(x_ref[...]), _split3(w_ref[...])
    logits = None
    for i, j in ((2, 0), (1, 1), (0, 2), (1, 0), (0, 1), (0, 0)):
        term = _dot(xs[i], ws[j])
        logits = term if logits is None else logits + term
    tm = logits.shape[0]
    lane = lax.broadcasted_iota(jnp.int32, (tm, LANES), 1).astype(F32)
    logits = jnp.where(lane < N_EXPERTS, logits + b_ref[...], -jnp.inf)
    m1 = jnp.max(logits, axis=-1, keepdims=True)
    e1 = jnp.min(jnp.where(logits == m1, lane, float(LANES)), axis=-1, keepdims=True)
    rest = jnp.where(lane == e1, -jnp.inf, logits)
    m2 = jnp.max(rest, axis=-1, keepdims=True)
    e2 = jnp.min(jnp.where(rest == m2, lane, float(LANES)), axis=-1, keepdims=True)
    ex = jnp.exp(m2 - m1)
    g1 = 1.0 / (1.0 + ex)
    g2 = ex * g1
    sel1, sel2 = lane == e1, lane == e2
    onehot = jnp.where(sel1 | sel2, 1.0, 0.0)
    tri = (lax.broadcasted_iota(jnp.int32, (tm, tm), 0)
           > lax.broadcasted_iota(jnp.int32, (tm, tm), 1)).astype(F32).astype(BF16)
    rank = _dot(tri, onehot.astype(BF16)) + carry_ref[...]
    r1 = jnp.sum(jnp.where(sel1, rank, 0.0), axis=-1, keepdims=True)
    r2 = jnp.sum(jnp.where(sel2, rank, 0.0), axis=-1, keepdims=True)
    total = carry_ref[...] + jnp.sum(onehot, axis=0, keepdims=True)
    carry_ref[...] = total
    count_ref[...] = total
    route = jnp.zeros((tm, LANES), F32)
    for ln, val in ((ROUTE_E1, e1), (ROUTE_E2, e2), (ROUTE_R1, r1), (ROUTE_R2, r2),
                    (ROUTE_G1, g1), (ROUTE_G2, g2)):
        route = jnp.where(lane == float(ln), val, route)
    route_ref[...] = route


def _router(xf, w_pad, b_pad):
    t = xf.shape[0]
    tm = min(t, ROUTER_TM)
    return pl.pallas_call(
        _router_kernel,
        out_shape=(jax.ShapeDtypeStruct((t, LANES), F32), jax.ShapeDtypeStruct((1, LANES), F32)),
        grid=(t // tm,),
        in_specs=[pl.BlockSpec((tm, D_MODEL), lambda i: (i, 0)),
                  pl.BlockSpec((D_MODEL, LANES), lambda i: (0, 0)),
                  pl.BlockSpec((1, LANES), lambda i: (0, 0))],
        out_specs=(pl.BlockSpec((tm, LANES), lambda i: (i, 0)),
                   pl.BlockSpec((1, LANES), lambda i: (0, 0))),
        scratch_shapes=[pltpu.VMEM((1, LANES), F32)],
        compiler_params=_cparams("arbitrary"),
        name="router",
    )(xf, w_pad, b_pad)


ROW_UNROLL = 8


def _row_copies(n, make_copy):
    def start(r, c):
        make_copy(r, 0).start()
        make_copy(r, 1).start()
        return c

    def wait(r, c):
        make_copy(r, 0).wait()
        make_copy(r, 1).wait()
        return c

    lax.fori_loop(0, n, start, 0, unroll=ROW_UNROLL)
    lax.fori_loop(0, n, wait, 0, unroll=ROW_UNROLL)


def _dispatch_kernel(dest_ref, pend_ref, x_ref, xs_hbm, zero_ref, sem, zsem):
    base = pl.program_id(0) * ROW_TM

    @pl.when(pl.program_id(0) == 0)
    def _():
        zero_ref[...] = jnp.zeros_like(zero_ref)

        def zero_copy(e):
            start = pl.multiple_of(pend_ref[e + 1] - MOE_TM, MOE_TM)
            return pltpu.make_async_copy(zero_ref, xs_hbm.at[pl.ds(start, MOE_TM)], zsem)

        def tail_copy(blk):
            return pltpu.make_async_copy(
                zero_ref, xs_hbm.at[pl.ds(pl.multiple_of(blk * MOE_TM, MOE_TM), MOE_TM)], zsem)

        first_tail = pend_ref[N_EXPERTS] // MOE_TM
        n_blocks = xs_hbm.shape[0] // MOE_TM
        for e in range(N_EXPERTS):
            @pl.when(pend_ref[e + 1] > pend_ref[e])
            def _():
                zero_copy(e).start()
        lax.fori_loop(first_tail, n_blocks, lambda blk, c: (tail_copy(blk).start(), c)[1], 0)
        for e in range(N_EXPERTS):
            @pl.when(pend_ref[e + 1] > pend_ref[e])
            def _():
                zero_copy(e).wait()
        lax.fori_loop(first_tail, n_blocks, lambda blk, c: (tail_copy(blk).wait(), c)[1], 0)

    def make_copy(r, k):
        return pltpu.make_async_copy(x_ref.at[pl.ds(r, 1)],
                                     xs_hbm.at[pl.ds(dest_ref[2 * (base + r) + k], 1)], sem)

    _row_copies(ROW_TM, make_copy)


def _dispatch(xf, dest, seg_ends, n_slots):
    t = xf.shape[0]
    return pl.pallas_call(
        _dispatch_kernel,
        out_shape=jax.ShapeDtypeStruct((n_slots, D_MODEL), F32),
        grid_spec=pltpu.PrefetchScalarGridSpec(
            num_scalar_prefetch=2,
            grid=(t // ROW_TM,),
            in_specs=[pl.BlockSpec((ROW_TM, D_MODEL), lambda i, d, p: (i, 0))],
            out_specs=pl.BlockSpec(memory_space=pl.ANY),
            scratch_shapes=[pltpu.VMEM((MOE_TM, D_MODEL), F32), pltpu.SemaphoreType.DMA(()),
                            pltpu.SemaphoreType.DMA(())]),
        compiler_params=_cparams("arbitrary"),
        name="moe_dispatch",
    )(dest, seg_ends, xf)


def _combine_kernel(dest_ref, x_ref, route_ref, g_ref, b_ref, ys_hbm, o_ref, ya, yb, sem):
    base = pl.program_id(0) * ROW_TM

    def make_copy(r, k):
        dst = ya if k == 0 else yb
        return pltpu.make_async_copy(ys_hbm.at[pl.ds(dest_ref[2 * (base + r) + k], 1)],
                                     dst.at[pl.ds(r, 1)], sem)

    _row_copies(ROW_TM, make_copy)
    route = route_ref[...]
    g1 = route[:, ROUTE_G1:ROUTE_G1 + 1]
    g2 = route[:, ROUTE_G2:ROUTE_G2 + 1]
    y = ya[...] * g1 + yb[...] * g2
    o_ref[...] = _layer_norm(DN_ALPHA * x_ref[...] + y, g_ref[...], b_ref[...])


def _combine_ln(xf, route, ys, dest, g, b):
    t = xf.shape[0]
    row = lambda n: pl.BlockSpec((ROW_TM, n), lambda i, d: (i, 0))
    const = pl.BlockSpec((1, D_MODEL), lambda i, d: (0, 0))
    return pl.pallas_call(
        _combine_kernel,
        out_shape=jax.ShapeDtypeStruct((t, D_MODEL), F32),
        grid_spec=pltpu.PrefetchScalarGridSpec(
            num_scalar_prefetch=1,
            grid=(t // ROW_TM,),
            in_specs=[row(D_MODEL), row(LANES), const, const, pl.BlockSpec(memory_space=pl.ANY)],
            out_specs=row(D_MODEL),
            scratch_shapes=[pltpu.VMEM((ROW_TM, D_MODEL), F32), pltpu.VMEM((ROW_TM, D_MODEL), F32),
                            pltpu.SemaphoreType.DMA(())]),
        compiler_params=_cparams("arbitrary"),
        name="moe_combine_ln",
    )(dest, xf, route, g, b, ys)


def _moe_ffn_ln(xf, w_router, b_router, w_in, w_out, g, b):
    t = xf.shape[0]
    w_pad = jnp.zeros((D_MODEL, LANES), F32).at[:, :N_EXPERTS].set(w_router)
    b_pad = jnp.zeros((1, LANES), F32).at[0, :N_EXPERTS].set(b_router)
    route, counts = _router(xf, w_pad, b_pad)
    counts = counts[0, :N_EXPERTS].astype(jnp.int32)
    pcounts = (counts + MOE_TM - 1) // MOE_TM * MOE_TM
    pends = jnp.cumsum(pcounts)
    pstarts = pends - pcounts
    e = route[:, ROUTE_E1:ROUTE_E2 + 1].astype(jnp.int32)
    r = route[:, ROUTE_R1:ROUTE_R2 + 1].astype(jnp.int32)
    dest = (pstarts[e] + r).reshape(-1)
    n_blocks = 2 * t // MOE_TM + N_EXPERTS
    blk_start = jnp.arange(n_blocks, dtype=jnp.int32) * MOE_TM
    block_expert = jnp.minimum(jnp.sum(blk_start[:, None] >= pends[None, :], axis=1),
                               N_EXPERTS - 1).astype(jnp.int32)
    n_used = pends[-1] // MOE_TM
    block_src = jnp.minimum(jnp.arange(n_blocks, dtype=jnp.int32), n_used - 1)
    seg_ends = jnp.concatenate([jnp.zeros((1,), jnp.int32), pends.astype(jnp.int32)])
    xs = _dispatch(xf, dest, seg_ends, n_blocks * MOE_TM)
    ys = _expert_ffn(xs, w_in, w_out, block_expert, block_src)
    return _combine_ln(xf, route, ys, dest, g, b)


def _dup_heads(w):
    d, n = w.shape
    w = w.reshape(d, n // HEAD_DIM, 1, HEAD_DIM)
    return jnp.broadcast_to(w, (d, n // HEAD_DIM, 2, HEAD_DIM)).reshape(d, 2 * n)


def kernel(x, mem, positions, w_in_swa, attn_sinks, w_in_fox, b_forget, w_mem_kv, w_out,
           ln_attn_g, ln_attn_b, w_ffn_in, w_ffn_out, w_router, b_router, w_exp_in,
           w_exp_out, ln_ffn_g, ln_ffn_b):
    b, s, d = x.shape
    t = b * s
    hkv = SWA_KV_HEADS * HEAD_DIM
    cos_t, sin_t = _rope_tables(positions)
    mem2d = mem.reshape(b * MEM_LEN, d)
    xf = x.reshape(t, d)
    for i in range(DEPTH):
        j = i // 2
        kvm = _mem_kv(mem2d, w_mem_kv[i].astype(BF16)).reshape(b, MEM_LEN, 2 * HM)
        if i % 2 == 0:
            w = w_in_swa[j]
            w = jnp.concatenate([w[:, :HQ], _dup_heads(w[:, HQ:HQ + hkv]),
                                 _dup_heads(w[:, HQ + hkv:HQ + 2 * hkv]), w[:, HQ + 2 * hkv:]],
                                axis=1).astype(BF16)
            q, kk, vv, qc = _swa_proj(xf, w, cos_t, sin_t)
            self_out = _swa_attn(q.reshape(b, s, HQ), kk.reshape(b, s, HKV2),
                                 vv.reshape(b, s, HKV2), attn_sinks[j])
            qc = qc.reshape(b, s, HM)
        else:
            w = jnp.pad(w_in_fox[j], ((0, 0), (0, FOX_COLS - w_in_fox.shape[-1]))).astype(BF16)
            bf_row = jnp.pad(b_forget[j], (0, LANES - SELF_HEADS))[None, :]
            q, k, v, qc, dcum = _fox_proj(xf.reshape(b, s, d), w, bf_row)
            self_out = _fox_attn(q, k, v, dcum)
        cross_out = _mem_attn(qc, kvm)
        xf = _out_proj_ln(self_out.reshape(t, HQ), cross_out.reshape(t, HM), xf,
                          w_out[i].astype(BF16), ln_attn_g[i][None, :], ln_attn_b[i][None, :])
        g, bb = ln_ffn_g[i][None, :], ln_ffn_b[i][None, :]
        if i % 2 == 0:
            xf = _dense_ffn_ln(xf, w_ffn_in[j].astype(BF16), w_ffn_out[j].astype(BF16), g, bb)
        else:
            xf = _moe_ffn_ln(xf, w_router[j], b_router[j], w_exp_in[j].astype(BF16),
                             w_exp_out[j].astype(BF16), g, bb)
    return xf.reshape(b, s, d)
```

```python
import functools

import jax
import jax.numpy as jnp
from jax import lax
from jax.experimental import pallas as pl
from jax.experimental.pallas import tpu as pltpu

D_MODEL = 1024
DEPTH = 4
CHUNK = 64
HEAD_DIM = 64
MEM_HEADS = 4
SELF_HEADS = 12
SWA_KV_HEADS = 3
SWA_GROUP = SELF_HEADS // SWA_KV_HEADS
WINDOW_CHUNKS = 2
ROT_DIM = HEAD_DIM // 4
ROT_HALF = ROT_DIM // 2
ROPE_THETA = 500000.0
MEM_LEN = 256
D_FF = 2816
N_EXPERTS = 8
LN_EPS = 1e-5
NEG = -1e30
ATTN_SCALE = HEAD_DIM ** -0.5
DN_ALPHA = (2.0 * DEPTH) ** 0.25

HQ = SELF_HEADS * HEAD_DIM
HM = MEM_HEADS * HEAD_DIM
HKV2 = 2 * SWA_KV_HEADS * HEAD_DIM
SWA_COLS = HQ + 2 * HKV2 + HM
FOX_COLS = 3 * HQ + HM + 128

LANES = 128
VMEM_LIMIT_BYTES = 56 * 1024 * 1024

PROJ_TM = 512
FOX_PROJ_TM = 256
SWA_TQ = 256
SWA_SUB = 128
FOX_TQ = 1024
FOX_TK = 1024
MEM_TQ = 512
FFN_TM = 256
MXU_DIM = 256
FF_CHUNKS = ((0, 6 * MXU_DIM), (6 * MXU_DIM, D_FF))
ROUTER_TM = 512
MOE_TM = 256
ROW_TM = 256

F32 = jnp.float32
BF16 = jnp.bfloat16


def _cparams(*sem):
    return pltpu.CompilerParams(dimension_semantics=sem, vmem_limit_bytes=VMEM_LIMIT_BYTES)


def _layer_norm(y, g, b):
    mu = jnp.mean(y, axis=-1, keepdims=True)
    d = y - mu
    var = jnp.mean(d * d, axis=-1, keepdims=True)
    return d * lax.rsqrt(var + LN_EPS) * g + b


def _split3(a):
    hi = a.astype(BF16)
    r1 = a - hi.astype(F32)
    mid = r1.astype(BF16)
    lo = (r1 - mid.astype(F32)).astype(BF16)
    return hi, mid, lo


def _dot(a, b):
    return jnp.dot(a, b, preferred_element_type=F32)


def _dot_nt(a, b):
    return lax.dot_general(a, b, (((1,), (1,)), ((), ())), preferred_element_type=F32)


def _rope_table_kernel(pos_ref, freq_ref, cos_ref, sin_ref):
    ang = pos_ref[...].astype(F32) * freq_ref[...]
    sub = lax.broadcasted_iota(jnp.int32, ang.shape, 1) & (HEAD_DIM - 1)
    c, s = jnp.cos(ang), jnp.sin(ang)
    cos_ref[...] = jnp.where(sub < ROT_DIM, c, 1.0)
    sin_ref[...] = jnp.where(sub < ROT_HALF, -s, jnp.where(sub < ROT_DIM, s, 0.0))


def _rope_tables(positions):
    t = positions.size
    tm = min(t, 2048)
    inv_freq = ROPE_THETA ** (-jnp.arange(0, ROT_DIM, 2, dtype=F32) / ROT_DIM)
    lane = jnp.arange(LANES) % HEAD_DIM % ROT_HALF
    freq = inv_freq[lane][None, :]
    out = jax.ShapeDtypeStruct((t, LANES), F32)
    return pl.pallas_call(
        _rope_table_kernel,
        out_shape=(out, out),
        grid=(t // tm,),
        in_specs=[pl.BlockSpec((tm, 1), lambda i: (i, 0)),
                  pl.BlockSpec((1, LANES), lambda i: (0, 0))],
        out_specs=(pl.BlockSpec((tm, LANES), lambda i: (i, 0)),
                   pl.BlockSpec((tm, LANES), lambda i: (i, 0))),
        compiler_params=_cparams("parallel"),
        name="rope_tables",
    )(positions.reshape(t, 1), freq)


def _mem_kv_kernel(m_ref, w_ref, o_ref):
    o_ref[...] = _dot(m_ref[...].astype(BF16), w_ref[...]).astype(BF16)


def _mem_kv(mem2d, w):
    n = mem2d.shape[0]
    tm = min(n, PROJ_TM)
    return pl.pallas_call(
        _mem_kv_kernel,
        out_shape=jax.ShapeDtypeStruct((n, 2 * HM), BF16),
        grid=(n // tm,),
        in_specs=[pl.BlockSpec((tm, D_MODEL), lambda i: (i, 0)),
                  pl.BlockSpec((D_MODEL, 2 * HM), lambda i: (0, 0))],
        out_specs=pl.BlockSpec((tm, 2 * HM), lambda i: (i, 0)),
        compiler_params=_cparams("parallel"),
        name="mem_kv_proj",
    )(mem2d, w)


def _swa_proj_kernel(x_ref, w_ref, cos_ref, sin_ref, q_ref, kk_ref, vv_ref, qc_ref):
    acc = _dot(x_ref[...].astype(BF16), w_ref[...])
    cos, sin = cos_ref[...], sin_ref[...]
    first = (lax.broadcasted_iota(jnp.int32, cos.shape, 1) & (HEAD_DIM - 1)) < ROT_HALF

    def rope(t):
        fwd = pltpu.roll(t, LANES - ROT_HALF, axis=1)
        bwd = pltpu.roll(t, ROT_HALF, axis=1)
        return t * cos + jnp.where(first, fwd, bwd) * sin

    for g in range(HQ // LANES):
        sl = slice(g * LANES, (g + 1) * LANES)
        q_ref[:, sl] = (rope(acc[:, sl]) * ATTN_SCALE).astype(BF16)
    for g in range(HKV2 // LANES):
        src = slice(HQ + g * LANES, HQ + (g + 1) * LANES)
        kk_ref[:, g * LANES:(g + 1) * LANES] = rope(acc[:, src]).astype(BF16)
    vv_ref[...] = acc[:, HQ + HKV2:HQ + 2 * HKV2].astype(BF16)
    qc_ref[...] = (acc[:, HQ + 2 * HKV2:] * ATTN_SCALE).astype(BF16)


def _swa_proj(xf, w, cos_t, sin_t):
    t = xf.shape[0]
    tm = min(t, PROJ_TM)
    row = lambda n: pl.BlockSpec((tm, n), lambda i: (i, 0))
    return pl.pallas_call(
        _swa_proj_kernel,
        out_shape=(jax.ShapeDtypeStruct((t, HQ), BF16), jax.ShapeDtypeStruct((t, HKV2), BF16),
                   jax.ShapeDtypeStruct((t, HKV2), BF16), jax.ShapeDtypeStruct((t, HM), BF16)),
        grid=(t // tm,),
        in_specs=[row(D_MODEL), pl.BlockSpec((D_MODEL, SWA_COLS), lambda i: (0, 0)),
                  row(LANES), row(LANES)],
        out_specs=(row(HQ), row(HKV2), row(HKV2), row(HM)),
        compiler_params=_cparams("parallel"),
        name="swa_proj",
    )(xf, w, cos_t, sin_t)


def _swa_attn_kernel(sink_ref, q_ref, kp_ref, kc_ref, vp_ref, vc_ref, o_ref):
    first_tile = pl.program_id(1) == 0
    half = SWA_TQ - SWA_SUB
    lane = lax.broadcasted_iota(jnp.int32, (SWA_SUB, LANES), 1)
    shift = CHUNK.bit_length() - 1
    rc = lax.broadcasted_iota(jnp.int32, (SWA_SUB, 2 * SWA_SUB), 0) >> shift
    cc = lax.broadcasted_iota(jnp.int32, (SWA_SUB, 2 * SWA_SUB), 1) >> shift
    band = (cc >= rc) & (cc <= rc + WINDOW_CHUNKS)
    band_prev = band & ((cc >= SWA_SUB // CHUNK) | jnp.logical_not(first_tile))
    for sub in range(SWA_TQ // SWA_SUB):
        rows = slice(sub * SWA_SUB, (sub + 1) * SWA_SUB)
        mask = band_prev if sub == 0 else band
        for h in range(SWA_KV_HEADS):
            cols = slice(h * LANES, (h + 1) * LANES)
            if sub == 0:
                kcat = jnp.concatenate([kp_ref[0, half:, cols], kc_ref[0, :SWA_SUB, cols]], axis=0)
                vcat = jnp.concatenate([vp_ref[0, half:, cols], vc_ref[0, :SWA_SUB, cols]], axis=0)
            else:
                kcat = kc_ref[0, :, cols]
                vcat = vc_ref[0, :, cols]
            for pair in range(SWA_GROUP // 2):
                g = h * (SWA_GROUP // 2) + pair
                qcols = slice(g * LANES, (g + 1) * LANES)
                qf = q_ref[0, rows, qcols].astype(F32)
                outs = []
                for a in range(2):
                    keep = (lane < HEAD_DIM) if a == 0 else (lane >= HEAD_DIM)
                    qa = jnp.where(keep, qf, 0.0).astype(BF16)
                    s = jnp.where(mask, _dot_nt(qa, kcat), NEG)
                    sink = sink_ref[2 * g + a]
                    m = jnp.maximum(jnp.max(s, axis=-1, keepdims=True), sink)
                    p = jnp.exp(s - m)
                    den = jnp.sum(p, axis=-1, keepdims=True) + jnp.exp(sink - m)
                    probs = (p * (1.0 / den)).astype(BF16)
                    outs.append(_dot(probs, vcat))
                o = jnp.where(lane < HEAD_DIM, outs[0], outs[1])
                o_ref[0, rows, qcols] = o.astype(BF16)


def _swa_attn(q, kk, vv, sinks):
    b, s, _ = q.shape
    nt = s // SWA_TQ
    cur = lambda n: pl.BlockSpec((1, SWA_TQ, n), lambda bi, i: (bi, i, 0))
    prev = lambda n: pl.BlockSpec((1, SWA_TQ, n), lambda bi, i: (bi, jnp.maximum(i - 1, 0), 0))
    return pl.pallas_call(
        _swa_attn_kernel,
        out_shape=jax.ShapeDtypeStruct((b, s, HQ), BF16),
        grid=(b, nt),
        in_specs=[pl.BlockSpec(memory_space=pltpu.SMEM),
                  cur(HQ), prev(HKV2), cur(HKV2), prev(HKV2), cur(HKV2)],
        out_specs=cur(HQ),
        compiler_params=_cparams("parallel", "parallel"),
        name="swa_attn",
    )(sinks, q, kk, kk, vv, vv)


def _fox_proj_kernel(x_ref, w_ref, bf_ref, q_ref, k_ref, v_ref, qc_ref, d_ref, carry_ref):
    @pl.when(pl.program_id(1) == 0)
    def _():
        carry_ref[...] = jnp.zeros_like(carry_ref)

    acc = _dot(x_ref[0].astype(BF16), w_ref[...])
    q_ref[0] = (acc[:, :HQ] * ATTN_SCALE).astype(BF16)
    k_ref[0] = acc[:, HQ:2 * HQ].astype(BF16)
    v_ref[0] = acc[:, 2 * HQ:3 * HQ].astype(BF16)
    qc_ref[0] = (acc[:, 3 * HQ:3 * HQ + HM] * ATTN_SCALE).astype(BF16)
    z = acc[:, 3 * HQ + HM:] + bf_ref[...]
    log_f = jnp.minimum(z, 0.0) - jnp.log(1.0 + jnp.exp(-jnp.abs(z)))
    tm = z.shape[0]
    tri = (lax.broadcasted_iota(jnp.int32, (tm, tm), 0)
           >= lax.broadcasted_iota(jnp.int32, (tm, tm), 1)).astype(F32).astype(BF16)
    hi, mid, lo = _split3(log_f)
    cum = (_dot(tri, lo) + _dot(tri, mid)) + _dot(tri, hi) + carry_ref[...]
    d_ref[0] = cum
    carry_ref[...] = cum[tm - 1:tm, :]


def _fox_proj(x3, w, bf_row):
    b, s, _ = x3.shape
    tm = min(s, FOX_PROJ_TM)
    row = lambda n: pl.BlockSpec((1, tm, n), lambda bi, i: (bi, i, 0))
    sds = lambda n, dt: jax.ShapeDtypeStruct((b, s, n), dt)
    return pl.pallas_call(
        _fox_proj_kernel,
        out_shape=(sds(HQ, BF16), sds(HQ, BF16), sds(HQ, BF16), sds(HM, BF16), sds(LANES, F32)),
        grid=(b, s // tm),
        in_specs=[row(D_MODEL), pl.BlockSpec((D_MODEL, FOX_COLS), lambda bi, i: (0, 0)),
                  pl.BlockSpec((1, LANES), lambda bi, i: (0, 0))],
        out_specs=(row(HQ), row(HQ), row(HQ), row(HM), row(LANES)),
        scratch_shapes=[pltpu.VMEM((1, LANES), F32)],
        compiler_params=_cparams("arbitrary", "arbitrary"),
        name="fox_proj",
    )(x3, w, bf_row)


def _fox_attn_kernel(q_ref, k_ref, v_ref, d_ref, o_ref, kaug_ref, qaug_ref, vt_ref, qt_ref):
    s_len = q_ref.shape[1]
    tq, tk = qt_ref.shape[3], vt_ref.shape[2]
    pair = pl.program_id(1)
    lane = lax.broadcasted_iota(jnp.int32, (s_len, LANES), 1)
    prow = lax.broadcasted_iota(jnp.int32, (3 * LANES, 2 * LANES), 0)
    pcol = lax.broadcasted_iota(jnp.int32, (3 * LANES, 2 * LANES), 1)
    term = prow >> 7
    kf = k_ref[0].astype(F32)
    dsplit = jnp.concatenate(_split3(d_ref[0]), axis=1)
    for a in range(2):
        head = 2 * pair + a
        from_head = (prow & (LANES - 1)) == head
        place = (jnp.where(from_head & (pcol == term), 1.0, 0.0)
                 + jnp.where(from_head & (pcol == LANES + 3 + term), -1.0, 0.0)).astype(BF16)
        extra = _dot(dsplit, place)
        q_extra = extra[:, :LANES] + jnp.where((lane >= 3) & (lane < 6), 1.0, 0.0)
        k_extra = extra[:, LANES:] + jnp.where(lane < 3, 1.0, 0.0)
        keep = (lane < HEAD_DIM) if a == 0 else (lane >= HEAD_DIM)
        kaug_ref[a, :, :LANES] = jnp.where(keep, kf, 0.0).astype(BF16)
        kaug_ref[a, :, LANES:] = k_extra.astype(BF16)
        qaug_ref[a, :, :LANES] = q_ref[0]
        qaug_ref[a, :, LANES:] = q_extra.astype(BF16)
    for j in range(s_len // tk):
        vt_ref[j] = v_ref[0, j * tk:(j + 1) * tk, :].astype(F32).T.astype(BF16)

    nq = s_len // tq
    for qi in range(nq):
        for a in range(2):
            qt_ref[a, qi] = qaug_ref[a, qi * tq:(qi + 1) * tq, :].astype(F32).T.astype(BF16)

    def tile(kb, qi, a, state, masked):
        m, l, acc = state
        s = _dot(kaug_ref[a, kb * tk:(kb + 1) * tk, :], qt_ref[a, qi])
        if masked:
            k_pos = kb * tk + lax.broadcasted_iota(jnp.int32, (tk, tq), 0)
            q_pos = qi * tq + lax.broadcasted_iota(jnp.int32, (tk, tq), 1)
            s = jnp.where(k_pos <= q_pos, s, NEG)
        m_new = jnp.maximum(m, jnp.max(s, axis=0, keepdims=True))
        alpha = jnp.exp(m - m_new)
        p = jnp.exp(s - m_new)
        l = alpha * l + jnp.sum(p, axis=0, keepdims=True)
        acc = alpha * acc + _dot(vt_ref[kb], p.astype(BF16))
        return m_new, l, acc

    state = {(qi, a): (jnp.full((1, tq), NEG, F32), jnp.zeros((1, tq), F32),
                       jnp.zeros((LANES, tq), F32)) for qi in range(nq) for a in range(2)}
    for kb in range(s_len // tk):
        for qi in range(kb * tk // tq, nq):
            masked = (kb + 1) * tk - 1 > qi * tq
            for a in range(2):
                state[qi, a] = tile(kb, qi, a, state[qi, a], masked)
    orow = lax.broadcasted_iota(jnp.int32, (LANES, tq), 0)
    for qi in range(nq):
        (_, l0, acc0), (_, l1, acc1) = state[qi, 0], state[qi, 1]
        o_t = jnp.where(orow < HEAD_DIM, acc0 * (1.0 / l0), acc1 * (1.0 / l1))
        o_ref[0, qi * tq:(qi + 1) * tq, :] = o_t.T.astype(BF16)


def _fox_attn(q, k, v, dcum):
    b, s, _ = q.shape
    tq, tk = min(FOX_TQ, s), min(FOX_TK, s)
    seq = lambda: pl.BlockSpec((1, s, LANES), lambda bi, j: (bi, 0, j))
    return pl.pallas_call(
        _fox_attn_kernel,
        out_shape=jax.ShapeDtypeStruct((b, s, HQ), BF16),
        grid=(b, HQ // LANES),
        in_specs=[seq(), seq(), seq(), pl.BlockSpec((1, s, LANES), lambda bi, j: (bi, 0, 0))],
        out_specs=seq(),
        scratch_shapes=[pltpu.VMEM((2, s, 2 * LANES), BF16), pltpu.VMEM((2, s, 2 * LANES), BF16),
                        pltpu.VMEM((s // tk, LANES, tk), BF16),
                        pltpu.VMEM((2, s // tq, 2 * LANES, tq), BF16)],
        compiler_params=_cparams("parallel", "parallel"),
        name="fox_attn",
    )(q, k, v, dcum)


def _mem_attn_kernel(q_ref, kv_ref, o_ref):
    lane = lax.broadcasted_iota(jnp.int32, (MEM_TQ, LANES), 1)
    for pair in range(HM // LANES):
        cols = slice(pair * LANES, (pair + 1) * LANES)
        kblk = kv_ref[0, :, cols]
        vblk = kv_ref[0, :, HM + pair * LANES:HM + (pair + 1) * LANES]
        qf = q_ref[0, :, cols].astype(F32)
        outs = []
        for a in range(2):
            keep = (lane < HEAD_DIM) if a == 0 else (lane >= HEAD_DIM)
            s = _dot_nt(jnp.where(keep, qf, 0.0).astype(BF16), kblk)
            p = jnp.exp(s - jnp.max(s, axis=-1, keepdims=True))
            l = jnp.sum(p, axis=-1, keepdims=True)
            outs.append(_dot(p.astype(BF16), vblk) * (1.0 / l))
        o_ref[0, :, cols] = jnp.where(lane < HEAD_DIM, outs[0], outs[1]).astype(BF16)


def _mem_attn(qc, kvm):
    b, s, _ = qc.shape
    return pl.pallas_call(
        _mem_attn_kernel,
        out_shape=jax.ShapeDtypeStruct((b, s, HM), BF16),
        grid=(b, s // MEM_TQ),
        in_specs=[pl.BlockSpec((1, MEM_TQ, HM), lambda bi, i: (bi, i, 0)),
                  pl.BlockSpec((1, MEM_LEN, 2 * HM), lambda bi, i: (bi, 0, 0))],
        out_specs=pl.BlockSpec((1, MEM_TQ, HM), lambda bi, i: (bi, i, 0)),
        compiler_params=_cparams("parallel", "parallel"),
        name="mem_attn",
    )(qc, kvm)


def _out_proj_kernel(s_ref, c_ref, x_ref, w_ref, g_ref, b_ref, o_ref):
    mix = _dot(s_ref[...], w_ref[0, :HQ, :]) + _dot(c_ref[...], w_ref[0, HQ:, :])
    o_ref[...] = _layer_norm(DN_ALPHA * x_ref[...] + mix, g_ref[...], b_ref[...])


def _out_proj_ln(self_out, cross_out, xf, w_all, layer, g, b):
    t = xf.shape[0]
    tm = min(t, PROJ_TM)
    row = lambda n: pl.BlockSpec((tm, n), lambda i: (i, 0))
    const = lambda r, c: pl.BlockSpec((r, c), lambda i: (0, 0))
    return pl.pallas_call(
        _out_proj_kernel,
        out_shape=jax.ShapeDtypeStruct((t, D_MODEL), F32),
        grid=(t // tm,),
        in_specs=[row(HQ), row(HM), row(D_MODEL),
                  pl.BlockSpec((1, D_MODEL, D_MODEL), lambda i: (layer, 0, 0)),
                  const(1, D_MODEL), const(1, D_MODEL)],
        out_specs=row(D_MODEL),
        compiler_params=_cparams("parallel"),
        name="out_proj_ln",
    )(self_out, cross_out, xf, w_all, g, b)


def _swiglu(xb, w_in_ref, w_out_ref):
    acc = None
    for lo, hi in FF_CHUNKS:
        ha = _dot(xb, w_in_ref[:, lo:hi])
        hb = _dot(xb, w_in_ref[:, D_FF + lo:D_FF + hi])
        act = (ha * (1.0 / (1.0 + jnp.exp(-ha))) * hb).astype(BF16)
        part = _dot(act, w_out_ref[lo:hi, :])
        acc = part if acc is None else acc + part
    return acc


def _dense_ffn_kernel(x_ref, w_in_ref, w_out_ref, g_ref, b_ref, o_ref):
    x = x_ref[...]
    ffn = _swiglu(x.astype(BF16), w_in_ref.at[0], w_out_ref.at[0])
    o_ref[...] = _layer_norm(DN_ALPHA * x + ffn, g_ref[...], b_ref[...])


def _dense_ffn_ln(xf, w_in, w_out, layer, g, b):
    t = xf.shape[0]
    tm = min(t, FFN_TM)
    row = pl.BlockSpec((tm, D_MODEL), lambda i: (i, 0))
    const = lambda r, c: pl.BlockSpec((r, c), lambda i: (0, 0))
    return pl.pallas_call(
        _dense_ffn_kernel,
        out_shape=jax.ShapeDtypeStruct((t, D_MODEL), F32),
        grid=(t // tm,),
        in_specs=[row, pl.BlockSpec((1, D_MODEL, 2 * D_FF), lambda i: (layer, 0, 0)),
                  pl.BlockSpec((1, D_FF, D_MODEL), lambda i: (layer, 0, 0)),
                  const(1, D_MODEL), const(1, D_MODEL)],
        out_specs=row,
        compiler_params=_cparams("parallel"),
        name="dense_ffn_ln",
    )(xf, w_in, w_out, g, b)


def _expert_kernel(be_ref, bx_ref, xs_ref, w_in_ref, w_out_ref, ys_ref):
    del be_ref
    i = pl.program_id(0)
    used = bx_ref[i] == i

    @pl.when(used)
    def _():
        ys_ref[...] = _swiglu(xs_ref[...].astype(BF16), w_in_ref.at[0, 0], w_out_ref.at[0, 0])

    @pl.when(jnp.logical_not(used))
    def _():
        ys_ref[...] = jnp.zeros_like(ys_ref)


def _expert_ffn(xs, w_in, w_out, layer, block_expert, block_src):
    n_slots = xs.shape[0]
    return pl.pallas_call(
        _expert_kernel,
        out_shape=jax.ShapeDtypeStruct((n_slots, D_MODEL), F32),
        grid_spec=pltpu.PrefetchScalarGridSpec(
            num_scalar_prefetch=2,
            grid=(n_slots // MOE_TM,),
            in_specs=[pl.BlockSpec((MOE_TM, D_MODEL), lambda i, be, bx: (bx[i], 0)),
                      pl.BlockSpec((1, 1, D_MODEL, 2 * D_FF),
                                   lambda i, be, bx: (layer, be[i], 0, 0)),
                      pl.BlockSpec((1, 1, D_FF, D_MODEL),
                                   lambda i, be, bx: (layer, be[i], 0, 0))],
            out_specs=pl.BlockSpec((MOE_TM, D_MODEL), lambda i, be, bx: (i, 0))),
        compiler_params=_cparams("arbitrary"),
        name="expert_ffn",
    )(block_expert, block_src, xs, w_in, w_out)


ROUTE_E1, ROUTE_E2, ROUTE_R1, ROUTE_R2, ROUTE_G1, ROUTE_G2 = range(6)


def _router_kernel(x_ref, w_ref, b_ref, route_ref, count_ref, carry_ref):
    @pl.when(pl.program_id(0) == 0)
    def _():
        carry_ref[...] = jnp.zeros_like(carry_ref)

    xs, ws = _split3(x_ref[...]), _split3(w_ref[...])
    logits = None
    for i, j in ((2, 0), (1, 1), (0, 2), (1, 0), (0, 1), (0, 0)):
        term = _dot(xs[i], ws[j])
        logits = term if logits is None else logits + term
    tm = logits.shape[0]
    lane = lax.broadcasted_iota(jnp.int32, (tm, LANES), 1).astype(F32)
    logits = jnp.where(lane < N_EXPERTS, logits + b_ref[...], -jnp.inf)
    m1 = jnp.max(logits, axis=-1, keepdims=True)
    e1 = jnp.min(jnp.where(logits == m1, lane, float(LANES)), axis=-1, keepdims=True)
    rest = jnp.where(lane == e1, -jnp.inf, logits)
    m2 = jnp.max(rest, axis=-1, keepdims=True)
    e2 = jnp.min(jnp.where(rest == m2, lane, float(LANES)), axis=-1, keepdims=True)
    ex = jnp.exp(m2 - m1)
    g1 = 1.0 / (1.0 + ex)
    g2 = ex * g1
    sel1, sel2 = lane == e1, lane == e2
    onehot = jnp.where(sel1 | sel2, 1.0, 0.0)
    tri = (lax.broadcasted_iota(jnp.int32, (tm, tm), 0)
           > lax.broadcasted_iota(jnp.int32, (tm, tm), 1)).astype(F32).astype(BF16)
    rank = _dot(tri, onehot.astype(BF16)) + carry_ref[...]
    r1 = jnp.sum(jnp.where(sel1, rank, 0.0), axis=-1, keepdims=True)
    r2 = jnp.sum(jnp.where(sel2, rank, 0.0), axis=-1, keepdims=True)
    total = carry_ref[...] + jnp.sum(onehot, axis=0, keepdims=True)
    carry_ref[...] = total
    count_ref[...] = total
    route = jnp.zeros((tm, LANES), F32)
    for ln, val in ((ROUTE_E1, e1), (ROUTE_E2, e2), (ROUTE_R1, r1), (ROUTE_R2, r2),
                    (ROUTE_G1, g1), (ROUTE_G2, g2)):
        route = jnp.where(lane == float(ln), val, route)
    route_ref[...] = route


def _router(xf, w_pad, b_pad):
    t = xf.shape[0]
    tm = min(t, ROUTER_TM)
    return pl.pallas_call(
        _router_kernel,
        out_shape=(jax.ShapeDtypeStruct((t, LANES), F32), jax.ShapeDtypeStruct((1, LANES), F32)),
        grid=(t // tm,),
        in_specs=[pl.BlockSpec((tm, D_MODEL), lambda i: (i, 0)),
                  pl.BlockSpec((D_MODEL, LANES), lambda i: (0, 0)),
                  pl.BlockSpec((1, LANES), lambda i: (0, 0))],
        out_specs=(pl.BlockSpec((tm, LANES), lambda i: (i, 0)),
                   pl.BlockSpec((1, LANES), lambda i: (0, 0))),
        scratch_shapes=[pltpu.VMEM((1, LANES), F32)],
        compiler_params=_cparams("arbitrary"),
        name="router",
    )(xf, w_pad, b_pad)


ROW_UNROLL = 8


def _row_copies(n, make_copy):
    def start(r, c):
        make_copy(r, 0).start()
        make_copy(r, 1).start()
        return c

    def wait(r, c):
        make_copy(r, 0).wait()
        make_copy(r, 1).wait()
        return c

    lax.fori_loop(0, n, start, 0, unroll=ROW_UNROLL)
    lax.fori_loop(0, n, wait, 0, unroll=ROW_UNROLL)


def _dispatch_kernel(dest_ref, pend_ref, x_ref, xs_hbm, zero_ref, sem, zsem):
    base = pl.program_id(0) * ROW_TM

    @pl.when(pl.program_id(0) == 0)
    def _():
        zero_ref[...] = jnp.zeros_like(zero_ref)

        def zero_copy(e):
            start = pl.multiple_of(pend_ref[e + 1] - MOE_TM, MOE_TM)
            return pltpu.make_async_copy(zero_ref, xs_hbm.at[pl.ds(start, MOE_TM)], zsem)

        def tail_copy(blk):
            return pltpu.make_async_copy(
                zero_ref, xs_hbm.at[pl.ds(pl.multiple_of(blk * MOE_TM, MOE_TM), MOE_TM)], zsem)

        first_tail = pend_ref[N_EXPERTS] // MOE_TM
        n_blocks = xs_hbm.shape[0] // MOE_TM
        for e in range(N_EXPERTS):
            @pl.when(pend_ref[e + 1] > pend_ref[e])
            def _():
                zero_copy(e).start()
        lax.fori_loop(first_tail, n_blocks, lambda blk, c: (tail_copy(blk).start(), c)[1], 0)
        for e in range(N_EXPERTS):
            @pl.when(pend_ref[e + 1] > pend_ref[e])
            def _():
                zero_copy(e).wait()
        lax.fori_loop(first_tail, n_blocks, lambda blk, c: (tail_copy(blk).wait(), c)[1], 0)

    def make_copy(r, k):
        return pltpu.make_async_copy(x_ref.at[pl.ds(r, 1)],
                                     xs_hbm.at[pl.ds(dest_ref[2 * (base + r) + k], 1)], sem)

    _row_copies(ROW_TM, make_copy)


def _dispatch(xf, dest, seg_ends, n_slots):
    t = xf.shape[0]
    return pl.pallas_call(
        _dispatch_kernel,
        out_shape=jax.ShapeDtypeStruct((n_slots, D_MODEL), F32),
        grid_spec=pltpu.PrefetchScalarGridSpec(
            num_scalar_prefetch=2,
            grid=(t // ROW_TM,),
            in_specs=[pl.BlockSpec((ROW_TM, D_MODEL), lambda i, d, p: (i, 0))],
            out_specs=pl.BlockSpec(memory_space=pl.ANY),
            scratch_shapes=[pltpu.VMEM((MOE_TM, D_MODEL), F32), pltpu.SemaphoreType.DMA(()),
                            pltpu.SemaphoreType.DMA(())]),
        compiler_params=_cparams("arbitrary"),
        name="moe_dispatch",
    )(dest, seg_ends, xf)


def _combine_kernel(dest_ref, x_ref, route_ref, g_ref, b_ref, ys_hbm, o_ref, ya, yb, sem):
    base = pl.program_id(0) * ROW_TM

    def make_copy(r, k):
        dst = ya if k == 0 else yb
        return pltpu.make_async_copy(ys_hbm.at[pl.ds(dest_ref[2 * (base + r) + k], 1)],
                                     dst.at[pl.ds(r, 1)], sem)

    _row_copies(ROW_TM, make_copy)
    route = route_ref[...]
    g1 = route[:, ROUTE_G1:ROUTE_G1 + 1]
    g2 = route[:, ROUTE_G2:ROUTE_G2 + 1]
    y = ya[...] * g1 + yb[...] * g2
    o_ref[...] = _layer_norm(DN_ALPHA * x_ref[...] + y, g_ref[...], b_ref[...])


def _combine_ln(xf, route, ys, dest, g, b):
    t = xf.shape[0]
    row = lambda n: pl.BlockSpec((ROW_TM, n), lambda i, d: (i, 0))
    const = pl.BlockSpec((1, D_MODEL), lambda i, d: (0, 0))
    return pl.pallas_call(
        _combine_kernel,
        out_shape=jax.ShapeDtypeStruct((t, D_MODEL), F32),
        grid_spec=pltpu.PrefetchScalarGridSpec(
            num_scalar_prefetch=1,
            grid=(t // ROW_TM,),
            in_specs=[row(D_MODEL), row(LANES), const, const, pl.BlockSpec(memory_space=pl.ANY)],
            out_specs=row(D_MODEL),
            scratch_shapes=[pltpu.VMEM((ROW_TM, D_MODEL), F32), pltpu.VMEM((ROW_TM, D_MODEL), F32),
                            pltpu.SemaphoreType.DMA(())]),
        compiler_params=_cparams("arbitrary"),
        name="moe_combine_ln",
    )(dest, xf, route, g, b, ys)


def _moe_ffn_ln(xf, w_router, b_router, w_in, w_out, layer, g, b):
    t = xf.shape[0]
    w_pad = jnp.zeros((D_MODEL, LANES), F32).at[:, :N_EXPERTS].set(w_router)
    b_pad = jnp.zeros((1, LANES), F32).at[0, :N_EXPERTS].set(b_router)
    route, counts = _router(xf, w_pad, b_pad)
    counts = counts[0, :N_EXPERTS].astype(jnp.int32)
    pcounts = (counts + MOE_TM - 1) // MOE_TM * MOE_TM
    pends = jnp.cumsum(pcounts)
    pstarts = pends - pcounts
    e = route[:, ROUTE_E1:ROUTE_E2 + 1].astype(jnp.int32)
    r = route[:, ROUTE_R1:ROUTE_R2 + 1].astype(jnp.int32)
    dest = (pstarts[e] + r).reshape(-1)
    n_blocks = 2 * t // MOE_TM + N_EXPERTS
    blk_start = jnp.arange(n_blocks, dtype=jnp.int32) * MOE_TM
    block_expert = jnp.minimum(jnp.sum(blk_start[:, None] >= pends[None, :], axis=1),
                               N_EXPERTS - 1).astype(jnp.int32)
    n_used = pends[-1] // MOE_TM
    block_src = jnp.minimum(jnp.arange(n_blocks, dtype=jnp.int32), n_used - 1)
    seg_ends = jnp.concatenate([jnp.zeros((1,), jnp.int32), pends.astype(jnp.int32)])
    xs = _dispatch(xf, dest, seg_ends, n_blocks * MOE_TM)
    ys = _expert_ffn(xs, w_in, w_out, layer, block_expert, block_src)
    return _combine_ln(xf, route, ys, dest, g, b)


def _dup_heads(w):
    d, n = w.shape
    w = w.reshape(d, n // HEAD_DIM, 1, HEAD_DIM)
    return jnp.broadcast_to(w, (d, n // HEAD_DIM, 2, HEAD_DIM)).reshape(d, 2 * n)


def kernel(x, mem, positions, w_in_swa, attn_sinks, w_in_fox, b_forget, w_mem_kv, w_out,
           ln_attn_g, ln_attn_b, w_ffn_in, w_ffn_out, w_router, b_router, w_exp_in,
           w_exp_out, ln_ffn_g, ln_ffn_b):
    b, s, d = x.shape
    t = b * s
    hkv = SWA_KV_HEADS * HEAD_DIM
    cos_t, sin_t = _rope_tables(positions)
    mem2d = mem.reshape(b * MEM_LEN, d)
    xf = x.reshape(t, d)
    w_out_b = w_out.astype(BF16)
    w_ffn_in_b, w_ffn_out_b = w_ffn_in.astype(BF16), w_ffn_out.astype(BF16)
    w_exp_in_b, w_exp_out_b = w_exp_in.astype(BF16), w_exp_out.astype(BF16)
    for i in range(DEPTH):
        j = i // 2
        kvm = _mem_kv(mem2d, w_mem_kv[i].astype(BF16)).reshape(b, MEM_LEN, 2 * HM)
        if i % 2 == 0:
            w = w_in_swa[j]
            w = jnp.concatenate([w[:, :HQ], _dup_heads(w[:, HQ:HQ + hkv]),
                                 _dup_heads(w[:, HQ + hkv:HQ + 2 * hkv]), w[:, HQ + 2 * hkv:]],
                                axis=1).astype(BF16)
            q, kk, vv, qc = _swa_proj(xf, w, cos_t, sin_t)
            self_out = _swa_attn(q.reshape(b, s, HQ), kk.reshape(b, s, HKV2),
                                 vv.reshape(b, s, HKV2), attn_sinks[j])
            qc = qc.reshape(b, s, HM)
        else:
            w = jnp.pad(w_in_fox[j], ((0, 0), (0, FOX_COLS - w_in_fox.shape[-1]))).astype(BF16)
            bf_row = jnp.pad(b_forget[j], (0, LANES - SELF_HEADS))[None, :]
            q, k, v, qc, dcum = _fox_proj(xf.reshape(b, s, d), w, bf_row)
            self_out = _fox_attn(q, k, v, dcum)
        cross_out = _mem_attn(qc, kvm)
        xf = _out_proj_ln(self_out.reshape(t, HQ), cross_out.reshape(t, HM), xf,
                          w_out_b, i, ln_attn_g[i][None, :], ln_attn_b[i][None, :])
        g, bb = ln_ffn_g[i][None, :], ln_ffn_b[i][None, :]
        if i % 2 == 0:
            xf = _dense_ffn_ln(xf, w_ffn_in_b, w_ffn_out_b, j, g, bb)
        else:
            xf = _moe_ffn_ln(xf, w_router[j], b_router[j], w_exp_in_b, w_exp_out_b, j, g, bb)
    return xf.reshape(b, s, d)
```

```python
import functools

import jax
import jax.numpy as jnp
from jax import lax
from jax.experimental import pallas as pl
from jax.experimental.pallas import tpu as pltpu

D_MODEL = 1024
DEPTH = 4
CHUNK = 64
HEAD_DIM = 64
MEM_HEADS = 4
SELF_HEADS = 12
SWA_KV_HEADS = 3
SWA_GROUP = SELF_HEADS // SWA_KV_HEADS
WINDOW_CHUNKS = 2
ROT_DIM = HEAD_DIM // 4
ROT_HALF = ROT_DIM // 2
ROPE_THETA = 500000.0
MEM_LEN = 256
D_FF = 2816
N_EXPERTS = 8
LN_EPS = 1e-5
NEG = -1e30
ATTN_SCALE = HEAD_DIM ** -0.5
DN_ALPHA = (2.0 * DEPTH) ** 0.25

HQ = SELF_HEADS * HEAD_DIM
HM = MEM_HEADS * HEAD_DIM
HKV2 = 2 * SWA_KV_HEADS * HEAD_DIM
SWA_COLS = HQ + 2 * HKV2 + HM
FOX_COLS = 3 * HQ + HM + 128

LANES = 128
VMEM_LIMIT_BYTES = 56 * 1024 * 1024

PROJ_TM = 512
FOX_PROJ_TM = 256
SWA_TQ = 256
SWA_SUB = 128
FOX_TQ = 2048
FOX_TK = 512
MEM_TQ = 512
FFN_TM = 256
MXU_DIM = 256
FF_CHUNKS = ((0, 6 * MXU_DIM), (6 * MXU_DIM, D_FF))
ROUTER_TM = 512
MOE_TM = 256
ROW_TM = 256

F32 = jnp.float32
BF16 = jnp.bfloat16


def _cparams(*sem):
    return pltpu.CompilerParams(dimension_semantics=sem, vmem_limit_bytes=VMEM_LIMIT_BYTES)


def _layer_norm(y, g, b):
    mu = jnp.mean(y, axis=-1, keepdims=True)
    d = y - mu
    var = jnp.mean(d * d, axis=-1, keepdims=True)
    return d * lax.rsqrt(var + LN_EPS) * g + b


def _split3(a):
    hi = a.astype(BF16)
    r1 = a - hi.astype(F32)
    mid = r1.astype(BF16)
    lo = (r1 - mid.astype(F32)).astype(BF16)
    return hi, mid, lo


def _dot(a, b):
    return jnp.dot(a, b, preferred_element_type=F32)


def _dot_nt(a, b):
    return lax.dot_general(a, b, (((1,), (1,)), ((), ())), preferred_element_type=F32)


def _rope_table_kernel(pos_ref, freq_ref, cos_ref, sin_ref):
    ang = pos_ref[...].astype(F32) * freq_ref[...]
    sub = lax.broadcasted_iota(jnp.int32, ang.shape, 1) & (HEAD_DIM - 1)
    c, s = jnp.cos(ang), jnp.sin(ang)
    cos_ref[...] = jnp.where(sub < ROT_DIM, c, 1.0)
    sin_ref[...] = jnp.where(sub < ROT_HALF, -s, jnp.where(sub < ROT_DIM, s, 0.0))


def _rope_tables(positions):
    t = positions.size
    tm = min(t, 2048)
    inv_freq = ROPE_THETA ** (-jnp.arange(0, ROT_DIM, 2, dtype=F32) / ROT_DIM)
    lane = jnp.arange(LANES) % HEAD_DIM % ROT_HALF
    freq = inv_freq[lane][None, :]
    out = jax.ShapeDtypeStruct((t, LANES), F32)
    return pl.pallas_call(
        _rope_table_kernel,
        out_shape=(out, out),
        grid=(t // tm,),
        in_specs=[pl.BlockSpec((tm, 1), lambda i: (i, 0)),
                  pl.BlockSpec((1, LANES), lambda i: (0, 0))],
        out_specs=(pl.BlockSpec((tm, LANES), lambda i: (i, 0)),
                   pl.BlockSpec((tm, LANES), lambda i: (i, 0))),
        compiler_params=_cparams("parallel"),
        name="rope_tables",
    )(positions.reshape(t, 1), freq)


def _mem_kv_kernel(m_ref, w_ref, o_ref):
    o_ref[...] = _dot(m_ref[...].astype(BF16), w_ref[...]).astype(BF16)


def _mem_kv(mem2d, w):
    n = mem2d.shape[0]
    tm = min(n, PROJ_TM)
    return pl.pallas_call(
        _mem_kv_kernel,
        out_shape=jax.ShapeDtypeStruct((n, 2 * HM), BF16),
        grid=(n // tm,),
        in_specs=[pl.BlockSpec((tm, D_MODEL), lambda i: (i, 0)),
                  pl.BlockSpec((D_MODEL, 2 * HM), lambda i: (0, 0))],
        out_specs=pl.BlockSpec((tm, 2 * HM), lambda i: (i, 0)),
        compiler_params=_cparams("parallel"),
        name="mem_kv_proj",
    )(mem2d, w)


def _swa_proj_kernel(x_ref, w_ref, cos_ref, sin_ref, q_ref, kk_ref, vv_ref, qc_ref):
    acc = _dot(x_ref[...].astype(BF16), w_ref[...])
    cos, sin = cos_ref[...], sin_ref[...]
    first = (lax.broadcasted_iota(jnp.int32, cos.shape, 1) & (HEAD_DIM - 1)) < ROT_HALF

    def rope(t):
        fwd = pltpu.roll(t, LANES - ROT_HALF, axis=1)
        bwd = pltpu.roll(t, ROT_HALF, axis=1)
        return t * cos + jnp.where(first, fwd, bwd) * sin

    for g in range(HQ // LANES):
        sl = slice(g * LANES, (g + 1) * LANES)
        q_ref[:, sl] = (rope(acc[:, sl]) * ATTN_SCALE).astype(BF16)
    for g in range(HKV2 // LANES):
        src = slice(HQ + g * LANES, HQ + (g + 1) * LANES)
        kk_ref[:, g * LANES:(g + 1) * LANES] = rope(acc[:, src]).astype(BF16)
    vv_ref[...] = acc[:, HQ + HKV2:HQ + 2 * HKV2].astype(BF16)
    qc_ref[...] = (acc[:, HQ + 2 * HKV2:] * ATTN_SCALE).astype(BF16)


def _swa_proj(xf, w, cos_t, sin_t):
    t = xf.shape[0]
    tm = min(t, PROJ_TM)
    row = lambda n: pl.BlockSpec((tm, n), lambda i: (i, 0))
    return pl.pallas_call(
        _swa_proj_kernel,
        out_shape=(jax.ShapeDtypeStruct((t, HQ), BF16), jax.ShapeDtypeStruct((t, HKV2), BF16),
                   jax.ShapeDtypeStruct((t, HKV2), BF16), jax.ShapeDtypeStruct((t, HM), BF16)),
        grid=(t // tm,),
        in_specs=[row(D_MODEL), pl.BlockSpec((D_MODEL, SWA_COLS), lambda i: (0, 0)),
                  row(LANES), row(LANES)],
        out_specs=(row(HQ), row(HKV2), row(HKV2), row(HM)),
        compiler_params=_cparams("parallel"),
        name="swa_proj",
    )(xf, w, cos_t, sin_t)


def _swa_attn_kernel(sink_ref, q_ref, kp_ref, kc_ref, vp_ref, vc_ref, o_ref):
    first_tile = pl.program_id(1) == 0
    half = SWA_TQ - SWA_SUB
    lane = lax.broadcasted_iota(jnp.int32, (SWA_SUB, LANES), 1)
    shift = CHUNK.bit_length() - 1
    rc = lax.broadcasted_iota(jnp.int32, (SWA_SUB, 2 * SWA_SUB), 0) >> shift
    cc = lax.broadcasted_iota(jnp.int32, (SWA_SUB, 2 * SWA_SUB), 1) >> shift
    band = (cc >= rc) & (cc <= rc + WINDOW_CHUNKS)
    band_prev = band & ((cc >= SWA_SUB // CHUNK) | jnp.logical_not(first_tile))
    for sub in range(SWA_TQ // SWA_SUB):
        rows = slice(sub * SWA_SUB, (sub + 1) * SWA_SUB)
        mask = band_prev if sub == 0 else band
        for h in range(SWA_KV_HEADS):
            cols = slice(h * LANES, (h + 1) * LANES)
            if sub == 0:
                kcat = jnp.concatenate([kp_ref[0, half:, cols], kc_ref[0, :SWA_SUB, cols]], axis=0)
                vcat = jnp.concatenate([vp_ref[0, half:, cols], vc_ref[0, :SWA_SUB, cols]], axis=0)
            else:
                kcat = kc_ref[0, :, cols]
                vcat = vc_ref[0, :, cols]
            for pair in range(SWA_GROUP // 2):
                g = h * (SWA_GROUP // 2) + pair
                qcols = slice(g * LANES, (g + 1) * LANES)
                qf = q_ref[0, rows, qcols].astype(F32)
                outs = []
                for a in range(2):
                    keep = (lane < HEAD_DIM) if a == 0 else (lane >= HEAD_DIM)
                    qa = jnp.where(keep, qf, 0.0).astype(BF16)
                    s = jnp.where(mask, _dot_nt(qa, kcat), NEG)
                    sink = sink_ref[2 * g + a]
                    m = jnp.maximum(jnp.max(s, axis=-1, keepdims=True), sink)
                    p = jnp.exp(s - m)
                    den = jnp.sum(p, axis=-1, keepdims=True) + jnp.exp(sink - m)
                    probs = (p * (1.0 / den)).astype(BF16)
                    outs.append(_dot(probs, vcat))
                o = jnp.where(lane < HEAD_DIM, outs[0], outs[1])
                o_ref[0, rows, qcols] = o.astype(BF16)


def _swa_attn(q, kk, vv, sinks):
    b, s, _ = q.shape
    nt = s // SWA_TQ
    cur = lambda n: pl.BlockSpec((1, SWA_TQ, n), lambda bi, i: (bi, i, 0))
    prev = lambda n: pl.BlockSpec((1, SWA_TQ, n), lambda bi, i: (bi, jnp.maximum(i - 1, 0), 0))
    return pl.pallas_call(
        _swa_attn_kernel,
        out_shape=jax.ShapeDtypeStruct((b, s, HQ), BF16),
        grid=(b, nt),
        in_specs=[pl.BlockSpec(memory_space=pltpu.SMEM),
                  cur(HQ), prev(HKV2), cur(HKV2), prev(HKV2), cur(HKV2)],
        out_specs=cur(HQ),
        compiler_params=_cparams("parallel", "parallel"),
        name="swa_attn",
    )(sinks, q, kk, kk, vv, vv)


def _fox_proj_kernel(x_ref, w_ref, bf_ref, q_ref, k_ref, v_ref, qc_ref, d_ref, carry_ref):
    @pl.when(pl.program_id(1) == 0)
    def _():
        carry_ref[...] = jnp.zeros_like(carry_ref)

    acc = _dot(x_ref[0].astype(BF16), w_ref[...])
    q_ref[0] = (acc[:, :HQ] * ATTN_SCALE).astype(BF16)
    k_ref[0] = acc[:, HQ:2 * HQ].astype(BF16)
    v_ref[0] = acc[:, 2 * HQ:3 * HQ].astype(BF16)
    qc_ref[0] = (acc[:, 3 * HQ:3 * HQ + HM] * ATTN_SCALE).astype(BF16)
    z = acc[:, 3 * HQ + HM:] + bf_ref[...]
    log_f = jnp.minimum(z, 0.0) - jnp.log(1.0 + jnp.exp(-jnp.abs(z)))
    tm = z.shape[0]
    tri = (lax.broadcasted_iota(jnp.int32, (tm, tm), 0)
           >= lax.broadcasted_iota(jnp.int32, (tm, tm), 1)).astype(F32).astype(BF16)
    hi, mid, lo = _split3(log_f)
    cum = (_dot(tri, lo) + _dot(tri, mid)) + _dot(tri, hi) + carry_ref[...]
    d_ref[0] = cum
    carry_ref[...] = cum[tm - 1:tm, :]


def _fox_proj(x3, w, bf_row):
    b, s, _ = x3.shape
    tm = min(s, FOX_PROJ_TM)
    row = lambda n: pl.BlockSpec((1, tm, n), lambda bi, i: (bi, i, 0))
    sds = lambda n, dt: jax.ShapeDtypeStruct((b, s, n), dt)
    return pl.pallas_call(
        _fox_proj_kernel,
        out_shape=(sds(HQ, BF16), sds(HQ, BF16), sds(HQ, BF16), sds(HM, BF16), sds(LANES, F32)),
        grid=(b, s // tm),
        in_specs=[row(D_MODEL), pl.BlockSpec((D_MODEL, FOX_COLS), lambda bi, i: (0, 0)),
                  pl.BlockSpec((1, LANES), lambda bi, i: (0, 0))],
        out_specs=(row(HQ), row(HQ), row(HQ), row(HM), row(LANES)),
        scratch_shapes=[pltpu.VMEM((1, LANES), F32)],
        compiler_params=_cparams("arbitrary", "arbitrary"),
        name="fox_proj",
    )(x3, w, bf_row)


def _fox_attn_kernel(q_ref, k_ref, v_ref, d_ref, o_ref, kaug_ref, qaug_ref, vt_ref, qt_ref):
    s_len = q_ref.shape[1]
    tq, tk = qt_ref.shape[3], vt_ref.shape[2]
    pair = pl.program_id(1)
    lane = lax.broadcasted_iota(jnp.int32, (s_len, LANES), 1)
    prow = lax.broadcasted_iota(jnp.int32, (3 * LANES, 2 * LANES), 0)
    pcol = lax.broadcasted_iota(jnp.int32, (3 * LANES, 2 * LANES), 1)
    term = prow >> 7
    kf = k_ref[0].astype(F32)
    dsplit = jnp.concatenate(_split3(d_ref[0]), axis=1)
    for a in range(2):
        head = 2 * pair + a
        from_head = (prow & (LANES - 1)) == head
        place = (jnp.where(from_head & (pcol == term), 1.0, 0.0)
                 + jnp.where(from_head & (pcol == LANES + 3 + term), -1.0, 0.0)).astype(BF16)
        extra = _dot(dsplit, place)
        q_extra = extra[:, :LANES] + jnp.where((lane >= 3) & (lane < 6), 1.0, 0.0)
        k_extra = extra[:, LANES:] + jnp.where(lane < 3, 1.0, 0.0)
        keep = (lane < HEAD_DIM) if a == 0 else (lane >= HEAD_DIM)
        kaug_ref[a, :, :LANES] = jnp.where(keep, kf, 0.0).astype(BF16)
        kaug_ref[a, :, LANES:] = k_extra.astype(BF16)
        qaug_ref[a, :, :LANES] = q_ref[0]
        qaug_ref[a, :, LANES:] = q_extra.astype(BF16)
    for j in range(s_len // tk):
        vt_ref[j] = v_ref[0, j * tk:(j + 1) * tk, :].astype(F32).T.astype(BF16)

    nq = s_len // tq
    for qi in range(nq):
        for a in range(2):
            qt_ref[a, qi] = qaug_ref[a, qi * tq:(qi + 1) * tq, :].astype(F32).T.astype(BF16)

    def tile(kb, qi, a, state):
        q_off = max(0, kb * tk - qi * tq)
        width = tq - q_off
        masked = (kb + 1) * tk - 1 > qi * tq + q_off
        m0, l0, acc0 = state
        m, l, acc = m0[:, q_off:], l0[:, q_off:], acc0[:, q_off:]
        s = _dot(kaug_ref[a, kb * tk:(kb + 1) * tk, :], qt_ref[a, qi, :, q_off:])
        if masked:
            n_mask = min(width, pl.cdiv((kb + 1) * tk - (qi * tq + q_off), LANES) * LANES)
            k_pos = kb * tk + lax.broadcasted_iota(jnp.int32, (tk, n_mask), 0)
            q_pos = qi * tq + q_off + lax.broadcasted_iota(jnp.int32, (tk, n_mask), 1)
            head_cols = jnp.where(k_pos <= q_pos, s[:, :n_mask], NEG)
            s = head_cols if n_mask == width else jnp.concatenate([head_cols, s[:, n_mask:]], axis=1)
        m_new = jnp.maximum(m, jnp.max(s, axis=0, keepdims=True))
        alpha = jnp.exp(m - m_new)
        p = jnp.exp(s - m_new)
        l = alpha * l + jnp.sum(p, axis=0, keepdims=True)
        acc = alpha * acc + _dot(vt_ref[kb], p.astype(BF16))
        if q_off:
            m_new = jnp.concatenate([m0[:, :q_off], m_new], axis=1)
            l = jnp.concatenate([l0[:, :q_off], l], axis=1)
            acc = jnp.concatenate([acc0[:, :q_off], acc], axis=1)
        return m_new, l, acc

    state = {(qi, a): (jnp.full((1, tq), NEG, F32), jnp.zeros((1, tq), F32),
                       jnp.zeros((LANES, tq), F32)) for qi in range(nq) for a in range(2)}
    for kb in range(s_len // tk):
        for qi in range(kb * tk // tq, nq):
            for a in range(2):
                state[qi, a] = tile(kb, qi, a, state[qi, a])
    orow = lax.broadcasted_iota(jnp.int32, (LANES, tq), 0)
    for qi in range(nq):
        (_, l0, acc0), (_, l1, acc1) = state[qi, 0], state[qi, 1]
        o_t = jnp.where(orow < HEAD_DIM, acc0 * (1.0 / l0), acc1 * (1.0 / l1))
        o_ref[0, qi * tq:(qi + 1) * tq, :] = o_t.T.astype(BF16)


def _fox_attn(q, k, v, dcum):
    b, s, _ = q.shape
    tq, tk = min(FOX_TQ, s), min(FOX_TK, s)
    seq = lambda: pl.BlockSpec((1, s, LANES), lambda bi, j: (bi, 0, j))
    return pl.pallas_call(
        _fox_attn_kernel,
        out_shape=jax.ShapeDtypeStruct((b, s, HQ), BF16),
        grid=(b, HQ // LANES),
        in_specs=[seq(), seq(), seq(), pl.BlockSpec((1, s, LANES), lambda bi, j: (bi, 0, 0))],
        out_specs=seq(),
        scratch_shapes=[pltpu.VMEM((2, s, 2 * LANES), BF16), pltpu.VMEM((2, s, 2 * LANES), BF16),
                        pltpu.VMEM((s // tk, LANES, tk), BF16),
                        pltpu.VMEM((2, s // tq, 2 * LANES, tq), BF16)],
        compiler_params=_cparams("parallel", "parallel"),
        name="fox_attn",
    )(q, k, v, dcum)


def _mem_attn_kernel(q_ref, kv_ref, o_ref):
    lane = lax.broadcasted_iota(jnp.int32, (MEM_TQ, LANES), 1)
    for pair in range(HM // LANES):
        cols = slice(pair * LANES, (pair + 1) * LANES)
        kblk = kv_ref[0, :, cols]
        vblk = kv_ref[0, :, HM + pair * LANES:HM + (pair + 1) * LANES]
        qf = q_ref[0, :, cols].astype(F32)
        outs = []
        for a in range(2):
            keep = (lane < HEAD_DIM) if a == 0 else (lane >= HEAD_DIM)
            s = _dot_nt(jnp.where(keep, qf, 0.0).astype(BF16), kblk)
            p = jnp.exp(s - jnp.max(s, axis=-1, keepdims=True))
            l = jnp.sum(p, axis=-1, keepdims=True)
            outs.append(_dot(p.astype(BF16), vblk) * (1.0 / l))
        o_ref[0, :, cols] = jnp.where(lane < HEAD_DIM, outs[0], outs[1]).astype(BF16)


def _mem_attn(qc, kvm):
    b, s, _ = qc.shape
    return pl.pallas_call(
        _mem_attn_kernel,
        out_shape=jax.ShapeDtypeStruct((b, s, HM), BF16),
        grid=(b, s // MEM_TQ),
        in_specs=[pl.BlockSpec((1, MEM_TQ, HM), lambda bi, i: (bi, i, 0)),
                  pl.BlockSpec((1, MEM_LEN, 2 * HM), lambda bi, i: (bi, 0, 0))],
        out_specs=pl.BlockSpec((1, MEM_TQ, HM), lambda bi, i: (bi, i, 0)),
        compiler_params=_cparams("parallel", "parallel"),
        name="mem_attn",
    )(qc, kvm)


def _out_proj_kernel(s_ref, c_ref, x_ref, w_ref, g_ref, b_ref, o_ref):
    mix = _dot(s_ref[...], w_ref[0, :HQ, :]) + _dot(c_ref[...], w_ref[0, HQ:, :])
    o_ref[...] = _layer_norm(DN_ALPHA * x_ref[...] + mix, g_ref[...], b_ref[...])


def _out_proj_ln(self_out, cross_out, xf, w_all, layer, g, b):
    t = xf.shape[0]
    tm = min(t, PROJ_TM)
    row = lambda n: pl.BlockSpec((tm, n), lambda i: (i, 0))
    const = lambda r, c: pl.BlockSpec((r, c), lambda i: (0, 0))
    return pl.pallas_call(
        _out_proj_kernel,
        out_shape=jax.ShapeDtypeStruct((t, D_MODEL), F32),
        grid=(t // tm,),
        in_specs=[row(HQ), row(HM), row(D_MODEL),
                  pl.BlockSpec((1, D_MODEL, D_MODEL), lambda i: (layer, 0, 0)),
                  const(1, D_MODEL), const(1, D_MODEL)],
        out_specs=row(D_MODEL),
        compiler_params=_cparams("parallel"),
        name="out_proj_ln",
    )(self_out, cross_out, xf, w_all, g, b)


def _swiglu(xb, w_in_ref, w_out_ref):
    acc = None
    for lo, hi in FF_CHUNKS:
        ha = _dot(xb, w_in_ref[:, lo:hi])
        hb = _dot(xb, w_in_ref[:, D_FF + lo:D_FF + hi])
        act = (ha * (1.0 / (1.0 + jnp.exp(-ha))) * hb).astype(BF16)
        part = _dot(act, w_out_ref[lo:hi, :])
        acc = part if acc is None else acc + part
    return acc


def _dense_ffn_kernel(x_ref, w_in_ref, w_out_ref, g_ref, b_ref, o_ref):
    x = x_ref[...]
    ffn = _swiglu(x.astype(BF16), w_in_ref.at[0], w_out_ref.at[0])
    o_ref[...] = _layer_norm(DN_ALPHA * x + ffn, g_ref[...], b_ref[...])


def _dense_ffn_ln(xf, w_in, w_out, layer, g, b):
    t = xf.shape[0]
    tm = min(t, FFN_TM)
    row = pl.BlockSpec((tm, D_MODEL), lambda i: (i, 0))
    const = lambda r, c: pl.BlockSpec((r, c), lambda i: (0, 0))
    return pl.pallas_call(
        _dense_ffn_kernel,
        out_shape=jax.ShapeDtypeStruct((t, D_MODEL), F32),
        grid=(t // tm,),
        in_specs=[row, pl.BlockSpec((1, D_MODEL, 2 * D_FF), lambda i: (layer, 0, 0)),
                  pl.BlockSpec((1, D_FF, D_MODEL), lambda i: (layer, 0, 0)),
                  const(1, D_MODEL), const(1, D_MODEL)],
        out_specs=row,
        compiler_params=_cparams("parallel"),
        name="dense_ffn_ln",
    )(xf, w_in, w_out, g, b)


def _expert_kernel(be_ref, bx_ref, xs_ref, w_in_ref, w_out_ref, ys_ref):
    del be_ref
    i = pl.program_id(0)
    used = bx_ref[i] == i

    @pl.when(used)
    def _():
        ys_ref[...] = _swiglu(xs_ref[...].astype(BF16), w_in_ref.at[0, 0], w_out_ref.at[0, 0])

    @pl.when(jnp.logical_not(used))
    def _():
        ys_ref[...] = jnp.zeros_like(ys_ref)


def _expert_ffn(xs, w_in, w_out, layer, block_expert, block_src):
    n_slots = xs.shape[0]
    return pl.pallas_call(
        _expert_kernel,
        out_shape=jax.ShapeDtypeStruct((n_slots, D_MODEL), F32),
        grid_spec=pltpu.PrefetchScalarGridSpec(
            num_scalar_prefetch=2,
            grid=(n_slots // MOE_TM,),
            in_specs=[pl.BlockSpec((MOE_TM, D_MODEL), lambda i, be, bx: (bx[i], 0)),
                      pl.BlockSpec((1, 1, D_MODEL, 2 * D_FF),
                                   lambda i, be, bx: (layer, be[i], 0, 0)),
                      pl.BlockSpec((1, 1, D_FF, D_MODEL),
                                   lambda i, be, bx: (layer, be[i], 0, 0))],
            out_specs=pl.BlockSpec((MOE_TM, D_MODEL), lambda i, be, bx: (i, 0))),
        compiler_params=_cparams("arbitrary"),
        name="expert_ffn",
    )(block_expert, block_src, xs, w_in, w_out)


ROUTE_E1, ROUTE_E2, ROUTE_R1, ROUTE_R2, ROUTE_G1, ROUTE_G2 = range(6)


def _router_kernel(x_ref, w_ref, b_ref, route_ref, count_ref, carry_ref):
    @pl.when(pl.program_id(0) == 0)
    def _():
        carry_ref[...] = jnp.zeros_like(carry_ref)

    xs, ws = _split3(x_ref[...]), _split3(w_ref[...])
    logits = (_dot(xs[1], ws[0]) + _dot(xs[0], ws[1])) + _dot(xs[0], ws[0])
    tm = logits.shape[0]
    lane = lax.broadcasted_iota(jnp.int32, (tm, LANES), 1).astype(F32)
    logits = jnp.where(lane < N_EXPERTS, logits + b_ref[...], -jnp.inf)
    m1 = jnp.max(logits, axis=-1, keepdims=True)
    e1 = jnp.min(jnp.where(logits == m1, lane, float(LANES)), axis=-1, keepdims=True)
    rest = jnp.where(lane == e1, -jnp.inf, logits)
    m2 = jnp.max(rest, axis=-1, keepdims=True)
    e2 = jnp.min(jnp.where(rest == m2, lane, float(LANES)), axis=-1, keepdims=True)
    ex = jnp.exp(m2 - m1)
    g1 = 1.0 / (1.0 + ex)
    g2 = ex * g1
    sel1, sel2 = lane == e1, lane == e2
    onehot = jnp.where(sel1 | sel2, 1.0, 0.0)
    tri = (lax.broadcasted_iota(jnp.int32, (tm, tm), 0)
           > lax.broadcasted_iota(jnp.int32, (tm, tm), 1)).astype(F32).astype(BF16)
    rank = _dot(tri, onehot.astype(BF16)) + carry_ref[...]
    r1 = jnp.sum(jnp.where(sel1, rank, 0.0), axis=-1, keepdims=True)
    r2 = jnp.sum(jnp.where(sel2, rank, 0.0), axis=-1, keepdims=True)
    total = carry_ref[...] + jnp.sum(onehot, axis=0, keepdims=True)
    carry_ref[...] = total
    count_ref[...] = total
    route = jnp.zeros((tm, LANES), F32)
    for ln, val in ((ROUTE_E1, e1), (ROUTE_E2, e2), (ROUTE_R1, r1), (ROUTE_R2, r2),
                    (ROUTE_G1, g1), (ROUTE_G2, g2)):
        route = jnp.where(lane == float(ln), val, route)
    route_ref[...] = route


def _router(xf, w_pad, b_pad):
    t = xf.shape[0]
    tm = min(t, ROUTER_TM)
    return pl.pallas_call(
        _router_kernel,
        out_shape=(jax.ShapeDtypeStruct((t, LANES), F32), jax.ShapeDtypeStruct((1, LANES), F32)),
        grid=(t // tm,),
        in_specs=[pl.BlockSpec((tm, D_MODEL), lambda i: (i, 0)),
                  pl.BlockSpec((D_MODEL, LANES), lambda i: (0, 0)),
                  pl.BlockSpec((1, LANES), lambda i: (0, 0))],
        out_specs=(pl.BlockSpec((tm, LANES), lambda i: (i, 0)),
                   pl.BlockSpec((1, LANES), lambda i: (0, 0))),
        scratch_shapes=[pltpu.VMEM((1, LANES), F32)],
        compiler_params=_cparams("arbitrary"),
        name="router",
    )(xf, w_pad, b_pad)


def _row_copies(n, make_copy):
    for r in range(n):
        make_copy(r, 0).start(priority=0)
        make_copy(r, 1).start(priority=1)
    for r in range(n):
        make_copy(r, 0).wait()
        make_copy(r, 1).wait()


def _dispatch_kernel(dest_ref, pend_ref, x_ref, xs_hbm, zero_ref, sem, zsem):
    base = pl.program_id(0) * ROW_TM

    @pl.when(pl.program_id(0) == 0)
    def _():
        zero_ref[...] = jnp.zeros_like(zero_ref)

        def zero_copy(e):
            start = pl.multiple_of(pend_ref[e + 1] - MOE_TM, MOE_TM)
            return pltpu.make_async_copy(zero_ref, xs_hbm.at[pl.ds(start, MOE_TM)], zsem)

        def tail_copy(blk):
            return pltpu.make_async_copy(
                zero_ref, xs_hbm.at[pl.ds(pl.multiple_of(blk * MOE_TM, MOE_TM), MOE_TM)], zsem)

        first_tail = pend_ref[N_EXPERTS] // MOE_TM
        n_blocks = xs_hbm.shape[0] // MOE_TM
        for e in range(N_EXPERTS):
            @pl.when(pend_ref[e + 1] > pend_ref[e])
            def _():
                zero_copy(e).start()
        lax.fori_loop(first_tail, n_blocks, lambda blk, c: (tail_copy(blk).start(), c)[1], 0)
        for e in range(N_EXPERTS):
            @pl.when(pend_ref[e + 1] > pend_ref[e])
            def _():
                zero_copy(e).wait()
        lax.fori_loop(first_tail, n_blocks, lambda blk, c: (tail_copy(blk).wait(), c)[1], 0)

    def make_copy(r, k):
        return pltpu.make_async_copy(x_ref.at[pl.ds(r, 1)],
                                     xs_hbm.at[pl.ds(dest_ref[2 * (base + r) + k], 1)], sem)

    _row_copies(ROW_TM, make_copy)


def _dispatch(xf, dest, seg_ends, n_slots):
    t = xf.shape[0]
    return pl.pallas_call(
        _dispatch_kernel,
        out_shape=jax.ShapeDtypeStruct((n_slots, D_MODEL), F32),
        grid_spec=pltpu.PrefetchScalarGridSpec(
            num_scalar_prefetch=2,
            grid=(t // ROW_TM,),
            in_specs=[pl.BlockSpec((ROW_TM, D_MODEL), lambda i, d, p: (i, 0))],
            out_specs=pl.BlockSpec(memory_space=pl.ANY),
            scratch_shapes=[pltpu.VMEM((MOE_TM, D_MODEL), F32), pltpu.SemaphoreType.DMA(()),
                            pltpu.SemaphoreType.DMA(())]),
        compiler_params=_cparams("arbitrary"),
        name="moe_dispatch",
    )(dest, seg_ends, xf)


def _combine_kernel(dest_ref, x_ref, route_ref, g_ref, b_ref, ys_hbm, o_ref, ya, yb, sem):
    base = pl.program_id(0) * ROW_TM

    def make_copy(r, k):
        dst = ya if k == 0 else yb
        return pltpu.make_async_copy(ys_hbm.at[pl.ds(dest_ref[2 * (base + r) + k], 1)],
                                     dst.at[pl.ds(r, 1)], sem)

    _row_copies(ROW_TM, make_copy)
    route = route_ref[...]
    g1 = route[:, ROUTE_G1:ROUTE_G1 + 1]
    g2 = route[:, ROUTE_G2:ROUTE_G2 + 1]
    y = ya[...] * g1 + yb[...] * g2
    o_ref[...] = _layer_norm(DN_ALPHA * x_ref[...] + y, g_ref[...], b_ref[...])


def _combine_ln(xf, route, ys, dest, g, b):
    t = xf.shape[0]
    row = lambda n: pl.BlockSpec((ROW_TM, n), lambda i, d: (i, 0))
    const = pl.BlockSpec((1, D_MODEL), lambda i, d: (0, 0))
    return pl.pallas_call(
        _combine_kernel,
        out_shape=jax.ShapeDtypeStruct((t, D_MODEL), F32),
        grid_spec=pltpu.PrefetchScalarGridSpec(
            num_scalar_prefetch=1,
            grid=(t // ROW_TM,),
            in_specs=[row(D_MODEL), row(LANES), const, const, pl.BlockSpec(memory_space=pl.ANY)],
            out_specs=row(D_MODEL),
            scratch_shapes=[pltpu.VMEM((ROW_TM, D_MODEL), F32), pltpu.VMEM((ROW_TM, D_MODEL), F32),
                            pltpu.SemaphoreType.DMA(())]),
        compiler_params=_cparams("arbitrary"),
        name="moe_combine_ln",
    )(dest, xf, route, g, b, ys)


def _moe_ffn_ln(xf, w_router, b_router, w_in, w_out, layer, g, b):
    t = xf.shape[0]
    w_pad = jnp.zeros((D_MODEL, LANES), F32).at[:, :N_EXPERTS].set(w_router)
    b_pad = jnp.zeros((1, LANES), F32).at[0, :N_EXPERTS].set(b_router)
    route, counts = _router(xf, w_pad, b_pad)
    counts = counts[0, :N_EXPERTS].astype(jnp.int32)
    pcounts = (counts + MOE_TM - 1) // MOE_TM * MOE_TM
    pends = jnp.cumsum(pcounts)
    pstarts = pends - pcounts
    e = route[:, ROUTE_E1:ROUTE_E2 + 1].astype(jnp.int32)
    r = route[:, ROUTE_R1:ROUTE_R2 + 1].astype(jnp.int32)
    dest = (pstarts[e] + r).reshape(-1)
    n_blocks = 2 * t // MOE_TM + N_EXPERTS
    blk_start = jnp.arange(n_blocks, dtype=jnp.int32) * MOE_TM
    block_expert = jnp.minimum(jnp.sum(blk_start[:, None] >= pends[None, :], axis=1),
                               N_EXPERTS - 1).astype(jnp.int32)
    n_used = pends[-1] // MOE_TM
    block_src = jnp.minimum(jnp.arange(n_blocks, dtype=jnp.int32), n_used - 1)
    seg_ends = jnp.concatenate([jnp.zeros((1,), jnp.int32), pends.astype(jnp.int32)])
    xs = _dispatch(xf, dest, seg_ends, n_blocks * MOE_TM)
    ys = _expert_ffn(xs, w_in, w_out, layer, block_expert, block_src)
    return _combine_ln(xf, route, ys, dest, g, b)


def _dup_heads(w):
    d, n = w.shape
    w = w.reshape(d, n // HEAD_DIM, 1, HEAD_DIM)
    return jnp.broadcast_to(w, (d, n // HEAD_DIM, 2, HEAD_DIM)).reshape(d, 2 * n)


def kernel(x, mem, positions, w_in_swa, attn_sinks, w_in_fox, b_forget, w_mem_kv, w_out,
           ln_attn_g, ln_attn_b, w_ffn_in, w_ffn_out, w_router, b_router, w_exp_in,
           w_exp_out, ln_ffn_g, ln_ffn_b):
    b, s, d = x.shape
    t = b * s
    hkv = SWA_KV_HEADS * HEAD_DIM
    cos_t, sin_t = _rope_tables(positions)
    mem2d = mem.reshape(b * MEM_LEN, d)
    xf = x.reshape(t, d)
    w_out_b = w_out.astype(BF16)
    w_ffn_in_b, w_ffn_out_b = w_ffn_in.astype(BF16), w_ffn_out.astype(BF16)
    w_exp_in_b, w_exp_out_b = w_exp_in.astype(BF16), w_exp_out.astype(BF16)
    for i in range(DEPTH):
        j = i // 2
        kvm = _mem_kv(mem2d, w_mem_kv[i].astype(BF16)).reshape(b, MEM_LEN, 2 * HM)
        if i % 2 == 0:
            w = w_in_swa[j]
            w = jnp.concatenate([w[:, :HQ], _dup_heads(w[:, HQ:HQ + hkv]),
                                 _dup_heads(w[:, HQ + hkv:HQ + 2 * hkv]), w[:, HQ + 2 * hkv:]],
                                axis=1).astype(BF16)
            q, kk, vv, qc = _swa_proj(xf, w, cos_t, sin_t)
            self_out = _swa_attn(q.reshape(b, s, HQ), kk.reshape(b, s, HKV2),
                                 vv.reshape(b, s, HKV2), attn_sinks[j])
            qc = qc.reshape(b, s, HM)
        else:
            w = jnp.pad(w_in_fox[j], ((0, 0), (0, FOX_COLS - w_in_fox.shape[-1]))).astype(BF16)
            bf_row = jnp.pad(b_forget[j], (0, LANES - SELF_HEADS))[None, :]
            q, k, v, qc, dcum = _fox_proj(xf.reshape(b, s, d), w, bf_row)
            self_out = _fox_attn(q, k, v, dcum)
        cross_out = _mem_attn(qc, kvm)
        xf = _out_proj_ln(self_out.reshape(t, HQ), cross_out.reshape(t, HM), xf,
                          w_out_b, i, ln_attn_g[i][None, :], ln_attn_b[i][None, :])
        g, bb = ln_ffn_g[i][None, :], ln_ffn_b[i][None, :]
        if i % 2 == 0:
            xf = _dense_ffn_ln(xf, w_ffn_in_b, w_ffn_out_b, j, g, bb)
        else:
            xf = _moe_ffn_ln(xf, w_router[j], b_router[j], w_exp_in_b, w_exp_out_b, j, g, bb)
    return xf.reshape(b, s, d)
```

```python
import functools

import jax
import jax.numpy as jnp
from jax import lax
from jax.experimental import pallas as pl
from jax.experimental.pallas import tpu as pltpu

D_MODEL = 1024
DEPTH = 4
CHUNK = 64
HEAD_DIM = 64
MEM_HEADS = 4
SELF_HEADS = 12
SWA_KV_HEADS = 3
SWA_GROUP = SELF_HEADS // SWA_KV_HEADS
WINDOW_CHUNKS = 2
ROT_DIM = HEAD_DIM // 4
ROT_HALF = ROT_DIM // 2
ROPE_THETA = 500000.0
MEM_LEN = 256
D_FF = 2816
N_EXPERTS = 8
LN_EPS = 1e-5
NEG = -1e30
ATTN_SCALE = HEAD_DIM ** -0.5
DN_ALPHA = (2.0 * DEPTH) ** 0.25

HQ = SELF_HEADS * HEAD_DIM
HM = MEM_HEADS * HEAD_DIM
HKV2 = 2 * SWA_KV_HEADS * HEAD_DIM
SWA_COLS = HQ + 2 * HKV2 + HM
FOX_COLS = 3 * HQ + HM + 128

LANES = 128
VMEM_LIMIT_BYTES = 56 * 1024 * 1024

PROJ_TM = 512
FOX_PROJ_TM = 512
SWA_TQ = 256
SWA_SUB = 128
FOX_TQ = 2048
FOX_TK = 512
MEM_TQ = 512
FFN_TM = 256
MXU_DIM = 256
FF_CHUNKS = ((0, 6 * MXU_DIM), (6 * MXU_DIM, D_FF))
ROUTER_TM = 512
MOE_TM = 256
ROW_TM = 256

F32 = jnp.float32
BF16 = jnp.bfloat16


def _cparams(*sem):
    return pltpu.CompilerParams(dimension_semantics=sem, vmem_limit_bytes=VMEM_LIMIT_BYTES)


def _layer_norm(y, g, b):
    mu = jnp.mean(y, axis=-1, keepdims=True)
    d = y - mu
    var = jnp.mean(d * d, axis=-1, keepdims=True)
    return d * lax.rsqrt(var + LN_EPS) * g + b


def _split3(a):
    hi = a.astype(BF16)
    r1 = a - hi.astype(F32)
    mid = r1.astype(BF16)
    lo = (r1 - mid.astype(F32)).astype(BF16)
    return hi, mid, lo


def _dot(a, b):
    return jnp.dot(a, b, preferred_element_type=F32)


def _dot_nt(a, b):
    return lax.dot_general(a, b, (((1,), (1,)), ((), ())), preferred_element_type=F32)


def _rope_table_kernel(pos_ref, freq_ref, cos_ref, sin_ref):
    ang = pos_ref[...].astype(F32) * freq_ref[...]
    sub = lax.broadcasted_iota(jnp.int32, ang.shape, 1) & (HEAD_DIM - 1)
    c, s = jnp.cos(ang), jnp.sin(ang)
    cos_ref[...] = jnp.where(sub < ROT_DIM, c, 1.0)
    sin_ref[...] = jnp.where(sub < ROT_HALF, -s, jnp.where(sub < ROT_DIM, s, 0.0))


def _rope_tables(positions):
    t = positions.size
    tm = min(t, 2048)
    inv_freq = ROPE_THETA ** (-jnp.arange(0, ROT_DIM, 2, dtype=F32) / ROT_DIM)
    lane = jnp.arange(LANES) % HEAD_DIM % ROT_HALF
    freq = inv_freq[lane][None, :]
    out = jax.ShapeDtypeStruct((t, LANES), F32)
    return pl.pallas_call(
        _rope_table_kernel,
        out_shape=(out, out),
        grid=(t // tm,),
        in_specs=[pl.BlockSpec((tm, 1), lambda i: (i, 0)),
                  pl.BlockSpec((1, LANES), lambda i: (0, 0))],
        out_specs=(pl.BlockSpec((tm, LANES), lambda i: (i, 0)),
                   pl.BlockSpec((tm, LANES), lambda i: (i, 0))),
        compiler_params=_cparams("parallel"),
        name="rope_tables",
    )(positions.reshape(t, 1), freq)


def _mem_kv_kernel(m_ref, w_ref, o_ref):
    o_ref[...] = _dot(m_ref[...].astype(BF16), w_ref[...]).astype(BF16)


def _mem_kv(mem2d, w):
    n = mem2d.shape[0]
    tm = min(n, PROJ_TM)
    return pl.pallas_call(
        _mem_kv_kernel,
        out_shape=jax.ShapeDtypeStruct((n, 2 * HM), BF16),
        grid=(n // tm,),
        in_specs=[pl.BlockSpec((tm, D_MODEL), lambda i: (i, 0)),
                  pl.BlockSpec((D_MODEL, 2 * HM), lambda i: (0, 0))],
        out_specs=pl.BlockSpec((tm, 2 * HM), lambda i: (i, 0)),
        compiler_params=_cparams("parallel"),
        name="mem_kv_proj",
    )(mem2d, w)


def _swa_proj_kernel(x_ref, w_ref, cos_ref, sin_ref, q_ref, kk_ref, vv_ref, qc_ref):
    acc = _dot(x_ref[...].astype(BF16), w_ref[...])
    cos, sin = cos_ref[...], sin_ref[...]
    first = (lax.broadcasted_iota(jnp.int32, cos.shape, 1) & (HEAD_DIM - 1)) < ROT_HALF

    def rope(t):
        fwd = pltpu.roll(t, LANES - ROT_HALF, axis=1)
        bwd = pltpu.roll(t, ROT_HALF, axis=1)
        return t * cos + jnp.where(first, fwd, bwd) * sin

    for g in range(HQ // LANES):
        sl = slice(g * LANES, (g + 1) * LANES)
        q_ref[:, sl] = (rope(acc[:, sl]) * ATTN_SCALE).astype(BF16)
    for g in range(HKV2 // LANES):
        src = slice(HQ + g * LANES, HQ + (g + 1) * LANES)
        kk_ref[:, g * LANES:(g + 1) * LANES] = rope(acc[:, src]).astype(BF16)
    vv_ref[...] = acc[:, HQ + HKV2:HQ + 2 * HKV2].astype(BF16)
    qc_ref[...] = (acc[:, HQ + 2 * HKV2:] * ATTN_SCALE).astype(BF16)


def _swa_proj(xf, w, cos_t, sin_t):
    t = xf.shape[0]
    tm = min(t, PROJ_TM)
    row = lambda n: pl.BlockSpec((tm, n), lambda i: (i, 0))
    return pl.pallas_call(
        _swa_proj_kernel,
        out_shape=(jax.ShapeDtypeStruct((t, HQ), BF16), jax.ShapeDtypeStruct((t, HKV2), BF16),
                   jax.ShapeDtypeStruct((t, HKV2), BF16), jax.ShapeDtypeStruct((t, HM), BF16)),
        grid=(t // tm,),
        in_specs=[row(D_MODEL), pl.BlockSpec((D_MODEL, SWA_COLS), lambda i: (0, 0)),
                  row(LANES), row(LANES)],
        out_specs=(row(HQ), row(HKV2), row(HKV2), row(HM)),
        compiler_params=_cparams("parallel"),
        name="swa_proj",
    )(xf, w, cos_t, sin_t)


def _swa_attn_kernel(sink_ref, q_ref, kp_ref, kc_ref, vp_ref, vc_ref, o_ref):
    first_tile = pl.program_id(1) == 0
    half = SWA_TQ - SWA_SUB
    lane = lax.broadcasted_iota(jnp.int32, (SWA_SUB, LANES), 1)
    shift = CHUNK.bit_length() - 1
    rc = lax.broadcasted_iota(jnp.int32, (SWA_SUB, 2 * SWA_SUB), 0) >> shift
    cc = lax.broadcasted_iota(jnp.int32, (SWA_SUB, 2 * SWA_SUB), 1) >> shift
    band = (cc >= rc) & (cc <= rc + WINDOW_CHUNKS)
    band_prev = band & ((cc >= SWA_SUB // CHUNK) | jnp.logical_not(first_tile))
    for sub in range(SWA_TQ // SWA_SUB):
        rows = slice(sub * SWA_SUB, (sub + 1) * SWA_SUB)
        mask = band_prev if sub == 0 else band
        for h in range(SWA_KV_HEADS):
            cols = slice(h * LANES, (h + 1) * LANES)
            if sub == 0:
                kcat = jnp.concatenate([kp_ref[0, half:, cols], kc_ref[0, :SWA_SUB, cols]], axis=0)
                vcat = jnp.concatenate([vp_ref[0, half:, cols], vc_ref[0, :SWA_SUB, cols]], axis=0)
            else:
                kcat = kc_ref[0, :, cols]
                vcat = vc_ref[0, :, cols]
            for pair in range(SWA_GROUP // 2):
                g = h * (SWA_GROUP // 2) + pair
                qcols = slice(g * LANES, (g + 1) * LANES)
                qf = q_ref[0, rows, qcols].astype(F32)
                outs = []
                for a in range(2):
                    keep = (lane < HEAD_DIM) if a == 0 else (lane >= HEAD_DIM)
                    qa = jnp.where(keep, qf, 0.0).astype(BF16)
                    s = jnp.where(mask, _dot_nt(qa, kcat), NEG)
                    sink = sink_ref[2 * g + a]
                    m = jnp.maximum(jnp.max(s, axis=-1, keepdims=True), sink)
                    p = jnp.exp(s - m)
                    den = jnp.sum(p, axis=-1, keepdims=True) + jnp.exp(sink - m)
                    probs = (p * (1.0 / den)).astype(BF16)
                    outs.append(_dot(probs, vcat))
                o = jnp.where(lane < HEAD_DIM, outs[0], outs[1])
                o_ref[0, rows, qcols] = o.astype(BF16)


def _swa_attn(q, kk, vv, sinks):
    b, s, _ = q.shape
    nt = s // SWA_TQ
    cur = lambda n: pl.BlockSpec((1, SWA_TQ, n), lambda bi, i: (bi, i, 0))
    prev = lambda n: pl.BlockSpec((1, SWA_TQ, n), lambda bi, i: (bi, jnp.maximum(i - 1, 0), 0))
    return pl.pallas_call(
        _swa_attn_kernel,
        out_shape=jax.ShapeDtypeStruct((b, s, HQ), BF16),
        grid=(b, nt),
        in_specs=[pl.BlockSpec(memory_space=pltpu.SMEM),
                  cur(HQ), prev(HKV2), cur(HKV2), prev(HKV2), cur(HKV2)],
        out_specs=cur(HQ),
        compiler_params=_cparams("parallel", "parallel"),
        name="swa_attn",
    )(sinks, q, kk, kk, vv, vv)


def _fox_proj_kernel(x_ref, w_ref, bf_ref, q_ref, k_ref, v_ref, qc_ref, d_ref, carry_ref):
    @pl.when(pl.program_id(1) == 0)
    def _():
        carry_ref[...] = jnp.zeros_like(carry_ref)

    acc = _dot(x_ref[0].astype(BF16), w_ref[...])
    q_ref[0] = (acc[:, :HQ] * ATTN_SCALE).astype(BF16)
    k_ref[0] = acc[:, HQ:2 * HQ].astype(BF16)
    v_ref[0] = acc[:, 2 * HQ:3 * HQ].astype(BF16)
    qc_ref[0] = (acc[:, 3 * HQ:3 * HQ + HM] * ATTN_SCALE).astype(BF16)
    z = acc[:, 3 * HQ + HM:] + bf_ref[...]
    log_f = jnp.minimum(z, 0.0) - jnp.log(1.0 + jnp.exp(-jnp.abs(z)))
    tm = z.shape[0]
    tri = (lax.broadcasted_iota(jnp.int32, (tm, tm), 0)
           >= lax.broadcasted_iota(jnp.int32, (tm, tm), 1)).astype(F32).astype(BF16)
    hi, mid, lo = _split3(log_f)
    cum = (_dot(tri, lo) + _dot(tri, mid)) + _dot(tri, hi) + carry_ref[...]
    d_ref[0] = cum
    carry_ref[...] = cum[tm - 1:tm, :]


def _fox_proj(x3, w, bf_row):
    b, s, _ = x3.shape
    tm = min(s, FOX_PROJ_TM)
    row = lambda n: pl.BlockSpec((1, tm, n), lambda bi, i: (bi, i, 0))
    sds = lambda n, dt: jax.ShapeDtypeStruct((b, s, n), dt)
    return pl.pallas_call(
        _fox_proj_kernel,
        out_shape=(sds(HQ, BF16), sds(HQ, BF16), sds(HQ, BF16), sds(HM, BF16), sds(LANES, F32)),
        grid=(b, s // tm),
        in_specs=[row(D_MODEL), pl.BlockSpec((D_MODEL, FOX_COLS), lambda bi, i: (0, 0)),
                  pl.BlockSpec((1, LANES), lambda bi, i: (0, 0))],
        out_specs=(row(HQ), row(HQ), row(HQ), row(HM), row(LANES)),
        scratch_shapes=[pltpu.VMEM((1, LANES), F32)],
        compiler_params=_cparams("arbitrary", "arbitrary"),
        name="fox_proj",
    )(x3, w, bf_row)


def _fox_attn_kernel(q_ref, k_ref, v_ref, d_ref, o_ref, kaug_ref, qaug_ref, vt_ref, qt_ref):
    s_len = q_ref.shape[1]
    tq, tk = qt_ref.shape[3], vt_ref.shape[2]
    pair = pl.program_id(1)
    lane = lax.broadcasted_iota(jnp.int32, (s_len, LANES), 1)
    prow = lax.broadcasted_iota(jnp.int32, (3 * LANES, 2 * LANES), 0)
    pcol = lax.broadcasted_iota(jnp.int32, (3 * LANES, 2 * LANES), 1)
    term = prow >> 7
    kf = k_ref[0].astype(F32)
    dsplit = jnp.concatenate(_split3(d_ref[0]), axis=1)
    for a in range(2):
        head = 2 * pair + a
        from_head = (prow & (LANES - 1)) == head
        place = (jnp.where(from_head & (pcol == term), 1.0, 0.0)
                 + jnp.where(from_head & (pcol == LANES + 3 + term), -1.0, 0.0)).astype(BF16)
        extra = _dot(dsplit, place)
        q_extra = extra[:, :LANES] + jnp.where((lane >= 3) & (lane < 6), 1.0, 0.0)
        k_extra = extra[:, LANES:] + jnp.where(lane < 3, 1.0, 0.0)
        keep = (lane < HEAD_DIM) if a == 0 else (lane >= HEAD_DIM)
        kaug_ref[a, :, :LANES] = jnp.where(keep, kf, 0.0).astype(BF16)
        kaug_ref[a, :, LANES:] = k_extra.astype(BF16)
        qaug_ref[a, :, :LANES] = q_ref[0]
        qaug_ref[a, :, LANES:] = q_extra.astype(BF16)
    for j in range(s_len // tk):
        vt_ref[j] = v_ref[0, j * tk:(j + 1) * tk, :].astype(F32).T.astype(BF16)

    nq = s_len // tq
    for qi in range(nq):
        for a in range(2):
            qt_ref[a, qi] = qaug_ref[a, qi * tq:(qi + 1) * tq, :].astype(F32).T.astype(BF16)

    def tile(kb, qi, a, state):
        q_off = max(0, kb * tk - qi * tq)
        width = tq - q_off
        masked = (kb + 1) * tk - 1 > qi * tq + q_off
        m0, l0, acc0 = state
        m, l, acc = m0[:, q_off:], l0[:, q_off:], acc0[:, q_off:]
        s = _dot(kaug_ref[a, kb * tk:(kb + 1) * tk, :], qt_ref[a, qi, :, q_off:])
        if masked:
            n_mask = min(width, pl.cdiv((kb + 1) * tk - (qi * tq + q_off), LANES) * LANES)
            k_pos = kb * tk + lax.broadcasted_iota(jnp.int32, (tk, n_mask), 0)
            q_pos = qi * tq + q_off + lax.broadcasted_iota(jnp.int32, (tk, n_mask), 1)
            head_cols = jnp.where(k_pos <= q_pos, s[:, :n_mask], NEG)
            s = head_cols if n_mask == width else jnp.concatenate([head_cols, s[:, n_mask:]], axis=1)
        m_new = jnp.maximum(m, jnp.max(s, axis=0, keepdims=True))
        alpha = jnp.exp(m - m_new)
        p = jnp.exp(s - m_new)
        l = alpha * l + jnp.sum(p, axis=0, keepdims=True)
        acc = alpha * acc + _dot(vt_ref[kb], p.astype(BF16))
        if q_off:
            m_new = jnp.concatenate([m0[:, :q_off], m_new], axis=1)
            l = jnp.concatenate([l0[:, :q_off], l], axis=1)
            acc = jnp.concatenate([acc0[:, :q_off], acc], axis=1)
        return m_new, l, acc

    state = {(qi, a): (jnp.full((1, tq), NEG, F32), jnp.zeros((1, tq), F32),
                       jnp.zeros((LANES, tq), F32)) for qi in range(nq) for a in range(2)}
    for kb in range(s_len // tk):
        for qi in range(kb * tk // tq, nq):
            for a in range(2):
                state[qi, a] = tile(kb, qi, a, state[qi, a])
    orow = lax.broadcasted_iota(jnp.int32, (LANES, tq), 0)
    for qi in range(nq):
        (_, l0, acc0), (_, l1, acc1) = state[qi, 0], state[qi, 1]
        o_t = jnp.where(orow < HEAD_DIM, acc0 * (1.0 / l0), acc1 * (1.0 / l1))
        o_ref[0, qi * tq:(qi + 1) * tq, :] = o_t.T.astype(BF16)


def _fox_attn(q, k, v, dcum):
    b, s, _ = q.shape
    tq, tk = min(FOX_TQ, s), min(FOX_TK, s)
    seq = lambda: pl.BlockSpec((1, s, LANES), lambda bi, j: (bi, 0, j))
    return pl.pallas_call(
        _fox_attn_kernel,
        out_shape=jax.ShapeDtypeStruct((b, s, HQ), BF16),
        grid=(b, HQ // LANES),
        in_specs=[seq(), seq(), seq(), pl.BlockSpec((1, s, LANES), lambda bi, j: (bi, 0, 0))],
        out_specs=seq(),
        scratch_shapes=[pltpu.VMEM((2, s, 2 * LANES), BF16), pltpu.VMEM((2, s, 2 * LANES), BF16),
                        pltpu.VMEM((s // tk, LANES, tk), BF16),
                        pltpu.VMEM((2, s // tq, 2 * LANES, tq), BF16)],
        compiler_params=_cparams("parallel", "parallel"),
        name="fox_attn",
    )(q, k, v, dcum)


def _mem_attn_kernel(q_ref, kv_ref, o_ref):
    lane = lax.broadcasted_iota(jnp.int32, (MEM_TQ, LANES), 1)
    for pair in range(HM // LANES):
        cols = slice(pair * LANES, (pair + 1) * LANES)
        kblk = kv_ref[0, :, cols]
        vblk = kv_ref[0, :, HM + pair * LANES:HM + (pair + 1) * LANES]
        qf = q_ref[0, :, cols].astype(F32)
        outs = []
        for a in range(2):
            keep = (lane < HEAD_DIM) if a == 0 else (lane >= HEAD_DIM)
            s = _dot_nt(jnp.where(keep, qf, 0.0).astype(BF16), kblk)
            p = jnp.exp(s - jnp.max(s, axis=-1, keepdims=True))
            l = jnp.sum(p, axis=-1, keepdims=True)
            outs.append(_dot(p.astype(BF16), vblk) * (1.0 / l))
        o_ref[0, :, cols] = jnp.where(lane < HEAD_DIM, outs[0], outs[1]).astype(BF16)


def _mem_attn(qc, kvm):
    b, s, _ = qc.shape
    return pl.pallas_call(
        _mem_attn_kernel,
        out_shape=jax.ShapeDtypeStruct((b, s, HM), BF16),
        grid=(b, s // MEM_TQ),
        in_specs=[pl.BlockSpec((1, MEM_TQ, HM), lambda bi, i: (bi, i, 0)),
                  pl.BlockSpec((1, MEM_LEN, 2 * HM), lambda bi, i: (bi, 0, 0))],
        out_specs=pl.BlockSpec((1, MEM_TQ, HM), lambda bi, i: (bi, i, 0)),
        compiler_params=_cparams("parallel", "parallel"),
        name="mem_attn",
    )(qc, kvm)


def _out_proj_kernel(s_ref, c_ref, x_ref, w_ref, g_ref, b_ref, o_ref):
    mix = _dot(s_ref[...], w_ref[0, :HQ, :]) + _dot(c_ref[...], w_ref[0, HQ:, :])
    o_ref[...] = _layer_norm(DN_ALPHA * x_ref[...] + mix, g_ref[...], b_ref[...])


def _out_proj_ln(self_out, cross_out, xf, w_all, layer, g, b):
    t = xf.shape[0]
    tm = min(t, PROJ_TM)
    row = lambda n: pl.BlockSpec((tm, n), lambda i: (i, 0))
    const = lambda r, c: pl.BlockSpec((r, c), lambda i: (0, 0))
    return pl.pallas_call(
        _out_proj_kernel,
        out_shape=jax.ShapeDtypeStruct((t, D_MODEL), F32),
        grid=(t // tm,),
        in_specs=[row(HQ), row(HM), row(D_MODEL),
                  pl.BlockSpec((1, D_MODEL, D_MODEL), lambda i: (layer, 0, 0)),
                  const(1, D_MODEL), const(1, D_MODEL)],
        out_specs=row(D_MODEL),
        compiler_params=_cparams("parallel"),
        name="out_proj_ln",
    )(self_out, cross_out, xf, w_all, g, b)


def _swiglu(xb, w_in_ref, w_out_ref):
    acc = None
    for lo, hi in FF_CHUNKS:
        ha = _dot(xb, w_in_ref[:, lo:hi])
        hb = _dot(xb, w_in_ref[:, D_FF + lo:D_FF + hi])
        act = (ha * (1.0 / (1.0 + jnp.exp(-ha))) * hb).astype(BF16)
        part = _dot(act, w_out_ref[lo:hi, :])
        acc = part if acc is None else acc + part
    return acc


def _dense_ffn_kernel(x_ref, w_in_ref, w_out_ref, g_ref, b_ref, ein_ref, eout_ref,
                      o_ref, ein_bf_ref, eout_bf_ref):
    x = x_ref[...]
    ffn = _swiglu(x.astype(BF16), w_in_ref.at[0], w_out_ref.at[0])
    o_ref[...] = _layer_norm(DN_ALPHA * x + ffn, g_ref[...], b_ref[...])
    ein_bf_ref[...] = ein_ref[...].astype(BF16)
    eout_bf_ref[...] = eout_ref[...].astype(BF16)


def _dense_ffn_ln(xf, w_in, w_out, layer, g, b, w_exp_in, w_exp_out):
    t = xf.shape[0]
    tm = min(t, FFN_TM)
    steps = t // tm
    n_layers = w_exp_in.shape[0]
    ein = w_exp_in.reshape(n_layers * N_EXPERTS * D_MODEL, 2 * D_FF)
    eout = w_exp_out.reshape(n_layers * N_EXPERTS * D_FF, D_MODEL)
    rin, rout = N_EXPERTS * D_MODEL // steps, N_EXPERTS * D_FF // steps
    assert rin * steps == N_EXPERTS * D_MODEL and rout * steps == N_EXPERTS * D_FF
    assert rin % 16 == 0 and rout % 16 == 0
    row = pl.BlockSpec((tm, D_MODEL), lambda i: (i, 0))
    const = lambda r, c: pl.BlockSpec((r, c), lambda i: (0, 0))
    once = pl.Buffered(1)
    y, ein_bf, eout_bf = pl.pallas_call(
        _dense_ffn_kernel,
        out_shape=(jax.ShapeDtypeStruct((t, D_MODEL), F32),
                   jax.ShapeDtypeStruct((N_EXPERTS * D_MODEL, 2 * D_FF), BF16),
                   jax.ShapeDtypeStruct((N_EXPERTS * D_FF, D_MODEL), BF16)),
        grid=(steps,),
        in_specs=[row,
                  pl.BlockSpec((1, D_MODEL, 2 * D_FF), lambda i: (layer, 0, 0), pipeline_mode=once),
                  pl.BlockSpec((1, D_FF, D_MODEL), lambda i: (layer, 0, 0), pipeline_mode=once),
                  const(1, D_MODEL), const(1, D_MODEL),
                  pl.BlockSpec((rin, 2 * D_FF), lambda i: (layer * steps + i, 0)),
                  pl.BlockSpec((rout, D_MODEL), lambda i: (layer * steps + i, 0))],
        out_specs=(row, pl.BlockSpec((rin, 2 * D_FF), lambda i: (i, 0)),
                   pl.BlockSpec((rout, D_MODEL), lambda i: (i, 0))),
        compiler_params=_cparams("parallel"),
        name="dense_ffn_ln",
    )(xf, w_in, w_out, g, b, ein, eout)
    return (y, ein_bf.reshape(N_EXPERTS, D_MODEL, 2 * D_FF),
            eout_bf.reshape(N_EXPERTS, D_FF, D_MODEL))


def _expert_kernel(be_ref, bx_ref, xs_ref, w_in_ref, w_out_ref, ys_ref):
    del be_ref
    i = pl.program_id(0)
    used = bx_ref[i] == i

    @pl.when(used)
    def _():
        ys_ref[...] = _swiglu(xs_ref[...].astype(BF16), w_in_ref.at[0], w_out_ref.at[0])

    @pl.when(jnp.logical_not(used))
    def _():
        ys_ref[...] = jnp.zeros_like(ys_ref)


def _expert_ffn(xs, w_in, w_out, block_expert, block_src):
    n_slots = xs.shape[0]
    return pl.pallas_call(
        _expert_kernel,
        out_shape=jax.ShapeDtypeStruct((n_slots, D_MODEL), F32),
        grid_spec=pltpu.PrefetchScalarGridSpec(
            num_scalar_prefetch=2,
            grid=(n_slots // MOE_TM,),
            in_specs=[pl.BlockSpec((MOE_TM, D_MODEL), lambda i, be, bx: (bx[i], 0)),
                      pl.BlockSpec((1, D_MODEL, 2 * D_FF), lambda i, be, bx: (be[i], 0, 0)),
                      pl.BlockSpec((1, D_FF, D_MODEL), lambda i, be, bx: (be[i], 0, 0))],
            out_specs=pl.BlockSpec((MOE_TM, D_MODEL), lambda i, be, bx: (i, 0))),
        compiler_params=_cparams("arbitrary"),
        name="expert_ffn",
    )(block_expert, block_src, xs, w_in, w_out)


ROUTE_E1, ROUTE_E2, ROUTE_R1, ROUTE_R2, ROUTE_G1, ROUTE_G2 = range(6)


def _router_kernel(x_ref, w_ref, b_ref, route_ref, count_ref, carry_ref):
    @pl.when(pl.program_id(0) == 0)
    def _():
        carry_ref[...] = jnp.zeros_like(carry_ref)

    xs, ws = _split3(x_ref[...]), _split3(w_ref[...])
    logits = (_dot(xs[1], ws[0]) + _dot(xs[0], ws[1])) + _dot(xs[0], ws[0])
    tm = logits.shape[0]
    lane = lax.broadcasted_iota(jnp.int32, (tm, LANES), 1).astype(F32)
    logits = jnp.where(lane < N_EXPERTS, logits + b_ref[...], -jnp.inf)
    m1 = jnp.max(logits, axis=-1, keepdims=True)
    e1 = jnp.min(jnp.where(logits == m1, lane, float(LANES)), axis=-1, keepdims=True)
    rest = jnp.where(lane == e1, -jnp.inf, logits)
    m2 = jnp.max(rest, axis=-1, keepdims=True)
    e2 = jnp.min(jnp.where(rest == m2, lane, float(LANES)), axis=-1, keepdims=True)
    ex = jnp.exp(m2 - m1)
    g1 = 1.0 / (1.0 + ex)
    g2 = ex * g1
    sel1, sel2 = lane == e1, lane == e2
    onehot = jnp.where(sel1 | sel2, 1.0, 0.0)
    tri = (lax.broadcasted_iota(jnp.int32, (tm, tm), 0)
           > lax.broadcasted_iota(jnp.int32, (tm, tm), 1)).astype(F32).astype(BF16)
    rank = _dot(tri, onehot.astype(BF16)) + carry_ref[...]
    r1 = jnp.sum(jnp.where(sel1, rank, 0.0), axis=-1, keepdims=True)
    r2 = jnp.sum(jnp.where(sel2, rank, 0.0), axis=-1, keepdims=True)
    total = carry_ref[...] + jnp.sum(onehot, axis=0, keepdims=True)
    carry_ref[...] = total
    count_ref[...] = total
    route = jnp.zeros((tm, LANES), F32)
    for ln, val in ((ROUTE_E1, e1), (ROUTE_E2, e2), (ROUTE_R1, r1), (ROUTE_R2, r2),
                    (ROUTE_G1, g1), (ROUTE_G2, g2)):
        route = jnp.where(lane == float(ln), val, route)
    route_ref[...] = route


def _router(xf, w_pad, b_pad):
    t = xf.shape[0]
    tm = min(t, ROUTER_TM)
    return pl.pallas_call(
        _router_kernel,
        out_shape=(jax.ShapeDtypeStruct((t, LANES), F32), jax.ShapeDtypeStruct((1, LANES), F32)),
        grid=(t // tm,),
        in_specs=[pl.BlockSpec((tm, D_MODEL), lambda i: (i, 0)),
                  pl.BlockSpec((D_MODEL, LANES), lambda i: (0, 0)),
                  pl.BlockSpec((1, LANES), lambda i: (0, 0))],
        out_specs=(pl.BlockSpec((tm, LANES), lambda i: (i, 0)),
                   pl.BlockSpec((1, LANES), lambda i: (0, 0))),
        scratch_shapes=[pltpu.VMEM((1, LANES), F32)],
        compiler_params=_cparams("arbitrary"),
        name="router",
    )(xf, w_pad, b_pad)


def _row_copies(n, make_copy):
    for r in range(n):
        make_copy(r, 0).start(priority=0)
        make_copy(r, 1).start(priority=1)
    for r in range(n):
        make_copy(r, 0).wait()
        make_copy(r, 1).wait()


def _dispatch_kernel(dest_ref, pend_ref, x_ref, xs_hbm, zero_ref, sem, zsem):
    base = pl.program_id(0) * ROW_TM

    @pl.when(pl.program_id(0) == 0)
    def _():
        zero_ref[...] = jnp.zeros_like(zero_ref)

        def zero_copy(e):
            start = pl.multiple_of(pend_ref[e + 1] - MOE_TM, MOE_TM)
            return pltpu.make_async_copy(zero_ref, xs_hbm.at[pl.ds(start, MOE_TM)], zsem)

        def tail_copy(blk):
            return pltpu.make_async_copy(
                zero_ref, xs_hbm.at[pl.ds(pl.multiple_of(blk * MOE_TM, MOE_TM), MOE_TM)], zsem)

        first_tail = pend_ref[N_EXPERTS] // MOE_TM
        n_blocks = xs_hbm.shape[0] // MOE_TM
        for e in range(N_EXPERTS):
            @pl.when(pend_ref[e + 1] > pend_ref[e])
            def _():
                zero_copy(e).start()
        lax.fori_loop(first_tail, n_blocks, lambda blk, c: (tail_copy(blk).start(), c)[1], 0)
        for e in range(N_EXPERTS):
            @pl.when(pend_ref[e + 1] > pend_ref[e])
            def _():
                zero_copy(e).wait()
        lax.fori_loop(first_tail, n_blocks, lambda blk, c: (tail_copy(blk).wait(), c)[1], 0)

    def make_copy(r, k):
        return pltpu.make_async_copy(x_ref.at[pl.ds(r, 1)],
                                     xs_hbm.at[pl.ds(dest_ref[2 * (base + r) + k], 1)], sem)

    _row_copies(ROW_TM, make_copy)


def _dispatch(xf, dest, seg_ends, n_slots):
    t = xf.shape[0]
    return pl.pallas_call(
        _dispatch_kernel,
        out_shape=jax.ShapeDtypeStruct((n_slots, D_MODEL), F32),
        grid_spec=pltpu.PrefetchScalarGridSpec(
            num_scalar_prefetch=2,
            grid=(t // ROW_TM,),
            in_specs=[pl.BlockSpec((ROW_TM, D_MODEL), lambda i, d, p: (i, 0))],
            out_specs=pl.BlockSpec(memory_space=pl.ANY),
            scratch_shapes=[pltpu.VMEM((MOE_TM, D_MODEL), F32), pltpu.SemaphoreType.DMA(()),
                            pltpu.SemaphoreType.DMA(())]),
        compiler_params=_cparams("arbitrary"),
        name="moe_dispatch",
    )(dest, seg_ends, xf)


def _combine_kernel(dest_ref, x_ref, route_ref, g_ref, b_ref, ys_hbm, o_ref, ya, yb, sem):
    base = pl.program_id(0) * ROW_TM

    def make_copy(r, k):
        dst = ya if k == 0 else yb
        return pltpu.make_async_copy(ys_hbm.at[pl.ds(dest_ref[2 * (base + r) + k], 1)],
                                     dst.at[pl.ds(r, 1)], sem)

    _row_copies(ROW_TM, make_copy)
    route = route_ref[...]
    g1 = route[:, ROUTE_G1:ROUTE_G1 + 1]
    g2 = route[:, ROUTE_G2:ROUTE_G2 + 1]
    y = ya[...] * g1 + yb[...] * g2
    o_ref[...] = _layer_norm(DN_ALPHA * x_ref[...] + y, g_ref[...], b_ref[...])


def _combine_ln(xf, route, ys, dest, g, b):
    t = xf.shape[0]
    row = lambda n: pl.BlockSpec((ROW_TM, n), lambda i, d: (i, 0))
    const = pl.BlockSpec((1, D_MODEL), lambda i, d: (0, 0))
    return pl.pallas_call(
        _combine_kernel,
        out_shape=jax.ShapeDtypeStruct((t, D_MODEL), F32),
        grid_spec=pltpu.PrefetchScalarGridSpec(
            num_scalar_prefetch=1,
            grid=(t // ROW_TM,),
            in_specs=[row(D_MODEL), row(LANES), const, const, pl.BlockSpec(memory_space=pl.ANY)],
            out_specs=row(D_MODEL),
            scratch_shapes=[pltpu.VMEM((ROW_TM, D_MODEL), F32), pltpu.VMEM((ROW_TM, D_MODEL), F32),
                            pltpu.SemaphoreType.DMA(())]),
        compiler_params=_cparams("arbitrary"),
        name="moe_combine_ln",
    )(dest, xf, route, g, b, ys)


def _moe_ffn_ln(xf, w_router, b_router, w_in, w_out, g, b):
    t = xf.shape[0]
    w_pad = jnp.zeros((D_MODEL, LANES), F32).at[:, :N_EXPERTS].set(w_router)
    b_pad = jnp.zeros((1, LANES), F32).at[0, :N_EXPERTS].set(b_router)
    route, counts = _router(xf, w_pad, b_pad)
    counts = counts[0, :N_EXPERTS].astype(jnp.int32)
    pcounts = (counts + MOE_TM - 1) // MOE_TM * MOE_TM
    pends = jnp.cumsum(pcounts)
    pstarts = pends - pcounts
    e = route[:, ROUTE_E1:ROUTE_E2 + 1].astype(jnp.int32)
    r = route[:, ROUTE_R1:ROUTE_R2 + 1].astype(jnp.int32)
    dest = (pstarts[e] + r).reshape(-1)
    n_blocks = 2 * t // MOE_TM + N_EXPERTS
    blk_start = jnp.arange(n_blocks, dtype=jnp.int32) * MOE_TM
    block_expert = jnp.minimum(jnp.sum(blk_start[:, None] >= pends[None, :], axis=1),
                               N_EXPERTS - 1).astype(jnp.int32)
    n_used = pends[-1] // MOE_TM
    block_src = jnp.minimum(jnp.arange(n_blocks, dtype=jnp.int32), n_used - 1)
    seg_ends = jnp.concatenate([jnp.zeros((1,), jnp.int32), pends.astype(jnp.int32)])
    xs = _dispatch(xf, dest, seg_ends, n_blocks * MOE_TM)
    ys = _expert_ffn(xs, w_in, w_out, block_expert, block_src)
    return _combine_ln(xf, route, ys, dest, g, b)


def _dup_heads(w):
    d, n = w.shape
    w = w.reshape(d, n // HEAD_DIM, 1, HEAD_DIM)
    return jnp.broadcast_to(w, (d, n // HEAD_DIM, 2, HEAD_DIM)).reshape(d, 2 * n)


def kernel(x, mem, positions, w_in_swa, attn_sinks, w_in_fox, b_forget, w_mem_kv, w_out,
           ln_attn_g, ln_attn_b, w_ffn_in, w_ffn_out, w_router, b_router, w_exp_in,
           w_exp_out, ln_ffn_g, ln_ffn_b):
    b, s, d = x.shape
    t = b * s
    hkv = SWA_KV_HEADS * HEAD_DIM
    cos_t, sin_t = _rope_tables(positions)
    mem2d = mem.reshape(b * MEM_LEN, d)
    xf = x.reshape(t, d)
    w_out_b = w_out.astype(BF16)
    w_ffn_in_b, w_ffn_out_b = w_ffn_in.astype(BF16), w_ffn_out.astype(BF16)
    exp_w = None
    for i in range(DEPTH):
        j = i // 2
        kvm = _mem_kv(mem2d, w_mem_kv[i].astype(BF16)).reshape(b, MEM_LEN, 2 * HM)
        if i % 2 == 0:
            w = w_in_swa[j]
            w = jnp.concatenate([w[:, :HQ], _dup_heads(w[:, HQ:HQ + hkv]),
                                 _dup_heads(w[:, HQ + hkv:HQ + 2 * hkv]), w[:, HQ + 2 * hkv:]],
                                axis=1).astype(BF16)
            q, kk, vv, qc = _swa_proj(xf, w, cos_t, sin_t)
            self_out = _swa_attn(q.reshape(b, s, HQ), kk.reshape(b, s, HKV2),
                                 vv.reshape(b, s, HKV2), attn_sinks[j])
            qc = qc.reshape(b, s, HM)
        else:
            w = jnp.pad(w_in_fox[j], ((0, 0), (0, FOX_COLS - w_in_fox.shape[-1]))).astype(BF16)
            bf_row = jnp.pad(b_forget[j], (0, LANES - SELF_HEADS))[None, :]
            q, k, v, qc, dcum = _fox_proj(xf.reshape(b, s, d), w, bf_row)
            self_out = _fox_attn(q, k, v, dcum)
        cross_out = _mem_attn(qc, kvm)
        xf = _out_proj_ln(self_out.reshape(t, HQ), cross_out.reshape(t, HM), xf,
                          w_out_b, i, ln_attn_g[i][None, :], ln_attn_b[i][None, :])
        g, bb = ln_ffn_g[i][None, :], ln_ffn_b[i][None, :]
        if i % 2 == 0:
            xf, *exp_w = _dense_ffn_ln(xf, w_ffn_in_b, w_ffn_out_b, j, g, bb, w_exp_in, w_exp_out)
        else:
            xf = _moe_ffn_ln(xf, w_router[j], b_router[j], exp_w[0], exp_w[1], g, bb)
    return xf.reshape(b, s, d)
```

```python
import functools

import jax
import jax.numpy as jnp
from jax import lax
from jax.experimental import pallas as pl
from jax.experimental.pallas import tpu as pltpu

D_MODEL = 1024
DEPTH = 4
CHUNK = 64
HEAD_DIM = 64
MEM_HEADS = 4
SELF_HEADS = 12
SWA_KV_HEADS = 3
SWA_GROUP = SELF_HEADS // SWA_KV_HEADS
WINDOW_CHUNKS = 2
ROT_DIM = HEAD_DIM // 4
ROT_HALF = ROT_DIM // 2
ROPE_THETA = 500000.0
MEM_LEN = 256
D_FF = 2816
N_EXPERTS = 8
LN_EPS = 1e-5
NEG = -1e30
ATTN_SCALE = HEAD_DIM ** -0.5
DN_ALPHA = (2.0 * DEPTH) ** 0.25

HQ = SELF_HEADS * HEAD_DIM
HM = MEM_HEADS * HEAD_DIM
HKV2 = 2 * SWA_KV_HEADS * HEAD_DIM
SWA_COLS = HQ + 2 * HKV2 + HM
FOX_COLS = 3 * HQ + HM + 128

LANES = 128
VMEM_LIMIT_BYTES = 56 * 1024 * 1024

PROJ_TM = 512
FOX_PROJ_TM = 512
SWA_TQ = 256
SWA_SUB = 128
FOX_TQ = 2048
FOX_TK = 512
MEM_TQ = 512
FFN_TM = 256
MXU_DIM = 256
FF_CHUNKS = ((0, 6 * MXU_DIM), (6 * MXU_DIM, D_FF))
ROUTER_TM = 512
MOE_TM = 256
ROW_TM = 256

F32 = jnp.float32
BF16 = jnp.bfloat16


def _cparams(*sem):
    return pltpu.CompilerParams(dimension_semantics=sem, vmem_limit_bytes=VMEM_LIMIT_BYTES)


def _layer_norm(y, g, b):
    mu = jnp.mean(y, axis=-1, keepdims=True)
    d = y - mu
    var = jnp.mean(d * d, axis=-1, keepdims=True)
    return d * lax.rsqrt(var + LN_EPS) * g + b


def _split3(a):
    hi = a.astype(BF16)
    r1 = a - hi.astype(F32)
    mid = r1.astype(BF16)
    lo = (r1 - mid.astype(F32)).astype(BF16)
    return hi, mid, lo


def _dot(a, b):
    return jnp.dot(a, b, preferred_element_type=F32)


def _dot_nt(a, b):
    return lax.dot_general(a, b, (((1,), (1,)), ((), ())), preferred_element_type=F32)


def _rope_table_kernel(pos_ref, freq_ref, cos_ref, sin_ref):
    ang = pos_ref[...].astype(F32) * freq_ref[...]
    sub = lax.broadcasted_iota(jnp.int32, ang.shape, 1) & (HEAD_DIM - 1)
    c, s = jnp.cos(ang), jnp.sin(ang)
    cos_ref[...] = jnp.where(sub < ROT_DIM, c, 1.0)
    sin_ref[...] = jnp.where(sub < ROT_HALF, -s, jnp.where(sub < ROT_DIM, s, 0.0))


def _rope_tables(positions):
    t = positions.size
    tm = min(t, 2048)
    inv_freq = ROPE_THETA ** (-jnp.arange(0, ROT_DIM, 2, dtype=F32) / ROT_DIM)
    lane = jnp.arange(LANES) % HEAD_DIM % ROT_HALF
    freq = inv_freq[lane][None, :]
    out = jax.ShapeDtypeStruct((t, LANES), F32)
    return pl.pallas_call(
        _rope_table_kernel,
        out_shape=(out, out),
        grid=(t // tm,),
        in_specs=[pl.BlockSpec((tm, 1), lambda i: (i, 0)),
                  pl.BlockSpec((1, LANES), lambda i: (0, 0))],
        out_specs=(pl.BlockSpec((tm, LANES), lambda i: (i, 0)),
                   pl.BlockSpec((tm, LANES), lambda i: (i, 0))),
        compiler_params=_cparams("parallel"),
        name="rope_tables",
    )(positions.reshape(t, 1), freq)


def _mem_kv_kernel(m_ref, w_ref, o_ref):
    o_ref[...] = _dot(m_ref[...].astype(BF16), w_ref[...]).astype(BF16)


def _mem_kv(mem2d, w):
    n = mem2d.shape[0]
    tm = min(n, PROJ_TM)
    return pl.pallas_call(
        _mem_kv_kernel,
        out_shape=jax.ShapeDtypeStruct((n, 2 * HM), BF16),
        grid=(n // tm,),
        in_specs=[pl.BlockSpec((tm, D_MODEL), lambda i: (i, 0)),
                  pl.BlockSpec((D_MODEL, 2 * HM), lambda i: (0, 0))],
        out_specs=pl.BlockSpec((tm, 2 * HM), lambda i: (i, 0)),
        compiler_params=_cparams("parallel"),
        name="mem_kv_proj",
    )(mem2d, w)


def _swa_proj_kernel(x_ref, w_ref, cos_ref, sin_ref, q_ref, kk_ref, vv_ref, qc_ref):
    acc = _dot(x_ref[...].astype(BF16), w_ref[...])
    cos, sin = cos_ref[...], sin_ref[...]
    first = (lax.broadcasted_iota(jnp.int32, cos.shape, 1) & (HEAD_DIM - 1)) < ROT_HALF

    def rope(t):
        fwd = pltpu.roll(t, LANES - ROT_HALF, axis=1)
        bwd = pltpu.roll(t, ROT_HALF, axis=1)
        return t * cos + jnp.where(first, fwd, bwd) * sin

    for g in range(HQ // LANES):
        sl = slice(g * LANES, (g + 1) * LANES)
        q_ref[:, sl] = (rope(acc[:, sl]) * ATTN_SCALE).astype(BF16)
    for g in range(HKV2 // LANES):
        src = slice(HQ + g * LANES, HQ + (g + 1) * LANES)
        kk_ref[:, g * LANES:(g + 1) * LANES] = rope(acc[:, src]).astype(BF16)
    vv_ref[...] = acc[:, HQ + HKV2:HQ + 2 * HKV2].astype(BF16)
    qc_ref[...] = (acc[:, HQ + 2 * HKV2:] * ATTN_SCALE).astype(BF16)


def _swa_proj(xf, w, cos_t, sin_t):
    t = xf.shape[0]
    tm = min(t, PROJ_TM)
    row = lambda n: pl.BlockSpec((tm, n), lambda i: (i, 0))
    return pl.pallas_call(
        _swa_proj_kernel,
        out_shape=(jax.ShapeDtypeStruct((t, HQ), BF16), jax.ShapeDtypeStruct((t, HKV2), BF16),
                   jax.ShapeDtypeStruct((t, HKV2), BF16), jax.ShapeDtypeStruct((t, HM), BF16)),
        grid=(t // tm,),
        in_specs=[row(D_MODEL), pl.BlockSpec((D_MODEL, SWA_COLS), lambda i: (0, 0)),
                  row(LANES), row(LANES)],
        out_specs=(row(HQ), row(HKV2), row(HKV2), row(HM)),
        compiler_params=_cparams("parallel"),
        name="swa_proj",
    )(xf, w, cos_t, sin_t)


def _swa_attn_kernel(sink_ref, q_ref, kp_ref, kc_ref, vp_ref, vc_ref, o_ref):
    first_tile = pl.program_id(1) == 0
    half = SWA_TQ - SWA_SUB
    lane = lax.broadcasted_iota(jnp.int32, (SWA_SUB, LANES), 1)
    shift = CHUNK.bit_length() - 1
    rc = lax.broadcasted_iota(jnp.int32, (SWA_SUB, 2 * SWA_SUB), 0) >> shift
    cc = lax.broadcasted_iota(jnp.int32, (SWA_SUB, 2 * SWA_SUB), 1) >> shift
    band = (cc >= rc) & (cc <= rc + WINDOW_CHUNKS)
    band_prev = band & ((cc >= SWA_SUB // CHUNK) | jnp.logical_not(first_tile))
    for sub in range(SWA_TQ // SWA_SUB):
        rows = slice(sub * SWA_SUB, (sub + 1) * SWA_SUB)
        mask = band_prev if sub == 0 else band
        for h in range(SWA_KV_HEADS):
            cols = slice(h * LANES, (h + 1) * LANES)
            if sub == 0:
                kcat = jnp.concatenate([kp_ref[0, half:, cols], kc_ref[0, :SWA_SUB, cols]], axis=0)
                vcat = jnp.concatenate([vp_ref[0, half:, cols], vc_ref[0, :SWA_SUB, cols]], axis=0)
            else:
                kcat = kc_ref[0, :, cols]
                vcat = vc_ref[0, :, cols]
            for pair in range(SWA_GROUP // 2):
                g = h * (SWA_GROUP // 2) + pair
                qcols = slice(g * LANES, (g + 1) * LANES)
                qf = q_ref[0, rows, qcols].astype(F32)
                outs = []
                for a in range(2):
                    keep = (lane < HEAD_DIM) if a == 0 else (lane >= HEAD_DIM)
                    qa = jnp.where(keep, qf, 0.0).astype(BF16)
                    s = jnp.where(mask, _dot_nt(qa, kcat), NEG)
                    sink = sink_ref[2 * g + a]
                    m = jnp.maximum(jnp.max(s, axis=-1, keepdims=True), sink)
                    p = jnp.exp(s - m)
                    den = jnp.sum(p, axis=-1, keepdims=True) + jnp.exp(sink - m)
                    probs = (p * (1.0 / den)).astype(BF16)
                    outs.append(_dot(probs, vcat))
                o = jnp.where(lane < HEAD_DIM, outs[0], outs[1])
                o_ref[0, rows, qcols] = o.astype(BF16)


def _swa_attn(q, kk, vv, sinks):
    b, s, _ = q.shape
    nt = s // SWA_TQ
    cur = lambda n: pl.BlockSpec((1, SWA_TQ, n), lambda bi, i: (bi, i, 0))
    prev = lambda n: pl.BlockSpec((1, SWA_TQ, n), lambda bi, i: (bi, jnp.maximum(i - 1, 0), 0))
    return pl.pallas_call(
        _swa_attn_kernel,
        out_shape=jax.ShapeDtypeStruct((b, s, HQ), BF16),
        grid=(b, nt),
        in_specs=[pl.BlockSpec(memory_space=pltpu.SMEM),
                  cur(HQ), prev(HKV2), cur(HKV2), prev(HKV2), cur(HKV2)],
        out_specs=cur(HQ),
        compiler_params=_cparams("parallel", "parallel"),
        name="swa_attn",
    )(sinks, q, kk, kk, vv, vv)


def _fox_proj_kernel(x_ref, w_ref, bf_ref, q_ref, k_ref, v_ref, qc_ref, d_ref, carry_ref):
    @pl.when(pl.program_id(1) == 0)
    def _():
        carry_ref[...] = jnp.zeros_like(carry_ref)

    acc = _dot(x_ref[0].astype(BF16), w_ref[...])
    q_ref[0] = (acc[:, :HQ] * ATTN_SCALE).astype(BF16)
    k_ref[0] = acc[:, HQ:2 * HQ].astype(BF16)
    v_ref[0] = acc[:, 2 * HQ:3 * HQ].astype(BF16)
    qc_ref[0] = (acc[:, 3 * HQ:3 * HQ + HM] * ATTN_SCALE).astype(BF16)
    z = acc[:, 3 * HQ + HM:] + bf_ref[...]
    log_f = jnp.minimum(z, 0.0) - jnp.log(1.0 + jnp.exp(-jnp.abs(z)))
    tm = z.shape[0]
    tri = (lax.broadcasted_iota(jnp.int32, (tm, tm), 0)
           >= lax.broadcasted_iota(jnp.int32, (tm, tm), 1)).astype(F32).astype(BF16)
    hi, mid, lo = _split3(log_f)
    cum = (_dot(tri, lo) + _dot(tri, mid)) + _dot(tri, hi) + carry_ref[...]
    d_ref[0] = cum
    carry_ref[...] = cum[tm - 1:tm, :]


def _fox_proj(x3, w, bf_row):
    b, s, _ = x3.shape
    tm = min(s, FOX_PROJ_TM)
    row = lambda n: pl.BlockSpec((1, tm, n), lambda bi, i: (bi, i, 0))
    sds = lambda n, dt: jax.ShapeDtypeStruct((b, s, n), dt)
    return pl.pallas_call(
        _fox_proj_kernel,
        out_shape=(sds(HQ, BF16), sds(HQ, BF16), sds(HQ, BF16), sds(HM, BF16), sds(LANES, F32)),
        grid=(b, s // tm),
        in_specs=[row(D_MODEL), pl.BlockSpec((D_MODEL, FOX_COLS), lambda bi, i: (0, 0)),
                  pl.BlockSpec((1, LANES), lambda bi, i: (0, 0))],
        out_specs=(row(HQ), row(HQ), row(HQ), row(HM), row(LANES)),
        scratch_shapes=[pltpu.VMEM((1, LANES), F32)],
        compiler_params=_cparams("arbitrary", "arbitrary"),
        name="fox_proj",
    )(x3, w, bf_row)


def _fox_attn_kernel(q_ref, k_ref, v_ref, d_ref, o_ref, kaug_ref, qaug_ref, vt_ref, qt_ref):
    s_len = q_ref.shape[1]
    tq, tk = qt_ref.shape[3], vt_ref.shape[2]
    pair = pl.program_id(1)
    lane = lax.broadcasted_iota(jnp.int32, (s_len, LANES), 1)
    prow = lax.broadcasted_iota(jnp.int32, (3 * LANES, 2 * LANES), 0)
    pcol = lax.broadcasted_iota(jnp.int32, (3 * LANES, 2 * LANES), 1)
    term = prow >> 7
    kf = k_ref[0].astype(F32)
    dsplit = jnp.concatenate(_split3(d_ref[0]), axis=1)
    for a in range(2):
        head = 2 * pair + a
        from_head = (prow & (LANES - 1)) == head
        place = (jnp.where(from_head & (pcol == term), 1.0, 0.0)
                 + jnp.where(from_head & (pcol == LANES + 3 + term), -1.0, 0.0)).astype(BF16)
        extra = _dot(dsplit, place)
        q_extra = extra[:, :LANES] + jnp.where((lane >= 3) & (lane < 6), 1.0, 0.0)
        k_extra = extra[:, LANES:] + jnp.where(lane < 3, 1.0, 0.0)
        keep = (lane < HEAD_DIM) if a == 0 else (lane >= HEAD_DIM)
        kaug_ref[a, :, :LANES] = jnp.where(keep, kf, 0.0).astype(BF16)
        kaug_ref[a, :, LANES:] = k_extra.astype(BF16)
        qaug_ref[a, :, :LANES] = q_ref[0]
        qaug_ref[a, :, LANES:] = q_extra.astype(BF16)
    for j in range(s_len // tk):
        vt_ref[j] = v_ref[0, j * tk:(j + 1) * tk, :].astype(F32).T.astype(BF16)

    nq = s_len // tq
    for qi in range(nq):
        for a in range(2):
            qt_ref[a, qi] = qaug_ref[a, qi * tq:(qi + 1) * tq, :].astype(F32).T.astype(BF16)

    def tile(kb, qi, a, state):
        q_off = max(0, kb * tk - qi * tq)
        width = tq - q_off
        masked = (kb + 1) * tk - 1 > qi * tq + q_off
        m0, l0, acc0 = state
        m, l, acc = m0[:, q_off:], l0[:, q_off:], acc0[:, q_off:]
        s = _dot(kaug_ref[a, kb * tk:(kb + 1) * tk, :], qt_ref[a, qi, :, q_off:])
        if masked:
            n_mask = min(width, pl.cdiv((kb + 1) * tk - (qi * tq + q_off), LANES) * LANES)
            k_pos = kb * tk + lax.broadcasted_iota(jnp.int32, (tk, n_mask), 0)
            q_pos = qi * tq + q_off + lax.broadcasted_iota(jnp.int32, (tk, n_mask), 1)
            head_cols = jnp.where(k_pos <= q_pos, s[:, :n_mask], NEG)
            s = head_cols if n_mask == width else jnp.concatenate([head_cols, s[:, n_mask:]], axis=1)
        m_new = jnp.maximum(m, jnp.max(s, axis=0, keepdims=True))
        alpha = jnp.exp(m - m_new)
        p = jnp.exp(s - m_new)
        l = alpha * l + jnp.sum(p, axis=0, keepdims=True)
        acc = alpha * acc + _dot(vt_ref[kb], p.astype(BF16))
        if q_off:
            m_new = jnp.concatenate([m0[:, :q_off], m_new], axis=1)
            l = jnp.concatenate([l0[:, :q_off], l], axis=1)
            acc = jnp.concatenate([acc0[:, :q_off], acc], axis=1)
        return m_new, l, acc

    state = {(qi, a): (jnp.full((1, tq), NEG, F32), jnp.zeros((1, tq), F32),
                       jnp.zeros((LANES, tq), F32)) for qi in range(nq) for a in range(2)}
    for kb in range(s_len // tk):
        for qi in range(kb * tk // tq, nq):
            for a in range(2):
                state[qi, a] = tile(kb, qi, a, state[qi, a])
    orow = lax.broadcasted_iota(jnp.int32, (LANES, tq), 0)
    for qi in range(nq):
        (_, l0, acc0), (_, l1, acc1) = state[qi, 0], state[qi, 1]
        o_t = jnp.where(orow < HEAD_DIM, acc0 * (1.0 / l0), acc1 * (1.0 / l1))
        o_ref[0, qi * tq:(qi + 1) * tq, :] = o_t.T.astype(BF16)


def _fox_attn(q, k, v, dcum):
    b, s, _ = q.shape
    tq, tk = min(FOX_TQ, s), min(FOX_TK, s)
    seq = lambda: pl.BlockSpec((1, s, LANES), lambda bi, j: (bi, 0, j))
    return pl.pallas_call(
        _fox_attn_kernel,
        out_shape=jax.ShapeDtypeStruct((b, s, HQ), BF16),
        grid=(b, HQ // LANES),
        in_specs=[seq(), seq(), seq(), pl.BlockSpec((1, s, LANES), lambda bi, j: (bi, 0, 0))],
        out_specs=seq(),
        scratch_shapes=[pltpu.VMEM((2, s, 2 * LANES), BF16), pltpu.VMEM((2, s, 2 * LANES), BF16),
                        pltpu.VMEM((s // tk, LANES, tk), BF16),
                        pltpu.VMEM((2, s // tq, 2 * LANES, tq), BF16)],
        compiler_params=_cparams("parallel", "parallel"),
        name="fox_attn",
    )(q, k, v, dcum)


def _mem_attn_kernel(q_ref, kv_ref, o_ref):
    lane = lax.broadcasted_iota(jnp.int32, (MEM_TQ, LANES), 1)
    for pair in range(HM // LANES):
        cols = slice(pair * LANES, (pair + 1) * LANES)
        kblk = kv_ref[0, :, cols]
        vblk = kv_ref[0, :, HM + pair * LANES:HM + (pair + 1) * LANES]
        qf = q_ref[0, :, cols].astype(F32)
        outs = []
        for a in range(2):
            keep = (lane < HEAD_DIM) if a == 0 else (lane >= HEAD_DIM)
            s = _dot_nt(jnp.where(keep, qf, 0.0).astype(BF16), kblk)
            p = jnp.exp(s - jnp.max(s, axis=-1, keepdims=True))
            l = jnp.sum(p, axis=-1, keepdims=True)
            outs.append(_dot(p.astype(BF16), vblk) * (1.0 / l))
        o_ref[0, :, cols] = jnp.where(lane < HEAD_DIM, outs[0], outs[1]).astype(BF16)


def _mem_attn(qc, kvm):
    b, s, _ = qc.shape
    return pl.pallas_call(
        _mem_attn_kernel,
        out_shape=jax.ShapeDtypeStruct((b, s, HM), BF16),
        grid=(b, s // MEM_TQ),
        in_specs=[pl.BlockSpec((1, MEM_TQ, HM), lambda bi, i: (bi, i, 0)),
                  pl.BlockSpec((1, MEM_LEN, 2 * HM), lambda bi, i: (bi, 0, 0))],
        out_specs=pl.BlockSpec((1, MEM_TQ, HM), lambda bi, i: (bi, i, 0)),
        compiler_params=_cparams("parallel", "parallel"),
        name="mem_attn",
    )(qc, kvm)


def _attn_out_ln(s_ref, c_ref, x_ref, w_ref, g_ref, b_ref):
    mix = _dot(s_ref[...], w_ref[0, :HQ, :]) + _dot(c_ref[...], w_ref[0, HQ:, :])
    return _layer_norm(DN_ALPHA * x_ref[...] + mix, g_ref[...], b_ref[...])


def _attn_out_specs(tm, layer):
    row = lambda n: pl.BlockSpec((tm, n), lambda i: (i, 0))
    vec = pl.BlockSpec((1, D_MODEL), lambda i: (0, 0))
    return [row(HQ), row(HM), row(D_MODEL),
            pl.BlockSpec((1, D_MODEL, D_MODEL), lambda i: (layer, 0, 0),
                         pipeline_mode=pl.Buffered(1)), vec, vec]


def _swiglu(xb, w_in_ref, w_out_ref):
    acc = None
    for lo, hi in FF_CHUNKS:
        ha = _dot(xb, w_in_ref[:, lo:hi])
        hb = _dot(xb, w_in_ref[:, D_FF + lo:D_FF + hi])
        act = (ha * (1.0 / (1.0 + jnp.exp(-ha))) * hb).astype(BF16)
        part = _dot(act, w_out_ref[lo:hi, :])
        acc = part if acc is None else acc + part
    return acc


def _dense_ffn_kernel(s_ref, c_ref, xin_ref, wo_ref, ga_ref, ba_ref, w_in_ref, w_out_ref,
                      g_ref, b_ref, ein_ref, eout_ref, o_ref, ein_bf_ref, eout_bf_ref):
    x = _attn_out_ln(s_ref, c_ref, xin_ref, wo_ref, ga_ref, ba_ref)
    ffn = _swiglu(x.astype(BF16), w_in_ref.at[0], w_out_ref.at[0])
    o_ref[...] = _layer_norm(DN_ALPHA * x + ffn, g_ref[...], b_ref[...])
    ein_bf_ref[...] = ein_ref[...].astype(BF16)
    eout_bf_ref[...] = eout_ref[...].astype(BF16)


def _dense_layer_tail(self_out, cross_out, xf, w_o, depth_idx, g_attn, b_attn,
                      w_in, w_out, layer, g, b, w_exp_in, w_exp_out):
    t = xf.shape[0]
    tm = min(t, FFN_TM)
    steps = t // tm
    n_layers = w_exp_in.shape[0]
    ein = w_exp_in.reshape(n_layers * N_EXPERTS * D_MODEL, 2 * D_FF)
    eout = w_exp_out.reshape(n_layers * N_EXPERTS * D_FF, D_MODEL)
    rin, rout = N_EXPERTS * D_MODEL // steps, N_EXPERTS * D_FF // steps
    assert rin * steps == N_EXPERTS * D_MODEL and rout * steps == N_EXPERTS * D_FF
    assert rin % 16 == 0 and rout % 16 == 0
    row = pl.BlockSpec((tm, D_MODEL), lambda i: (i, 0))
    const = lambda r, c: pl.BlockSpec((r, c), lambda i: (0, 0))
    once = pl.Buffered(1)
    y, ein_bf, eout_bf = pl.pallas_call(
        _dense_ffn_kernel,
        out_shape=(jax.ShapeDtypeStruct((t, D_MODEL), F32),
                   jax.ShapeDtypeStruct((N_EXPERTS * D_MODEL, 2 * D_FF), BF16),
                   jax.ShapeDtypeStruct((N_EXPERTS * D_FF, D_MODEL), BF16)),
        grid=(steps,),
        in_specs=_attn_out_specs(tm, depth_idx) + [
                  pl.BlockSpec((1, D_MODEL, 2 * D_FF), lambda i: (layer, 0, 0), pipeline_mode=once),
                  pl.BlockSpec((1, D_FF, D_MODEL), lambda i: (layer, 0, 0), pipeline_mode=once),
                  const(1, D_MODEL), const(1, D_MODEL),
                  pl.BlockSpec((rin, 2 * D_FF), lambda i: (layer * steps + i, 0)),
                  pl.BlockSpec((rout, D_MODEL), lambda i: (layer * steps + i, 0))],
        out_specs=(row, pl.BlockSpec((rin, 2 * D_FF), lambda i: (i, 0)),
                   pl.BlockSpec((rout, D_MODEL), lambda i: (i, 0))),
        compiler_params=_cparams("parallel"),
        name="dense_ffn_ln",
    )(self_out, cross_out, xf, w_o, g_attn, b_attn, w_in, w_out, g, b, ein, eout)
    return (y, ein_bf.reshape(N_EXPERTS, D_MODEL, 2 * D_FF),
            eout_bf.reshape(N_EXPERTS, D_FF, D_MODEL))


def _expert_kernel(be_ref, bx_ref, xs_ref, w_in_ref, w_out_ref, ys_ref):
    del be_ref
    i = pl.program_id(0)
    used = bx_ref[i] == i

    @pl.when(used)
    def _():
        ys_ref[...] = _swiglu(xs_ref[...].astype(BF16), w_in_ref.at[0], w_out_ref.at[0])

    @pl.when(jnp.logical_not(used))
    def _():
        ys_ref[...] = jnp.zeros_like(ys_ref)


def _expert_ffn(xs, w_in, w_out, block_expert, block_src):
    n_slots = xs.shape[0]
    return pl.pallas_call(
        _expert_kernel,
        out_shape=jax.ShapeDtypeStruct((n_slots, D_MODEL), F32),
        grid_spec=pltpu.PrefetchScalarGridSpec(
            num_scalar_prefetch=2,
            grid=(n_slots // MOE_TM,),
            in_specs=[pl.BlockSpec((MOE_TM, D_MODEL), lambda i, be, bx: (bx[i], 0)),
                      pl.BlockSpec((1, D_MODEL, 2 * D_FF), lambda i, be, bx: (be[i], 0, 0)),
                      pl.BlockSpec((1, D_FF, D_MODEL), lambda i, be, bx: (be[i], 0, 0))],
            out_specs=pl.BlockSpec((MOE_TM, D_MODEL), lambda i, be, bx: (i, 0))),
        compiler_params=_cparams("arbitrary"),
        name="expert_ffn",
    )(block_expert, block_src, xs, w_in, w_out)


ROUTE_E1, ROUTE_E2, ROUTE_R1, ROUTE_R2, ROUTE_G1, ROUTE_G2 = range(6)


def _router_kernel(s_ref, c_ref, xin_ref, wo_ref, ga_ref, ba_ref, w_ref, b_ref,
                   x_ref, route_ref, count_ref, carry_ref):
    @pl.when(pl.program_id(0) == 0)
    def _():
        carry_ref[...] = jnp.zeros_like(carry_ref)

    x = _attn_out_ln(s_ref, c_ref, xin_ref, wo_ref, ga_ref, ba_ref)
    x_ref[...] = x
    xs, ws = _split3(x), _split3(w_ref[...])
    logits = (_dot(xs[1], ws[0]) + _dot(xs[0], ws[1])) + _dot(xs[0], ws[0])
    tm = logits.shape[0]
    lane = lax.broadcasted_iota(jnp.int32, (tm, LANES), 1).astype(F32)
    logits = jnp.where(lane < N_EXPERTS, logits + b_ref[...], -jnp.inf)
    m1 = jnp.max(logits, axis=-1, keepdims=True)
    e1 = jnp.min(jnp.where(logits == m1, lane, float(LANES)), axis=-1, keepdims=True)
    rest = jnp.where(lane == e1, -jnp.inf, logits)
    m2 = jnp.max(rest, axis=-1, keepdims=True)
    e2 = jnp.min(jnp.where(rest == m2, lane, float(LANES)), axis=-1, keepdims=True)
    ex = jnp.exp(m2 - m1)
    g1 = 1.0 / (1.0 + ex)
    g2 = ex * g1
    sel1, sel2 = lane == e1, lane == e2
    onehot = jnp.where(sel1 | sel2, 1.0, 0.0)
    tri = (lax.broadcasted_iota(jnp.int32, (tm, tm), 0)
           > lax.broadcasted_iota(jnp.int32, (tm, tm), 1)).astype(F32).astype(BF16)
    rank = _dot(tri, onehot.astype(BF16)) + carry_ref[...]
    r1 = jnp.sum(jnp.where(sel1, rank, 0.0), axis=-1, keepdims=True)
    r2 = jnp.sum(jnp.where(sel2, rank, 0.0), axis=-1, keepdims=True)
    total = carry_ref[...] + jnp.sum(onehot, axis=0, keepdims=True)
    carry_ref[...] = total
    count_ref[...] = total
    route = jnp.zeros((tm, LANES), F32)
    for ln, val in ((ROUTE_E1, e1), (ROUTE_E2, e2), (ROUTE_R1, r1), (ROUTE_R2, r2),
                    (ROUTE_G1, g1), (ROUTE_G2, g2)):
        route = jnp.where(lane == float(ln), val, route)
    route_ref[...] = route


def _attn_out_router(self_out, cross_out, xf, w_o, depth_idx, g_attn, b_attn, w_pad, b_pad):
    t = xf.shape[0]
    tm = min(t, ROUTER_TM)
    return pl.pallas_call(
        _router_kernel,
        out_shape=(jax.ShapeDtypeStruct((t, D_MODEL), F32), jax.ShapeDtypeStruct((t, LANES), F32),
                   jax.ShapeDtypeStruct((1, LANES), F32)),
        grid=(t // tm,),
        in_specs=_attn_out_specs(tm, depth_idx) + [
                  pl.BlockSpec((D_MODEL, LANES), lambda i: (0, 0)),
                  pl.BlockSpec((1, LANES), lambda i: (0, 0))],
        out_specs=(pl.BlockSpec((tm, D_MODEL), lambda i: (i, 0)),
                   pl.BlockSpec((tm, LANES), lambda i: (i, 0)),
                   pl.BlockSpec((1, LANES), lambda i: (0, 0))),
        scratch_shapes=[pltpu.VMEM((1, LANES), F32)],
        compiler_params=_cparams("arbitrary"),
        name="router",
    )(self_out, cross_out, xf, w_o, g_attn, b_attn, w_pad, b_pad)


def _row_copies(n, make_copy):
    for r in range(n):
        make_copy(r, 0).start(priority=0)
        make_copy(r, 1).start(priority=1)
    for r in range(n):
        make_copy(r, 0).wait()
        make_copy(r, 1).wait()


def _dispatch_kernel(dest_ref, pend_ref, x_ref, xs_hbm, zero_ref, sem, zsem):
    base = pl.program_id(0) * ROW_TM

    @pl.when(pl.program_id(0) == 0)
    def _():
        zero_ref[...] = jnp.zeros_like(zero_ref)

        def zero_copy(e):
            start = pl.multiple_of(pend_ref[e + 1] - MOE_TM, MOE_TM)
            return pltpu.make_async_copy(zero_ref, xs_hbm.at[pl.ds(start, MOE_TM)], zsem)

        def tail_copy(blk):
            return pltpu.make_async_copy(
                zero_ref, xs_hbm.at[pl.ds(pl.multiple_of(blk * MOE_TM, MOE_TM), MOE_TM)], zsem)

        first_tail = pend_ref[N_EXPERTS] // MOE_TM
        n_blocks = xs_hbm.shape[0] // MOE_TM
        for e in range(N_EXPERTS):
            @pl.when(pend_ref[e + 1] > pend_ref[e])
            def _():
                zero_copy(e).start()
        lax.fori_loop(first_tail, n_blocks, lambda blk, c: (tail_copy(blk).start(), c)[1], 0)
        for e in range(N_EXPERTS):
            @pl.when(pend_ref[e + 1] > pend_ref[e])
            def _():
                zero_copy(e).wait()
        lax.fori_loop(first_tail, n_blocks, lambda blk, c: (tail_copy(blk).wait(), c)[1], 0)

    def make_copy(r, k):
        return pltpu.make_async_copy(x_ref.at[pl.ds(r, 1)],
                                     xs_hbm.at[pl.ds(dest_ref[2 * (base + r) + k], 1)], sem)

    _row_copies(ROW_TM, make_copy)


def _dispatch(xf, dest, seg_ends, n_slots):
    t = xf.shape[0]
    return pl.pallas_call(
        _dispatch_kernel,
        out_shape=jax.ShapeDtypeStruct((n_slots, D_MODEL), F32),
        grid_spec=pltpu.PrefetchScalarGridSpec(
            num_scalar_prefetch=2,
            grid=(t // ROW_TM,),
            in_specs=[pl.BlockSpec((ROW_TM, D_MODEL), lambda i, d, p: (i, 0))],
            out_specs=pl.BlockSpec(memory_space=pl.ANY),
            scratch_shapes=[pltpu.VMEM((MOE_TM, D_MODEL), F32), pltpu.SemaphoreType.DMA(()),
                            pltpu.SemaphoreType.DMA(())]),
        compiler_params=_cparams("arbitrary"),
        name="moe_dispatch",
    )(dest, seg_ends, xf)


def _combine_kernel(dest_ref, x_ref, route_ref, g_ref, b_ref, ys_hbm, o_ref, ya, yb, sem):
    base = pl.program_id(0) * ROW_TM

    def make_copy(r, k):
        dst = ya if k == 0 else yb
        return pltpu.make_async_copy(ys_hbm.at[pl.ds(dest_ref[2 * (base + r) + k], 1)],
                                     dst.at[pl.ds(r, 1)], sem)

    _row_copies(ROW_TM, make_copy)
    route = route_ref[...]
    g1 = route[:, ROUTE_G1:ROUTE_G1 + 1]
    g2 = route[:, ROUTE_G2:ROUTE_G2 + 1]
    y = ya[...] * g1 + yb[...] * g2
    o_ref[...] = _layer_norm(DN_ALPHA * x_ref[...] + y, g_ref[...], b_ref[...])


def _combine_ln(xf, route, ys, dest, g, b):
    t = xf.shape[0]
    row = lambda n: pl.BlockSpec((ROW_TM, n), lambda i, d: (i, 0))
    const = pl.BlockSpec((1, D_MODEL), lambda i, d: (0, 0))
    return pl.pallas_call(
        _combine_kernel,
        out_shape=jax.ShapeDtypeStruct((t, D_MODEL), F32),
        grid_spec=pltpu.PrefetchScalarGridSpec(
            num_scalar_prefetch=1,
            grid=(t // ROW_TM,),
            in_specs=[row(D_MODEL), row(LANES), const, const, pl.BlockSpec(memory_space=pl.ANY)],
            out_specs=row(D_MODEL),
            scratch_shapes=[pltpu.VMEM((ROW_TM, D_MODEL), F32), pltpu.VMEM((ROW_TM, D_MODEL), F32),
                            pltpu.SemaphoreType.DMA(())]),
        compiler_params=_cparams("arbitrary"),
        name="moe_combine_ln",
    )(dest, xf, route, g, b, ys)


def _moe_layer_tail(self_out, cross_out, xf, w_o, depth_idx, g_attn, b_attn,
                    w_router, b_router, w_in, w_out, g, b):
    t = xf.shape[0]
    w_pad = jnp.zeros((D_MODEL, LANES), F32).at[:, :N_EXPERTS].set(w_router)
    b_pad = jnp.zeros((1, LANES), F32).at[0, :N_EXPERTS].set(b_router)
    xf, route, counts = _attn_out_router(self_out, cross_out, xf, w_o, depth_idx, g_attn, b_attn,
                                         w_pad, b_pad)
    counts = counts[0, :N_EXPERTS].astype(jnp.int32)
    pcounts = (counts + MOE_TM - 1) // MOE_TM * MOE_TM
    pends = jnp.cumsum(pcounts)
    pstarts = pends - pcounts
    e = route[:, ROUTE_E1:ROUTE_E2 + 1].astype(jnp.int32)
    r = route[:, ROUTE_R1:ROUTE_R2 + 1].astype(jnp.int32)
    seg_start = sum(jnp.where(e == k, pstarts[k], 0) for k in range(N_EXPERTS))
    dest = (seg_start + r).reshape(-1)
    n_blocks = 2 * t // MOE_TM + N_EXPERTS
    blk_start = jnp.arange(n_blocks, dtype=jnp.int32) * MOE_TM
    block_expert = jnp.minimum(jnp.sum(blk_start[:, None] >= pends[None, :], axis=1),
                               N_EXPERTS - 1).astype(jnp.int32)
    n_used = pends[-1] // MOE_TM
    block_src = jnp.minimum(jnp.arange(n_blocks, dtype=jnp.int32), n_used - 1)
    seg_ends = jnp.concatenate([jnp.zeros((1,), jnp.int32), pends.astype(jnp.int32)])
    xs = _dispatch(xf, dest, seg_ends, n_blocks * MOE_TM)
    ys = _expert_ffn(xs, w_in, w_out, block_expert, block_src)
    return _combine_ln(xf, route, ys, dest, g, b)


def _dup_heads(w):
    d, n = w.shape
    w = w.reshape(d, n // HEAD_DIM, 1, HEAD_DIM)
    return jnp.broadcast_to(w, (d, n // HEAD_DIM, 2, HEAD_DIM)).reshape(d, 2 * n)


def kernel(x, mem, positions, w_in_swa, attn_sinks, w_in_fox, b_forget, w_mem_kv, w_out,
           ln_attn_g, ln_attn_b, w_ffn_in, w_ffn_out, w_router, b_router, w_exp_in,
           w_exp_out, ln_ffn_g, ln_ffn_b):
    b, s, d = x.shape
    t = b * s
    hkv = SWA_KV_HEADS * HEAD_DIM
    cos_t, sin_t = _rope_tables(positions)
    mem2d = mem.reshape(b * MEM_LEN, d)
    xf = x.reshape(t, d)
    w_out_b = w_out.astype(BF16)
    w_ffn_in_b, w_ffn_out_b = w_ffn_in.astype(BF16), w_ffn_out.astype(BF16)
    exp_w = None
    for i in range(DEPTH):
        j = i // 2
        kvm = _mem_kv(mem2d, w_mem_kv[i].astype(BF16)).reshape(b, MEM_LEN, 2 * HM)
        if i % 2 == 0:
            w = w_in_swa[j]
            w = jnp.concatenate([w[:, :HQ], _dup_heads(w[:, HQ:HQ + hkv]),
                                 _dup_heads(w[:, HQ + hkv:HQ + 2 * hkv]), w[:, HQ + 2 * hkv:]],
                                axis=1).astype(BF16)
            q, kk, vv, qc = _swa_proj(xf, w, cos_t, sin_t)
            self_out = _swa_attn(q.reshape(b, s, HQ), kk.reshape(b, s, HKV2),
                                 vv.reshape(b, s, HKV2), attn_sinks[j])
            qc = qc.reshape(b, s, HM)
        else:
            w = jnp.pad(w_in_fox[j], ((0, 0), (0, FOX_COLS - w_in_fox.shape[-1]))).astype(BF16)
            bf_row = jnp.pad(b_forget[j], (0, LANES - SELF_HEADS))[None, :]
            q, k, v, qc, dcum = _fox_proj(xf.reshape(b, s, d), w, bf_row)
            self_out = _fox_attn(q, k, v, dcum)
        cross_out = _mem_attn(qc, kvm)
        attn = (self_out.reshape(t, HQ), cross_out.reshape(t, HM), xf, w_out_b, i,
                ln_attn_g[i][None, :], ln_attn_b[i][None, :])
        g, bb = ln_ffn_g[i][None, :], ln_ffn_b[i][None, :]
        if i % 2 == 0:
            xf, *exp_w = _dense_layer_tail(*attn, w_ffn_in_b, w_ffn_out_b, j, g, bb,
                                           w_exp_in, w_exp_out)
        else:
            xf = _moe_layer_tail(*attn, w_router[j], b_router[j], exp_w[0], exp_w[1], g, bb)
    return xf.reshape(b, s, d)
```

```python
import functools

import jax
import jax.numpy as jnp
from jax import lax
from jax.experimental import pallas as pl
from jax.experimental.pallas import tpu as pltpu

D_MODEL = 1024
DEPTH = 4
CHUNK = 64
HEAD_DIM = 64
MEM_HEADS = 4
SELF_HEADS = 12
SWA_KV_HEADS = 3
SWA_GROUP = SELF_HEADS // SWA_KV_HEADS
WINDOW_CHUNKS = 2
ROT_DIM = HEAD_DIM // 4
ROT_HALF = ROT_DIM // 2
ROPE_THETA = 500000.0
MEM_LEN = 256
D_FF = 2816
N_EXPERTS = 8
LN_EPS = 1e-5
NEG = -1e30
ATTN_SCALE = HEAD_DIM ** -0.5
DN_ALPHA = (2.0 * DEPTH) ** 0.25

HQ = SELF_HEADS * HEAD_DIM
HM = MEM_HEADS * HEAD_DIM
HKV2 = 2 * SWA_KV_HEADS * HEAD_DIM
SWA_COLS = HQ + 2 * HKV2 + HM
FOX_COLS = 3 * HQ + HM + 128

LANES = 128
VMEM_LIMIT_BYTES = 56 * 1024 * 1024

PROJ_TM = 512
FOX_PROJ_TM = 512
SWA_TQ = 256
SWA_SUB = 128
FOX_TQ = 2048
FOX_TK = 512
MEM_TQ = 512
FFN_TM = 256
MXU_DIM = 256
FF_CHUNKS = ((0, 6 * MXU_DIM), (6 * MXU_DIM, D_FF))
ROUTER_TM = 512
MOE_TM = 256
ROW_TM = 256

F32 = jnp.float32
BF16 = jnp.bfloat16


def _cparams(*sem):
    return pltpu.CompilerParams(dimension_semantics=sem, vmem_limit_bytes=VMEM_LIMIT_BYTES)


def _layer_norm(y, g, b):
    mu = jnp.mean(y, axis=-1, keepdims=True)
    d = y - mu
    var = jnp.mean(d * d, axis=-1, keepdims=True)
    return d * lax.rsqrt(var + LN_EPS) * g + b


def _split3(a):
    hi = a.astype(BF16)
    r1 = a - hi.astype(F32)
    mid = r1.astype(BF16)
    lo = (r1 - mid.astype(F32)).astype(BF16)
    return hi, mid, lo


def _dot(a, b):
    return jnp.dot(a, b, preferred_element_type=F32)


def _dot_nt(a, b):
    return lax.dot_general(a, b, (((1,), (1,)), ((), ())), preferred_element_type=F32)


def _rope_table_kernel(pos_ref, freq_ref, cos_ref, sin_ref):
    ang = pos_ref[...].astype(F32) * freq_ref[...]
    sub = lax.broadcasted_iota(jnp.int32, ang.shape, 1) & (HEAD_DIM - 1)
    c, s = jnp.cos(ang), jnp.sin(ang)
    cos_ref[...] = jnp.where(sub < ROT_DIM, c, 1.0)
    sin_ref[...] = jnp.where(sub < ROT_HALF, -s, jnp.where(sub < ROT_DIM, s, 0.0))


def _rope_tables(positions):
    t = positions.size
    tm = min(t, 2048)
    inv_freq = ROPE_THETA ** (-jnp.arange(0, ROT_DIM, 2, dtype=F32) / ROT_DIM)
    lane = jnp.arange(LANES) % HEAD_DIM % ROT_HALF
    freq = inv_freq[lane][None, :]
    out = jax.ShapeDtypeStruct((t, LANES), F32)
    return pl.pallas_call(
        _rope_table_kernel,
        out_shape=(out, out),
        grid=(t // tm,),
        in_specs=[pl.BlockSpec((tm, 1), lambda i: (i, 0)),
                  pl.BlockSpec((1, LANES), lambda i: (0, 0))],
        out_specs=(pl.BlockSpec((tm, LANES), lambda i: (i, 0)),
                   pl.BlockSpec((tm, LANES), lambda i: (i, 0))),
        compiler_params=_cparams("parallel"),
        name="rope_tables",
    )(positions.reshape(t, 1), freq)


def _mem_kv_kernel(m_ref, w_ref, o_ref):
    o_ref[...] = _dot(m_ref[...].astype(BF16), w_ref[...]).astype(BF16)


def _mem_kv(mem2d, w):
    n = mem2d.shape[0]
    tm = min(n, PROJ_TM)
    return pl.pallas_call(
        _mem_kv_kernel,
        out_shape=jax.ShapeDtypeStruct((n, 2 * HM), BF16),
        grid=(n // tm,),
        in_specs=[pl.BlockSpec((tm, D_MODEL), lambda i: (i, 0)),
                  pl.BlockSpec((D_MODEL, 2 * HM), lambda i: (0, 0))],
        out_specs=pl.BlockSpec((tm, 2 * HM), lambda i: (i, 0)),
        compiler_params=_cparams("parallel"),
        name="mem_kv_proj",
    )(mem2d, w)


def _swa_proj_kernel(x_ref, w_ref, cos_ref, sin_ref, q_ref, kk_ref, vv_ref, qc_ref):
    acc = _dot(x_ref[...].astype(BF16), w_ref[...])
    cos, sin = cos_ref[...], sin_ref[...]
    first = (lax.broadcasted_iota(jnp.int32, cos.shape, 1) & (HEAD_DIM - 1)) < ROT_HALF

    def rope(t):
        fwd = pltpu.roll(t, LANES - ROT_HALF, axis=1)
        bwd = pltpu.roll(t, ROT_HALF, axis=1)
        return t * cos + jnp.where(first, fwd, bwd) * sin

    for g in range(HQ // LANES):
        sl = slice(g * LANES, (g + 1) * LANES)
        q_ref[:, sl] = (rope(acc[:, sl]) * ATTN_SCALE).astype(BF16)
    for g in range(HKV2 // LANES):
        src = slice(HQ + g * LANES, HQ + (g + 1) * LANES)
        kk_ref[:, g * LANES:(g + 1) * LANES] = rope(acc[:, src]).astype(BF16)
    vv_ref[...] = acc[:, HQ + HKV2:HQ + 2 * HKV2].astype(BF16)
    qc_ref[...] = (acc[:, HQ + 2 * HKV2:] * ATTN_SCALE).astype(BF16)


def _swa_proj(xf, w, cos_t, sin_t):
    t = xf.shape[0]
    tm = min(t, PROJ_TM)
    row = lambda n: pl.BlockSpec((tm, n), lambda i: (i, 0))
    return pl.pallas_call(
        _swa_proj_kernel,
        out_shape=(jax.ShapeDtypeStruct((t, HQ), BF16), jax.ShapeDtypeStruct((t, HKV2), BF16),
                   jax.ShapeDtypeStruct((t, HKV2), BF16), jax.ShapeDtypeStruct((t, HM), BF16)),
        grid=(t // tm,),
        in_specs=[row(D_MODEL), pl.BlockSpec((D_MODEL, SWA_COLS), lambda i: (0, 0)),
                  row(LANES), row(LANES)],
        out_specs=(row(HQ), row(HKV2), row(HKV2), row(HM)),
        compiler_params=_cparams("parallel"),
        name="swa_proj",
    )(xf, w, cos_t, sin_t)


def _swa_attn_kernel(sink_ref, q_ref, kp_ref, kc_ref, vp_ref, vc_ref, o_ref):
    first_tile = pl.program_id(1) == 0
    half = SWA_TQ - SWA_SUB
    lane = lax.broadcasted_iota(jnp.int32, (SWA_SUB, LANES), 1)
    shift = CHUNK.bit_length() - 1
    rc = lax.broadcasted_iota(jnp.int32, (SWA_SUB, 2 * SWA_SUB), 0) >> shift
    cc = lax.broadcasted_iota(jnp.int32, (SWA_SUB, 2 * SWA_SUB), 1) >> shift
    band = (cc >= rc) & (cc <= rc + WINDOW_CHUNKS)
    band_prev = band & ((cc >= SWA_SUB // CHUNK) | jnp.logical_not(first_tile))
    for sub in range(SWA_TQ // SWA_SUB):
        rows = slice(sub * SWA_SUB, (sub + 1) * SWA_SUB)
        mask = band_prev if sub == 0 else band
        for h in range(SWA_KV_HEADS):
            cols = slice(h * LANES, (h + 1) * LANES)
            if sub == 0:
                kcat = jnp.concatenate([kp_ref[0, half:, cols], kc_ref[0, :SWA_SUB, cols]], axis=0)
                vcat = jnp.concatenate([vp_ref[0, half:, cols], vc_ref[0, :SWA_SUB, cols]], axis=0)
            else:
                kcat = kc_ref[0, :, cols]
                vcat = vc_ref[0, :, cols]
            for pair in range(SWA_GROUP // 2):
                g = h * (SWA_GROUP // 2) + pair
                qcols = slice(g * LANES, (g + 1) * LANES)
                qf = q_ref[0, rows, qcols].astype(F32)
                outs = []
                for a in range(2):
                    keep = (lane < HEAD_DIM) if a == 0 else (lane >= HEAD_DIM)
                    qa = jnp.where(keep, qf, 0.0).astype(BF16)
                    s = jnp.where(mask, _dot_nt(qa, kcat), NEG)
                    sink = sink_ref[2 * g + a]
                    m = jnp.maximum(jnp.max(s, axis=-1, keepdims=True), sink)
                    p = jnp.exp(s - m)
                    den = jnp.sum(p, axis=-1, keepdims=True) + jnp.exp(sink - m)
                    probs = (p * (1.0 / den)).astype(BF16)
                    outs.append(_dot(probs, vcat))
                o = jnp.where(lane < HEAD_DIM, outs[0], outs[1])
                o_ref[0, rows, qcols] = o.astype(BF16)


def _swa_attn(q, kk, vv, sinks):
    b, s, _ = q.shape
    nt = s // SWA_TQ
    cur = lambda n: pl.BlockSpec((1, SWA_TQ, n), lambda bi, i: (bi, i, 0))
    prev = lambda n: pl.BlockSpec((1, SWA_TQ, n), lambda bi, i: (bi, jnp.maximum(i - 1, 0), 0))
    return pl.pallas_call(
        _swa_attn_kernel,
        out_shape=jax.ShapeDtypeStruct((b, s, HQ), BF16),
        grid=(b, nt),
        in_specs=[pl.BlockSpec(memory_space=pltpu.SMEM),
                  cur(HQ), prev(HKV2), cur(HKV2), prev(HKV2), cur(HKV2)],
        out_specs=cur(HQ),
        compiler_params=_cparams("parallel", "parallel"),
        name="swa_attn",
    )(sinks, q, kk, kk, vv, vv)


def _fox_proj_kernel(x_ref, w_ref, bf_ref, q_ref, k_ref, v_ref, qc_ref, d_ref, carry_ref):
    @pl.when(pl.program_id(1) == 0)
    def _():
        carry_ref[...] = jnp.zeros_like(carry_ref)

    acc = _dot(x_ref[0].astype(BF16), w_ref[...])
    q_ref[0] = (acc[:, :HQ] * ATTN_SCALE).astype(BF16)
    k_ref[0] = acc[:, HQ:2 * HQ].astype(BF16)
    v_ref[0] = acc[:, 2 * HQ:3 * HQ].astype(BF16)
    qc_ref[0] = (acc[:, 3 * HQ:3 * HQ + HM] * ATTN_SCALE).astype(BF16)
    z = acc[:, 3 * HQ + HM:] + bf_ref[...]
    log_f = jnp.minimum(z, 0.0) - jnp.log(1.0 + jnp.exp(-jnp.abs(z)))
    tm = z.shape[0]
    tri = (lax.broadcasted_iota(jnp.int32, (tm, tm), 0)
           >= lax.broadcasted_iota(jnp.int32, (tm, tm), 1)).astype(F32).astype(BF16)
    hi, mid, lo = _split3(log_f)
    cum = (_dot(tri, lo) + _dot(tri, mid)) + _dot(tri, hi) + carry_ref[...]
    d_ref[0] = cum
    carry_ref[...] = cum[tm - 1:tm, :]


def _fox_proj(x3, w, bf_row):
    b, s, _ = x3.shape
    tm = min(s, FOX_PROJ_TM)
    row = lambda n: pl.BlockSpec((1, tm, n), lambda bi, i: (bi, i, 0))
    sds = lambda n, dt: jax.ShapeDtypeStruct((b, s, n), dt)
    return pl.pallas_call(
        _fox_proj_kernel,
        out_shape=(sds(HQ, BF16), sds(HQ, BF16), sds(HQ, BF16), sds(HM, BF16), sds(LANES, F32)),
        grid=(b, s // tm),
        in_specs=[row(D_MODEL), pl.BlockSpec((D_MODEL, FOX_COLS), lambda bi, i: (0, 0)),
                  pl.BlockSpec((1, LANES), lambda bi, i: (0, 0))],
        out_specs=(row(HQ), row(HQ), row(HQ), row(HM), row(LANES)),
        scratch_shapes=[pltpu.VMEM((1, LANES), F32)],
        compiler_params=_cparams("arbitrary", "arbitrary"),
        name="fox_proj",
    )(x3, w, bf_row)


def _fox_attn_kernel(q_ref, k_ref, v_ref, d_ref, o_ref, kaug_ref, qaug_ref, vt_ref, qt_ref):
    s_len = q_ref.shape[1]
    tq, tk = qt_ref.shape[3], vt_ref.shape[2]
    pair = pl.program_id(1)
    lane = lax.broadcasted_iota(jnp.int32, (s_len, LANES), 1)
    prow = lax.broadcasted_iota(jnp.int32, (3 * LANES, 2 * LANES), 0)
    pcol = lax.broadcasted_iota(jnp.int32, (3 * LANES, 2 * LANES), 1)
    term = prow >> 7
    kf = k_ref[0].astype(F32)
    dsplit = jnp.concatenate(_split3(d_ref[0]), axis=1)
    for a in range(2):
        head = 2 * pair + a
        from_head = (prow & (LANES - 1)) == head
        place = (jnp.where(from_head & (pcol == term), 1.0, 0.0)
                 + jnp.where(from_head & (pcol == LANES + 3 + term), -1.0, 0.0)).astype(BF16)
        extra = _dot(dsplit, place)
        q_extra = extra[:, :LANES] + jnp.where((lane >= 3) & (lane < 6), 1.0, 0.0)
        k_extra = extra[:, LANES:] + jnp.where(lane < 3, 1.0, 0.0)
        keep = (lane < HEAD_DIM) if a == 0 else (lane >= HEAD_DIM)
        kaug_ref[a, :, :LANES] = jnp.where(keep, kf, 0.0).astype(BF16)
        kaug_ref[a, :, LANES:] = k_extra.astype(BF16)
        qaug_ref[a, :, :LANES] = q_ref[0]
        qaug_ref[a, :, LANES:] = q_extra.astype(BF16)
    for j in range(s_len // tk):
        vt_ref[j] = v_ref[0, j * tk:(j + 1) * tk, :].astype(F32).T.astype(BF16)

    nq = s_len // tq
    for qi in range(nq):
        for a in range(2):
            qt_ref[a, qi] = qaug_ref[a, qi * tq:(qi + 1) * tq, :].astype(F32).T.astype(BF16)

    def tile(kb, qi, a, state):
        q_off = max(0, kb * tk - qi * tq)
        width = tq - q_off
        masked = (kb + 1) * tk - 1 > qi * tq + q_off
        m0, l0, acc0 = state
        m, l, acc = m0[:, q_off:], l0[:, q_off:], acc0[:, q_off:]
        s = _dot(kaug_ref[a, kb * tk:(kb + 1) * tk, :], qt_ref[a, qi, :, q_off:])
        if masked:
            n_mask = min(width, pl.cdiv((kb + 1) * tk - (qi * tq + q_off), LANES) * LANES)
            k_pos = kb * tk + lax.broadcasted_iota(jnp.int32, (tk, n_mask), 0)
            q_pos = qi * tq + q_off + lax.broadcasted_iota(jnp.int32, (tk, n_mask), 1)
            head_cols = jnp.where(k_pos <= q_pos, s[:, :n_mask], NEG)
            s = head_cols if n_mask == width else jnp.concatenate([head_cols, s[:, n_mask:]], axis=1)
        m_new = jnp.maximum(m, jnp.max(s, axis=0, keepdims=True))
        alpha = jnp.exp(m - m_new)
        p = jnp.exp(s - m_new)
        l = alpha * l + jnp.sum(p, axis=0, keepdims=True)
        acc = alpha * acc + _dot(vt_ref[kb], p.astype(BF16))
        if q_off:
            m_new = jnp.concatenate([m0[:, :q_off], m_new], axis=1)
            l = jnp.concatenate([l0[:, :q_off], l], axis=1)
            acc = jnp.concatenate([acc0[:, :q_off], acc], axis=1)
        return m_new, l, acc

    state = {(qi, a): (jnp.full((1, tq), NEG, F32), jnp.zeros((1, tq), F32),
                       jnp.zeros((LANES, tq), F32)) for qi in range(nq) for a in range(2)}
    for kb in range(s_len // tk):
        for qi in range(kb * tk // tq, nq):
            for a in range(2):
                state[qi, a] = tile(kb, qi, a, state[qi, a])
    orow = lax.broadcasted_iota(jnp.int32, (LANES, tq), 0)
    for qi in range(nq):
        (_, l0, acc0), (_, l1, acc1) = state[qi, 0], state[qi, 1]
        o_t = jnp.where(orow < HEAD_DIM, acc0 * (1.0 / l0), acc1 * (1.0 / l1))
        o_ref[0, qi * tq:(qi + 1) * tq, :] = o_t.T.astype(BF16)


def _fox_attn(q, k, v, dcum):
    b, s, _ = q.shape
    tq, tk = min(FOX_TQ, s), min(FOX_TK, s)
    seq = lambda: pl.BlockSpec((1, s, LANES), lambda bi, j: (bi, 0, j))
    return pl.pallas_call(
        _fox_attn_kernel,
        out_shape=jax.ShapeDtypeStruct((b, s, HQ), BF16),
        grid=(b, HQ // LANES),
        in_specs=[seq(), seq(), seq(), pl.BlockSpec((1, s, LANES), lambda bi, j: (bi, 0, 0))],
        out_specs=seq(),
        scratch_shapes=[pltpu.VMEM((2, s, 2 * LANES), BF16), pltpu.VMEM((2, s, 2 * LANES), BF16),
                        pltpu.VMEM((s // tk, LANES, tk), BF16),
                        pltpu.VMEM((2, s // tq, 2 * LANES, tq), BF16)],
        compiler_params=_cparams("parallel", "parallel"),
        name="fox_attn",
    )(q, k, v, dcum)


def _mem_attn_kernel(q_ref, kv_ref, o_ref):
    lane = lax.broadcasted_iota(jnp.int32, (MEM_TQ, LANES), 1)
    for pair in range(HM // LANES):
        cols = slice(pair * LANES, (pair + 1) * LANES)
        kblk = kv_ref[0, :, cols]
        vblk = kv_ref[0, :, HM + pair * LANES:HM + (pair + 1) * LANES]
        qf = q_ref[0, :, cols].astype(F32)
        outs = []
        for a in range(2):
            keep = (lane < HEAD_DIM) if a == 0 else (lane >= HEAD_DIM)
            s = _dot_nt(jnp.where(keep, qf, 0.0).astype(BF16), kblk)
            p = jnp.exp(s - jnp.max(s, axis=-1, keepdims=True))
            l = jnp.sum(p, axis=-1, keepdims=True)
            outs.append(_dot(p.astype(BF16), vblk) * (1.0 / l))
        o_ref[0, :, cols] = jnp.where(lane < HEAD_DIM, outs[0], outs[1]).astype(BF16)


def _mem_attn(qc, kvm):
    b, s, _ = qc.shape
    return pl.pallas_call(
        _mem_attn_kernel,
        out_shape=jax.ShapeDtypeStruct((b, s, HM), BF16),
        grid=(b, s // MEM_TQ),
        in_specs=[pl.BlockSpec((1, MEM_TQ, HM), lambda bi, i: (bi, i, 0)),
                  pl.BlockSpec((1, MEM_LEN, 2 * HM), lambda bi, i: (bi, 0, 0))],
        out_specs=pl.BlockSpec((1, MEM_TQ, HM), lambda bi, i: (bi, i, 0)),
        compiler_params=_cparams("parallel", "parallel"),
        name="mem_attn",
    )(qc, kvm)


def _attn_out_ln(s_ref, c_ref, x_ref, w_ref, g_ref, b_ref):
    mix = _dot(s_ref[...], w_ref[0, :HQ, :]) + _dot(c_ref[...], w_ref[0, HQ:, :])
    return _layer_norm(DN_ALPHA * x_ref[...] + mix, g_ref[...], b_ref[...])


def _attn_out_specs(tm, layer):
    row = lambda n: pl.BlockSpec((tm, n), lambda i: (i, 0))
    vec = pl.BlockSpec((1, D_MODEL), lambda i: (0, 0))
    return [row(HQ), row(HM), row(D_MODEL),
            pl.BlockSpec((1, D_MODEL, D_MODEL), lambda i: (layer, 0, 0),
                         pipeline_mode=pl.Buffered(1)), vec, vec]


def _swiglu(xb, w_in_ref, w_out_ref):
    acc = None
    for lo, hi in FF_CHUNKS:
        ha = _dot(xb, w_in_ref[:, lo:hi])
        hb = _dot(xb, w_in_ref[:, D_FF + lo:D_FF + hi])
        act = (ha * (1.0 / (1.0 + jnp.exp(-ha))) * hb).astype(BF16)
        part = _dot(act, w_out_ref[lo:hi, :])
        acc = part if acc is None else acc + part
    return acc


def _dense_ffn_kernel(s_ref, c_ref, xin_ref, wo_ref, ga_ref, ba_ref, w_in_ref, w_out_ref,
                      g_ref, b_ref, ein_ref, eout_ref, o_ref, ein_bf_ref, eout_bf_ref):
    x = _attn_out_ln(s_ref, c_ref, xin_ref, wo_ref, ga_ref, ba_ref)
    ffn = _swiglu(x.astype(BF16), w_in_ref.at[0], w_out_ref.at[0])
    o_ref[...] = _layer_norm(DN_ALPHA * x + ffn, g_ref[...], b_ref[...])
    ein_bf_ref[...] = ein_ref[...].astype(BF16)
    eout_bf_ref[...] = eout_ref[...].astype(BF16)


def _dense_layer_tail(self_out, cross_out, xf, w_o, depth_idx, g_attn, b_attn,
                      w_in, w_out, layer, g, b, w_exp_in, w_exp_out):
    t = xf.shape[0]
    tm = min(t, FFN_TM)
    steps = t // tm
    n_layers = w_exp_in.shape[0]
    ein = w_exp_in.reshape(n_layers * N_EXPERTS * D_MODEL, 2 * D_FF)
    eout = w_exp_out.reshape(n_layers * N_EXPERTS * D_FF, D_MODEL)
    rin, rout = N_EXPERTS * D_MODEL // steps, N_EXPERTS * D_FF // steps
    assert rin * steps == N_EXPERTS * D_MODEL and rout * steps == N_EXPERTS * D_FF
    assert rin % 16 == 0 and rout % 16 == 0
    row = pl.BlockSpec((tm, D_MODEL), lambda i: (i, 0))
    const = lambda r, c: pl.BlockSpec((r, c), lambda i: (0, 0))
    once = pl.Buffered(1)
    y, ein_bf, eout_bf = pl.pallas_call(
        _dense_ffn_kernel,
        out_shape=(jax.ShapeDtypeStruct((t, D_MODEL), F32),
                   jax.ShapeDtypeStruct((N_EXPERTS * D_MODEL, 2 * D_FF), BF16),
                   jax.ShapeDtypeStruct((N_EXPERTS * D_FF, D_MODEL), BF16)),
        grid=(steps,),
        in_specs=_attn_out_specs(tm, depth_idx) + [
                  pl.BlockSpec((1, D_MODEL, 2 * D_FF), lambda i: (layer, 0, 0), pipeline_mode=once),
                  pl.BlockSpec((1, D_FF, D_MODEL), lambda i: (layer, 0, 0), pipeline_mode=once),
                  const(1, D_MODEL), const(1, D_MODEL),
                  pl.BlockSpec((rin, 2 * D_FF), lambda i: (layer * steps + i, 0)),
                  pl.BlockSpec((rout, D_MODEL), lambda i: (layer * steps + i, 0))],
        out_specs=(row, pl.BlockSpec((rin, 2 * D_FF), lambda i: (i, 0)),
                   pl.BlockSpec((rout, D_MODEL), lambda i: (i, 0))),
        compiler_params=_cparams("parallel"),
        name="dense_ffn_ln",
    )(self_out, cross_out, xf, w_o, g_attn, b_attn, w_in, w_out, g, b, ein, eout)
    return (y, ein_bf.reshape(N_EXPERTS, D_MODEL, 2 * D_FF),
            eout_bf.reshape(N_EXPERTS, D_FF, D_MODEL))


def _expert_kernel(be_ref, bx_ref, xs_ref, w_in_ref, w_out_ref, ys_ref):
    del be_ref
    i = pl.program_id(0)
    used = bx_ref[i] == i

    @pl.when(used)
    def _():
        ys_ref[...] = _swiglu(xs_ref[...].astype(BF16), w_in_ref.at[0], w_out_ref.at[0])

    @pl.when(jnp.logical_not(used))
    def _():
        ys_ref[...] = jnp.zeros_like(ys_ref)


def _expert_ffn(xs, w_in, w_out, block_expert, block_src):
    n_slots = xs.shape[0]
    return pl.pallas_call(
        _expert_kernel,
        out_shape=jax.ShapeDtypeStruct((n_slots, D_MODEL), F32),
        grid_spec=pltpu.PrefetchScalarGridSpec(
            num_scalar_prefetch=2,
            grid=(n_slots // MOE_TM,),
            in_specs=[pl.BlockSpec((MOE_TM, D_MODEL), lambda i, be, bx: (bx[i], 0)),
                      pl.BlockSpec((1, D_MODEL, 2 * D_FF), lambda i, be, bx: (be[i], 0, 0)),
                      pl.BlockSpec((1, D_FF, D_MODEL), lambda i, be, bx: (be[i], 0, 0))],
            out_specs=pl.BlockSpec((MOE_TM, D_MODEL), lambda i, be, bx: (i, 0))),
        compiler_params=_cparams("arbitrary"),
        name="expert_ffn",
    )(block_expert, block_src, xs, w_in, w_out)


ROUTE_E1, ROUTE_E2, ROUTE_R1, ROUTE_R2, ROUTE_G1, ROUTE_G2 = range(6)


def _router_kernel(s_ref, c_ref, xin_ref, wo_ref, ga_ref, ba_ref, w_ref, b_ref,
                   x_ref, route_ref, count_ref, carry_ref):
    @pl.when(pl.program_id(0) == 0)
    def _():
        carry_ref[...] = jnp.zeros_like(carry_ref)

    x = _attn_out_ln(s_ref, c_ref, xin_ref, wo_ref, ga_ref, ba_ref)
    x_ref[...] = x
    xs, ws = _split3(x), _split3(w_ref[...])
    logits = (_dot(xs[1], ws[0]) + _dot(xs[0], ws[1])) + _dot(xs[0], ws[0])
    tm = logits.shape[0]
    lane = lax.broadcasted_iota(jnp.int32, (tm, LANES), 1).astype(F32)
    logits = jnp.where(lane < N_EXPERTS, logits + b_ref[...], -jnp.inf)
    m1 = jnp.max(logits, axis=-1, keepdims=True)
    e1 = jnp.min(jnp.where(logits == m1, lane, float(LANES)), axis=-1, keepdims=True)
    rest = jnp.where(lane == e1, -jnp.inf, logits)
    m2 = jnp.max(rest, axis=-1, keepdims=True)
    e2 = jnp.min(jnp.where(rest == m2, lane, float(LANES)), axis=-1, keepdims=True)
    ex = jnp.exp(m2 - m1)
    g1 = 1.0 / (1.0 + ex)
    g2 = ex * g1
    sel1, sel2 = lane == e1, lane == e2
    onehot = jnp.where(sel1 | sel2, 1.0, 0.0)
    tri = (lax.broadcasted_iota(jnp.int32, (tm, tm), 0)
           > lax.broadcasted_iota(jnp.int32, (tm, tm), 1)).astype(F32).astype(BF16)
    rank = _dot(tri, onehot.astype(BF16)) + carry_ref[...]
    r1 = jnp.sum(jnp.where(sel1, rank, 0.0), axis=-1, keepdims=True)
    r2 = jnp.sum(jnp.where(sel2, rank, 0.0), axis=-1, keepdims=True)
    total = carry_ref[...] + jnp.sum(onehot, axis=0, keepdims=True)
    carry_ref[...] = total
    count_ref[...] = total
    route = jnp.zeros((tm, LANES), F32)
    for ln, val in ((ROUTE_E1, e1), (ROUTE_E2, e2), (ROUTE_R1, r1), (ROUTE_R2, r2),
                    (ROUTE_G1, g1), (ROUTE_G2, g2)):
        route = jnp.where(lane == float(ln), val, route)
    route_ref[...] = route


def _attn_out_router(self_out, cross_out, xf, w_o, depth_idx, g_attn, b_attn, w_pad, b_pad):
    t = xf.shape[0]
    tm = min(t, ROUTER_TM)
    return pl.pallas_call(
        _router_kernel,
        out_shape=(jax.ShapeDtypeStruct((t, D_MODEL), F32), jax.ShapeDtypeStruct((t, LANES), F32),
                   jax.ShapeDtypeStruct((1, LANES), F32)),
        grid=(t // tm,),
        in_specs=_attn_out_specs(tm, depth_idx) + [
                  pl.BlockSpec((D_MODEL, LANES), lambda i: (0, 0)),
                  pl.BlockSpec((1, LANES), lambda i: (0, 0))],
        out_specs=(pl.BlockSpec((tm, D_MODEL), lambda i: (i, 0)),
                   pl.BlockSpec((tm, LANES), lambda i: (i, 0)),
                   pl.BlockSpec((1, LANES), lambda i: (0, 0))),
        scratch_shapes=[pltpu.VMEM((1, LANES), F32)],
        compiler_params=_cparams("arbitrary"),
        name="router",
    )(self_out, cross_out, xf, w_o, g_attn, b_attn, w_pad, b_pad)


def _start_rows(n, make_copy):
    for r in range(n):
        make_copy(r, 0).start(priority=0)
        make_copy(r, 1).start(priority=1)


def _wait_rows(n, make_copy):
    for r in range(n):
        make_copy(r, 0).wait()
        make_copy(r, 1).wait()


def _for_slot(slot, fn):
    for static_slot in range(2):
        @pl.when(slot == static_slot)
        def _():
            fn(static_slot)


def _dispatch_kernel(dest_ref, pend_ref, x_ref, xs_hbm, zero_ref, stage_ref, sems, zsem):
    step, n_steps = pl.program_id(0), pl.num_programs(0)

    @pl.when(pl.program_id(0) == 0)
    def _():
        zero_ref[...] = jnp.zeros_like(zero_ref)

        def zero_copy(e):
            start = pl.multiple_of(pend_ref[e + 1] - MOE_TM, MOE_TM)
            return pltpu.make_async_copy(zero_ref, xs_hbm.at[pl.ds(start, MOE_TM)], zsem)

        def tail_copy(blk):
            return pltpu.make_async_copy(
                zero_ref, xs_hbm.at[pl.ds(pl.multiple_of(blk * MOE_TM, MOE_TM), MOE_TM)], zsem)

        first_tail = pend_ref[N_EXPERTS] // MOE_TM
        n_blocks = xs_hbm.shape[0] // MOE_TM
        for e in range(N_EXPERTS):
            @pl.when(pend_ref[e + 1] > pend_ref[e])
            def _():
                zero_copy(e).start()
        lax.fori_loop(first_tail, n_blocks, lambda blk, c: (tail_copy(blk).start(), c)[1], 0)
        for e in range(N_EXPERTS):
            @pl.when(pend_ref[e + 1] > pend_ref[e])
            def _():
                zero_copy(e).wait()
        lax.fori_loop(first_tail, n_blocks, lambda blk, c: (tail_copy(blk).wait(), c)[1], 0)

    def copies(of_step, slot):
        base = of_step * ROW_TM

        def make_copy(r, k):
            return pltpu.make_async_copy(stage_ref.at[slot, pl.ds(r, 1)],
                                         xs_hbm.at[pl.ds(dest_ref[2 * (base + r) + k], 1)],
                                         sems.at[slot])
        return make_copy

    def issue(slot):
        stage_ref[slot] = x_ref[...]
        _start_rows(ROW_TM, copies(step, slot))

    _for_slot(step % 2, issue)

    @pl.when(step > 0)
    def _():
        _for_slot((step - 1) % 2, lambda slot: _wait_rows(ROW_TM, copies(step - 1, slot)))

    @pl.when(step == n_steps - 1)
    def _():
        _for_slot(step % 2, lambda slot: _wait_rows(ROW_TM, copies(step, slot)))


def _dispatch(xf, dest, seg_ends, n_slots):
    t = xf.shape[0]
    return pl.pallas_call(
        _dispatch_kernel,
        out_shape=jax.ShapeDtypeStruct((n_slots, D_MODEL), F32),
        grid_spec=pltpu.PrefetchScalarGridSpec(
            num_scalar_prefetch=2,
            grid=(t // ROW_TM,),
            in_specs=[pl.BlockSpec((ROW_TM, D_MODEL), lambda i, d, p: (i, 0))],
            out_specs=pl.BlockSpec(memory_space=pl.ANY),
            scratch_shapes=[pltpu.VMEM((MOE_TM, D_MODEL), F32),
                            pltpu.VMEM((2, ROW_TM, D_MODEL), F32),
                            pltpu.SemaphoreType.DMA((2,)), pltpu.SemaphoreType.DMA(())]),
        compiler_params=_cparams("arbitrary"),
        name="moe_dispatch",
    )(dest, seg_ends, xf)


def _combine_kernel(dest_ref, x_ref, route_ref, g_ref, b_ref, ys_hbm, o_ref, ya, yb, sems):
    step, n_steps = pl.program_id(0), pl.num_programs(0)

    def copies(of_step, slot):
        base = of_step * ROW_TM

        def make_copy(r, k):
            dst = ya if k == 0 else yb
            return pltpu.make_async_copy(ys_hbm.at[pl.ds(dest_ref[2 * (base + r) + k], 1)],
                                         dst.at[slot, pl.ds(r, 1)], sems.at[slot])
        return make_copy

    @pl.when(step == 0)
    def _():
        _start_rows(ROW_TM, copies(0, 0))

    @pl.when(step + 1 < n_steps)
    def _():
        _for_slot((step + 1) % 2, lambda slot: _start_rows(ROW_TM, copies(step + 1, slot)))

    def finish(slot):
        _wait_rows(ROW_TM, copies(step, slot))
        route = route_ref[...]
        g1 = route[:, ROUTE_G1:ROUTE_G1 + 1]
        g2 = route[:, ROUTE_G2:ROUTE_G2 + 1]
        y = ya[slot] * g1 + yb[slot] * g2
        o_ref[...] = _layer_norm(DN_ALPHA * x_ref[...] + y, g_ref[...], b_ref[...])

    _for_slot(step % 2, finish)


def _combine_ln(xf, route, ys, dest, g, b):
    t = xf.shape[0]
    row = lambda n: pl.BlockSpec((ROW_TM, n), lambda i, d: (i, 0))
    const = pl.BlockSpec((1, D_MODEL), lambda i, d: (0, 0))
    return pl.pallas_call(
        _combine_kernel,
        out_shape=jax.ShapeDtypeStruct((t, D_MODEL), F32),
        grid_spec=pltpu.PrefetchScalarGridSpec(
            num_scalar_prefetch=1,
            grid=(t // ROW_TM,),
            in_specs=[row(D_MODEL), row(LANES), const, const, pl.BlockSpec(memory_space=pl.ANY)],
            out_specs=row(D_MODEL),
            scratch_shapes=[pltpu.VMEM((2, ROW_TM, D_MODEL), F32),
                            pltpu.VMEM((2, ROW_TM, D_MODEL), F32),
                            pltpu.SemaphoreType.DMA((2,))]),
        compiler_params=_cparams("arbitrary"),
        name="moe_combine_ln",
    )(dest, xf, route, g, b, ys)


def _moe_layer_tail(self_out, cross_out, xf, w_o, depth_idx, g_attn, b_attn,
                    w_router, b_router, w_in, w_out, g, b):
    t = xf.shape[0]
    w_pad = jnp.zeros((D_MODEL, LANES), F32).at[:, :N_EXPERTS].set(w_router)
    b_pad = jnp.zeros((1, LANES), F32).at[0, :N_EXPERTS].set(b_router)
    xf, route, counts = _attn_out_router(self_out, cross_out, xf, w_o, depth_idx, g_attn, b_attn,
                                         w_pad, b_pad)
    counts = counts[0, :N_EXPERTS].astype(jnp.int32)
    pcounts = (counts + MOE_TM - 1) // MOE_TM * MOE_TM
    pends = jnp.cumsum(pcounts)
    pstarts = pends - pcounts
    e = route[:, ROUTE_E1:ROUTE_E2 + 1].astype(jnp.int32)
    r = route[:, ROUTE_R1:ROUTE_R2 + 1].astype(jnp.int32)
    seg_start = sum(jnp.where(e == k, pstarts[k], 0) for k in range(N_EXPERTS))
    dest = (seg_start + r).reshape(-1)
    n_blocks = 2 * t // MOE_TM + N_EXPERTS
    blk_start = jnp.arange(n_blocks, dtype=jnp.int32) * MOE_TM
    block_expert = jnp.minimum(jnp.sum(blk_start[:, None] >= pends[None, :], axis=1),
                               N_EXPERTS - 1).astype(jnp.int32)
    n_used = pends[-1] // MOE_TM
    block_src = jnp.minimum(jnp.arange(n_blocks, dtype=jnp.int32), n_used - 1)
    seg_ends = jnp.concatenate([jnp.zeros((1,), jnp.int32), pends.astype(jnp.int32)])
    xs = _dispatch(xf, dest, seg_ends, n_blocks * MOE_TM)
    ys = _expert_ffn(xs, w_in, w_out, block_expert, block_src)
    return _combine_ln(xf, route, ys, dest, g, b)


def _dup_heads(w):
    d, n = w.shape
    w = w.reshape(d, n // HEAD_DIM, 1, HEAD_DIM)
    return jnp.broadcast_to(w, (d, n // HEAD_DIM, 2, HEAD_DIM)).reshape(d, 2 * n)


def kernel(x, mem, positions, w_in_swa, attn_sinks, w_in_fox, b_forget, w_mem_kv, w_out,
           ln_attn_g, ln_attn_b, w_ffn_in, w_ffn_out, w_router, b_router, w_exp_in,
           w_exp_out, ln_ffn_g, ln_ffn_b):
    b, s, d = x.shape
    t = b * s
    hkv = SWA_KV_HEADS * HEAD_DIM
    cos_t, sin_t = _rope_tables(positions)
    mem2d = mem.reshape(b * MEM_LEN, d)
    xf = x.reshape(t, d)
    w_out_b = w_out.astype(BF16)
    w_ffn_in_b, w_ffn_out_b = w_ffn_in.astype(BF16), w_ffn_out.astype(BF16)
    exp_w = None
    for i in range(DEPTH):
        j = i // 2
        kvm = _mem_kv(mem2d, w_mem_kv[i].astype(BF16)).reshape(b, MEM_LEN, 2 * HM)
        if i % 2 == 0:
            w = w_in_swa[j]
            w = jnp.concatenate([w[:, :HQ], _dup_heads(w[:, HQ:HQ + hkv]),
                                 _dup_heads(w[:, HQ + hkv:HQ + 2 * hkv]), w[:, HQ + 2 * hkv:]],
                                axis=1).astype(BF16)
            q, kk, vv, qc = _swa_proj(xf, w, cos_t, sin_t)
            self_out = _swa_attn(q.reshape(b, s, HQ), kk.reshape(b, s, HKV2),
                                 vv.reshape(b, s, HKV2), attn_sinks[j])
            qc = qc.reshape(b, s, HM)
        else:
            w = jnp.pad(w_in_fox[j], ((0, 0), (0, FOX_COLS - w_in_fox.shape[-1]))).astype(BF16)
            bf_row = jnp.pad(b_forget[j], (0, LANES - SELF_HEADS))[None, :]
            q, k, v, qc, dcum = _fox_proj(xf.reshape(b, s, d), w, bf_row)
            self_out = _fox_attn(q, k, v, dcum)
        cross_out = _mem_attn(qc, kvm)
        attn = (self_out.reshape(t, HQ), cross_out.reshape(t, HM), xf, w_out_b, i,
                ln_attn_g[i][None, :], ln_attn_b[i][None, :])
        g, bb = ln_ffn_g[i][None, :], ln_ffn_b[i][None, :]
        if i % 2 == 0:
            xf, *exp_w = _dense_layer_tail(*attn, w_ffn_in_b, w_ffn_out_b, j, g, bb,
                                           w_exp_in, w_exp_out)
        else:
            xf = _moe_layer_tail(*attn, w_router[j], b_router[j], exp_w[0], exp_w[1], g, bb)
    return xf.reshape(b, s, d)
```

```python
import functools

import jax
import jax.numpy as jnp
from jax import lax
from jax.experimental import pallas as pl
from jax.experimental.pallas import tpu as pltpu

D_MODEL = 1024
DEPTH = 4
CHUNK = 64
HEAD_DIM = 64
MEM_HEADS = 4
SELF_HEADS = 12
SWA_KV_HEADS = 3
SWA_GROUP = SELF_HEADS // SWA_KV_HEADS
WINDOW_CHUNKS = 2
ROT_DIM = HEAD_DIM // 4
ROT_HALF = ROT_DIM // 2
ROPE_THETA = 500000.0
MEM_LEN = 256
D_FF = 2816
N_EXPERTS = 8
LN_EPS = 1e-5
NEG = -1e30
ATTN_SCALE = HEAD_DIM ** -0.5
LOG2E = 1.4426950408889634
DN_ALPHA = (2.0 * DEPTH) ** 0.25

HQ = SELF_HEADS * HEAD_DIM
HM = MEM_HEADS * HEAD_DIM
HKV2 = 2 * SWA_KV_HEADS * HEAD_DIM
SWA_COLS = HQ + 2 * HKV2 + HM
FOX_COLS = 3 * HQ + HM + 128

LANES = 128
VMEM_LIMIT_BYTES = 56 * 1024 * 1024

PROJ_TM = 512
FOX_PROJ_TM = 512
SWA_TQ = 256
SWA_SUB = 128
FOX_TQ = 2048
FOX_TK = 512
FFN_TM = 256
MXU_DIM = 256
FF_CHUNKS = ((0, 6 * MXU_DIM), (6 * MXU_DIM, D_FF))
ROUTER_TM = 512
MOE_TM = 256
ROW_TM = 256

F32 = jnp.float32
BF16 = jnp.bfloat16


def _cparams(*sem):
    return pltpu.CompilerParams(dimension_semantics=sem, vmem_limit_bytes=VMEM_LIMIT_BYTES)


def _layer_norm(y, g, b):
    mu = jnp.mean(y, axis=-1, keepdims=True)
    d = y - mu
    var = jnp.mean(d * d, axis=-1, keepdims=True)
    return d * lax.rsqrt(var + LN_EPS) * g + b


def _split3(a):
    hi = a.astype(BF16)
    r1 = a - hi.astype(F32)
    mid = r1.astype(BF16)
    lo = (r1 - mid.astype(F32)).astype(BF16)
    return hi, mid, lo


def _dot(a, b):
    return jnp.dot(a, b, preferred_element_type=F32)


def _dot_nt(a, b):
    return lax.dot_general(a, b, (((1,), (1,)), ((), ())), preferred_element_type=F32)


def _rope_table_kernel(pos_ref, freq_ref, cos_ref, sin_ref):
    ang = pos_ref[...].astype(F32) * freq_ref[...]
    sub = lax.broadcasted_iota(jnp.int32, ang.shape, 1) & (HEAD_DIM - 1)
    c, s = jnp.cos(ang), jnp.sin(ang)
    cos_ref[...] = jnp.where(sub < ROT_DIM, c, 1.0)
    sin_ref[...] = jnp.where(sub < ROT_HALF, -s, jnp.where(sub < ROT_DIM, s, 0.0))


def _rope_tables(positions):
    t = positions.size
    tm = min(t, 2048)
    inv_freq = ROPE_THETA ** (-jnp.arange(0, ROT_DIM, 2, dtype=F32) / ROT_DIM)
    lane = jnp.arange(LANES) % HEAD_DIM % ROT_HALF
    freq = inv_freq[lane][None, :]
    out = jax.ShapeDtypeStruct((t, LANES), F32)
    return pl.pallas_call(
        _rope_table_kernel,
        out_shape=(out, out),
        grid=(t // tm,),
        in_specs=[pl.BlockSpec((tm, 1), lambda i: (i, 0)),
                  pl.BlockSpec((1, LANES), lambda i: (0, 0))],
        out_specs=(pl.BlockSpec((tm, LANES), lambda i: (i, 0)),
                   pl.BlockSpec((tm, LANES), lambda i: (i, 0))),
        compiler_params=_cparams("parallel"),
        name="rope_tables",
    )(positions.reshape(t, 1), freq)


def _mem_kv_kernel(m_ref, w_ref, o_ref):
    o_ref[...] = _dot(m_ref[...].astype(BF16), w_ref[...]).astype(BF16)


def _mem_kv(mem2d, w):
    n = mem2d.shape[0]
    tm = min(n, PROJ_TM)
    return pl.pallas_call(
        _mem_kv_kernel,
        out_shape=jax.ShapeDtypeStruct((n, 2 * HM), BF16),
        grid=(n // tm,),
        in_specs=[pl.BlockSpec((tm, D_MODEL), lambda i: (i, 0)),
                  pl.BlockSpec((D_MODEL, 2 * HM), lambda i: (0, 0))],
        out_specs=pl.BlockSpec((tm, 2 * HM), lambda i: (i, 0)),
        compiler_params=_cparams("parallel"),
        name="mem_kv_proj",
    )(mem2d, w)


def _swa_proj_kernel(x_ref, w_ref, cos_ref, sin_ref, kvm_ref, q_ref, kk_ref, vv_ref, cross_ref):
    acc = _dot(x_ref[...].astype(BF16), w_ref[...])
    cos, sin = cos_ref[...], sin_ref[...]
    first = (lax.broadcasted_iota(jnp.int32, cos.shape, 1) & (HEAD_DIM - 1)) < ROT_HALF

    def rope(t):
        fwd = pltpu.roll(t, LANES - ROT_HALF, axis=1)
        bwd = pltpu.roll(t, ROT_HALF, axis=1)
        return t * cos + jnp.where(first, fwd, bwd) * sin

    for g in range(HQ // LANES):
        sl = slice(g * LANES, (g + 1) * LANES)
        q_ref[:, sl] = (rope(acc[:, sl]) * ATTN_SCALE).astype(BF16)
    for g in range(HKV2 // LANES):
        src = slice(HQ + g * LANES, HQ + (g + 1) * LANES)
        kk_ref[:, g * LANES:(g + 1) * LANES] = rope(acc[:, src]).astype(BF16)
    vv_ref[...] = acc[:, HQ + HKV2:HQ + 2 * HKV2].astype(BF16)
    _memory_attention(acc[:, HQ + 2 * HKV2:] * ATTN_SCALE, kvm_ref.at[0], cross_ref)


def _swa_proj(xf, w, cos_t, sin_t, kvm):
    t = xf.shape[0]
    tm = min(t // kvm.shape[0], PROJ_TM)
    steps_per_batch = t // kvm.shape[0] // tm
    row = lambda n: pl.BlockSpec((tm, n), lambda i: (i, 0))
    return pl.pallas_call(
        _swa_proj_kernel,
        out_shape=(jax.ShapeDtypeStruct((t, HQ), BF16), jax.ShapeDtypeStruct((t, HKV2), BF16),
                   jax.ShapeDtypeStruct((t, HKV2), BF16), jax.ShapeDtypeStruct((t, HM), BF16)),
        grid=(t // tm,),
        in_specs=[row(D_MODEL), pl.BlockSpec((D_MODEL, SWA_COLS), lambda i: (0, 0)),
                  row(LANES), row(LANES),
                  pl.BlockSpec((1, MEM_LEN, 2 * HM), lambda i: (i // steps_per_batch, 0, 0))],
        out_specs=(row(HQ), row(HKV2), row(HKV2), row(HM)),
        compiler_params=_cparams("parallel"),
        name="swa_proj",
    )(xf, w, cos_t, sin_t, kvm)


def _swa_attn_kernel(sink_ref, q_ref, kp_ref, kc_ref, vp_ref, vc_ref, o_ref):
    first_tile = pl.program_id(1) == 0
    half = SWA_TQ - SWA_SUB
    lane = lax.broadcasted_iota(jnp.int32, (SWA_SUB, LANES), 1)
    shift = CHUNK.bit_length() - 1
    rc = lax.broadcasted_iota(jnp.int32, (SWA_SUB, 2 * SWA_SUB), 0) >> shift
    cc = lax.broadcasted_iota(jnp.int32, (SWA_SUB, 2 * SWA_SUB), 1) >> shift
    band = (cc >= rc) & (cc <= rc + WINDOW_CHUNKS)
    band_prev = band & ((cc >= SWA_SUB // CHUNK) | jnp.logical_not(first_tile))
    for sub in range(SWA_TQ // SWA_SUB):
        rows = slice(sub * SWA_SUB, (sub + 1) * SWA_SUB)
        mask = band_prev if sub == 0 else band
        for h in range(SWA_KV_HEADS):
            cols = slice(h * LANES, (h + 1) * LANES)
            if sub == 0:
                kcat = jnp.concatenate([kp_ref[0, half:, cols], kc_ref[0, :SWA_SUB, cols]], axis=0)
                vcat = jnp.concatenate([vp_ref[0, half:, cols], vc_ref[0, :SWA_SUB, cols]], axis=0)
            else:
                kcat = kc_ref[0, :, cols]
                vcat = vc_ref[0, :, cols]
            for pair in range(SWA_GROUP // 2):
                g = h * (SWA_GROUP // 2) + pair
                qcols = slice(g * LANES, (g + 1) * LANES)
                qf = q_ref[0, rows, qcols].astype(F32)
                outs = []
                for a in range(2):
                    keep = (lane < HEAD_DIM) if a == 0 else (lane >= HEAD_DIM)
                    qa = jnp.where(keep, qf, 0.0).astype(BF16)
                    s = jnp.where(mask, _dot_nt(qa, kcat), NEG)
                    sink = sink_ref[2 * g + a]
                    m = jnp.maximum(jnp.max(s, axis=-1, keepdims=True), sink)
                    p = jnp.exp(s - m)
                    den = jnp.sum(p, axis=-1, keepdims=True) + jnp.exp(sink - m)
                    probs = (p * (1.0 / den)).astype(BF16)
                    outs.append(_dot(probs, vcat))
                o = jnp.where(lane < HEAD_DIM, outs[0], outs[1])
                o_ref[0, rows, qcols] = o.astype(BF16)


def _swa_attn(q, kk, vv, sinks):
    b, s, _ = q.shape
    nt = s // SWA_TQ
    cur = lambda n: pl.BlockSpec((1, SWA_TQ, n), lambda bi, i: (bi, i, 0))
    prev = lambda n: pl.BlockSpec((1, SWA_TQ, n), lambda bi, i: (bi, jnp.maximum(i - 1, 0), 0))
    return pl.pallas_call(
        _swa_attn_kernel,
        out_shape=jax.ShapeDtypeStruct((b, s, HQ), BF16),
        grid=(b, nt),
        in_specs=[pl.BlockSpec(memory_space=pltpu.SMEM),
                  cur(HQ), prev(HKV2), cur(HKV2), prev(HKV2), cur(HKV2)],
        out_specs=cur(HQ),
        compiler_params=_cparams("parallel", "parallel"),
        name="swa_attn",
    )(sinks, q, kk, kk, vv, vv)


def _fox_proj_kernel(x_ref, w_ref, bf_ref, kvm_ref, q_ref, k_ref, v_ref, cross_ref, d_ref,
                     carry_ref):
    @pl.when(pl.program_id(1) == 0)
    def _():
        carry_ref[...] = jnp.zeros_like(carry_ref)

    acc = _dot(x_ref[0].astype(BF16), w_ref[...])
    q_ref[0] = (acc[:, :HQ] * (ATTN_SCALE * LOG2E)).astype(BF16)
    k_ref[0] = acc[:, HQ:2 * HQ].astype(BF16)
    v_ref[0] = acc[:, 2 * HQ:3 * HQ].astype(BF16)
    _memory_attention(acc[:, 3 * HQ:3 * HQ + HM] * ATTN_SCALE, kvm_ref.at[0], cross_ref.at[0])
    z = acc[:, 3 * HQ + HM:] + bf_ref[...]
    log_f = jnp.minimum(z, 0.0) - jnp.log(1.0 + jnp.exp(-jnp.abs(z)))
    tm = z.shape[0]
    tri = (lax.broadcasted_iota(jnp.int32, (tm, tm), 0)
           >= lax.broadcasted_iota(jnp.int32, (tm, tm), 1)).astype(F32).astype(BF16)
    hi, mid, lo = _split3(log_f)
    cum = (_dot(tri, lo) + _dot(tri, mid)) + _dot(tri, hi) + carry_ref[...]
    d_ref[0] = cum
    carry_ref[...] = cum[tm - 1:tm, :]


def _fox_proj(x3, w, bf_row, kvm):
    b, s, _ = x3.shape
    tm = min(s, FOX_PROJ_TM)
    row = lambda n: pl.BlockSpec((1, tm, n), lambda bi, i: (bi, i, 0))
    sds = lambda n, dt: jax.ShapeDtypeStruct((b, s, n), dt)
    return pl.pallas_call(
        _fox_proj_kernel,
        out_shape=(sds(HQ, BF16), sds(HQ, BF16), sds(HQ, BF16), sds(HM, BF16), sds(LANES, F32)),
        grid=(b, s // tm),
        in_specs=[row(D_MODEL), pl.BlockSpec((D_MODEL, FOX_COLS), lambda bi, i: (0, 0)),
                  pl.BlockSpec((1, LANES), lambda bi, i: (0, 0)),
                  pl.BlockSpec((1, MEM_LEN, 2 * HM), lambda bi, i: (bi, 0, 0))],
        out_specs=(row(HQ), row(HQ), row(HQ), row(HM), row(LANES)),
        scratch_shapes=[pltpu.VMEM((1, LANES), F32)],
        compiler_params=_cparams("arbitrary", "arbitrary"),
        name="fox_proj",
    )(x3, w, bf_row, kvm)


def _fox_attn_kernel(q_ref, k_ref, v_ref, d_ref, o_ref, kaug_ref, qaug_ref, vt_ref, qt_ref):
    s_len = q_ref.shape[1]
    tq, tk = qt_ref.shape[3], vt_ref.shape[2]
    pair = pl.program_id(1)
    lane = lax.broadcasted_iota(jnp.int32, (s_len, LANES), 1)
    prow = lax.broadcasted_iota(jnp.int32, (3 * LANES, 2 * LANES), 0)
    pcol = lax.broadcasted_iota(jnp.int32, (3 * LANES, 2 * LANES), 1)
    term = prow >> 7
    kf = k_ref[0].astype(F32)
    dsplit = jnp.concatenate(_split3(d_ref[0] * LOG2E), axis=1)
    for a in range(2):
        head = 2 * pair + a
        from_head = (prow & (LANES - 1)) == head
        place = (jnp.where(from_head & (pcol == term), 1.0, 0.0)
                 + jnp.where(from_head & (pcol == LANES + 3 + term), -1.0, 0.0)).astype(BF16)
        extra = _dot(dsplit, place)
        q_extra = extra[:, :LANES] + jnp.where((lane >= 3) & (lane < 6), 1.0, 0.0)
        k_extra = extra[:, LANES:] + jnp.where(lane < 3, 1.0, 0.0)
        keep = (lane < HEAD_DIM) if a == 0 else (lane >= HEAD_DIM)
        kaug_ref[a, :, :LANES] = jnp.where(keep, kf, 0.0).astype(BF16)
        kaug_ref[a, :, LANES:] = k_extra.astype(BF16)
        qaug_ref[a, :, :LANES] = q_ref[0]
        qaug_ref[a, :, LANES:] = q_extra.astype(BF16)
    for j in range(s_len // tk):
        vt_ref[j] = v_ref[0, j * tk:(j + 1) * tk, :].astype(F32).T.astype(BF16)

    nq = s_len // tq
    for qi in range(nq):
        for a in range(2):
            qt_ref[a, qi] = qaug_ref[a, qi * tq:(qi + 1) * tq, :].astype(F32).T.astype(BF16)

    def tile(kb, qi, a, state):
        q_off = max(0, kb * tk - qi * tq)
        width = tq - q_off
        masked = (kb + 1) * tk - 1 > qi * tq + q_off
        m0, l0, acc0 = state
        m, l, acc = m0[:, q_off:], l0[:, q_off:], acc0[:, q_off:]
        s = _dot(kaug_ref[a, kb * tk:(kb + 1) * tk, :], qt_ref[a, qi, :, q_off:])
        if masked:
            n_mask = min(width, pl.cdiv((kb + 1) * tk - (qi * tq + q_off), LANES) * LANES)
            k_pos = kb * tk + lax.broadcasted_iota(jnp.int32, (tk, n_mask), 0)
            q_pos = qi * tq + q_off + lax.broadcasted_iota(jnp.int32, (tk, n_mask), 1)
            head_cols = jnp.where(k_pos <= q_pos, s[:, :n_mask], NEG)
            s = head_cols if n_mask == width else jnp.concatenate([head_cols, s[:, n_mask:]], axis=1)
        m_new = jnp.maximum(m, jnp.max(s, axis=0, keepdims=True))
        alpha = jnp.exp2(m - m_new)
        p = jnp.exp2(s - m_new)
        l = alpha * l + jnp.sum(p, axis=0, keepdims=True)
        acc = alpha * acc + _dot(vt_ref[kb], p.astype(BF16))
        if q_off:
            m_new = jnp.concatenate([m0[:, :q_off], m_new], axis=1)
            l = jnp.concatenate([l0[:, :q_off], l], axis=1)
            acc = jnp.concatenate([acc0[:, :q_off], acc], axis=1)
        return m_new, l, acc

    state = {(qi, a): (jnp.full((1, tq), NEG, F32), jnp.zeros((1, tq), F32),
                       jnp.zeros((LANES, tq), F32)) for qi in range(nq) for a in range(2)}
    for kb in range(s_len // tk):
        for qi in range(kb * tk // tq, nq):
            for a in range(2):
                state[qi, a] = tile(kb, qi, a, state[qi, a])
    orow = lax.broadcasted_iota(jnp.int32, (LANES, tq), 0)
    for qi in range(nq):
        (_, l0, acc0), (_, l1, acc1) = state[qi, 0], state[qi, 1]
        o_t = jnp.where(orow < HEAD_DIM, acc0 * (1.0 / l0), acc1 * (1.0 / l1))
        o_ref[0, qi * tq:(qi + 1) * tq, :] = o_t.T.astype(BF16)


def _fox_attn(q, k, v, dcum):
    b, s, _ = q.shape
    tq, tk = min(FOX_TQ, s), min(FOX_TK, s)
    seq = lambda: pl.BlockSpec((1, s, LANES), lambda bi, j: (bi, 0, j))
    return pl.pallas_call(
        _fox_attn_kernel,
        out_shape=jax.ShapeDtypeStruct((b, s, HQ), BF16),
        grid=(b, HQ // LANES),
        in_specs=[seq(), seq(), seq(), pl.BlockSpec((1, s, LANES), lambda bi, j: (bi, 0, 0))],
        out_specs=seq(),
        scratch_shapes=[pltpu.VMEM((2, s, 2 * LANES), BF16), pltpu.VMEM((2, s, 2 * LANES), BF16),
                        pltpu.VMEM((s // tk, LANES, tk), BF16),
                        pltpu.VMEM((2, s // tq, 2 * LANES, tq), BF16)],
        compiler_params=_cparams("parallel", "parallel"),
        name="fox_attn",
    )(q, k, v, dcum)


def _memory_attention(qc, kv_ref, out_ref):
    lane = lax.broadcasted_iota(jnp.int32, (qc.shape[0], LANES), 1)
    for pair in range(HM // LANES):
        cols = slice(pair * LANES, (pair + 1) * LANES)
        kblk = kv_ref[:, cols]
        vblk = kv_ref[:, HM + pair * LANES:HM + (pair + 1) * LANES]
        qf = qc[:, cols]
        outs = []
        for a in range(2):
            keep = (lane < HEAD_DIM) if a == 0 else (lane >= HEAD_DIM)
            s = _dot_nt(jnp.where(keep, qf, 0.0).astype(BF16), kblk)
            p = jnp.exp(s - jnp.max(s, axis=-1, keepdims=True))
            l = jnp.sum(p, axis=-1, keepdims=True)
            outs.append(_dot(p.astype(BF16), vblk) * (1.0 / l))
        out_ref[:, cols] = jnp.where(lane < HEAD_DIM, outs[0], outs[1]).astype(BF16)


def _attn_out_ln(s_ref, c_ref, x_ref, w_ref, g_ref, b_ref):
    mix = _dot(s_ref[...], w_ref[0, :HQ, :]) + _dot(c_ref[...], w_ref[0, HQ:, :])
    return _layer_norm(DN_ALPHA * x_ref[...] + mix, g_ref[...], b_ref[...])


def _attn_out_specs(tm, layer):
    row = lambda n: pl.BlockSpec((tm, n), lambda i: (i, 0))
    vec = pl.BlockSpec((1, D_MODEL), lambda i: (0, 0))
    return [row(HQ), row(HM), row(D_MODEL),
            pl.BlockSpec((1, D_MODEL, D_MODEL), lambda i: (layer, 0, 0),
                         pipeline_mode=pl.Buffered(1)), vec, vec]


def _swiglu(xb, w_in_ref, w_out_ref):
    acc = None
    for lo, hi in FF_CHUNKS:
        ha = _dot(xb, w_in_ref[:, lo:hi])
        hb = _dot(xb, w_in_ref[:, D_FF + lo:D_FF + hi])
        act = (ha * (1.0 / (1.0 + jnp.exp(-ha))) * hb).astype(BF16)
        part = _dot(act, w_out_ref[lo:hi, :])
        acc = part if acc is None else acc + part
    return acc


def _dense_ffn_kernel(s_ref, c_ref, xin_ref, wo_ref, ga_ref, ba_ref, w_in_ref, w_out_ref,
                      g_ref, b_ref, ein_ref, eout_ref, o_ref, ein_bf_ref, eout_bf_ref):
    x = _attn_out_ln(s_ref, c_ref, xin_ref, wo_ref, ga_ref, ba_ref)
    ffn = _swiglu(x.astype(BF16), w_in_ref.at[0], w_out_ref.at[0])
    o_ref[...] = _layer_norm(DN_ALPHA * x + ffn, g_ref[...], b_ref[...])
    ein_bf_ref[...] = ein_ref[...].astype(BF16)
    eout_bf_ref[...] = eout_ref[...].astype(BF16)


def _dense_layer_tail(self_out, cross_out, xf, w_o, depth_idx, g_attn, b_attn,
                      w_in, w_out, layer, g, b, w_exp_in, w_exp_out):
    t = xf.shape[0]
    tm = min(t, FFN_TM)
    steps = t // tm
    n_layers = w_exp_in.shape[0]
    ein = w_exp_in.reshape(n_layers * N_EXPERTS * D_MODEL, 2 * D_FF)
    eout = w_exp_out.reshape(n_layers * N_EXPERTS * D_FF, D_MODEL)
    rin, rout = N_EXPERTS * D_MODEL // steps, N_EXPERTS * D_FF // steps
    assert rin * steps == N_EXPERTS * D_MODEL and rout * steps == N_EXPERTS * D_FF
    assert rin % 16 == 0 and rout % 16 == 0
    row = pl.BlockSpec((tm, D_MODEL), lambda i: (i, 0))
    const = lambda r, c: pl.BlockSpec((r, c), lambda i: (0, 0))
    once = pl.Buffered(1)
    y, ein_bf, eout_bf = pl.pallas_call(
        _dense_ffn_kernel,
        out_shape=(jax.ShapeDtypeStruct((t, D_MODEL), F32),
                   jax.ShapeDtypeStruct((N_EXPERTS * D_MODEL, 2 * D_FF), BF16),
                   jax.ShapeDtypeStruct((N_EXPERTS * D_FF, D_MODEL), BF16)),
        grid=(steps,),
        in_specs=_attn_out_specs(tm, depth_idx) + [
                  pl.BlockSpec((1, D_MODEL, 2 * D_FF), lambda i: (layer, 0, 0), pipeline_mode=once),
                  pl.BlockSpec((1, D_FF, D_MODEL), lambda i: (layer, 0, 0), pipeline_mode=once),
                  const(1, D_MODEL), const(1, D_MODEL),
                  pl.BlockSpec((rin, 2 * D_FF), lambda i: (layer * steps + i, 0)),
                  pl.BlockSpec((rout, D_MODEL), lambda i: (layer * steps + i, 0))],
        out_specs=(row, pl.BlockSpec((rin, 2 * D_FF), lambda i: (i, 0)),
                   pl.BlockSpec((rout, D_MODEL), lambda i: (i, 0))),
        compiler_params=_cparams("parallel"),
        name="dense_ffn_ln",
    )(self_out, cross_out, xf, w_o, g_attn, b_attn, w_in, w_out, g, b, ein, eout)
    return (y, ein_bf.reshape(N_EXPERTS, D_MODEL, 2 * D_FF),
            eout_bf.reshape(N_EXPERTS, D_FF, D_MODEL))


def _expert_kernel(be_ref, bx_ref, xs_ref, w_in_ref, w_out_ref, ys_ref):
    del be_ref
    i = pl.program_id(0)
    used = bx_ref[i] == i

    @pl.when(used)
    def _():
        ys_ref[...] = _swiglu(xs_ref[...].astype(BF16), w_in_ref.at[0], w_out_ref.at[0])

    @pl.when(jnp.logical_not(used))
    def _():
        ys_ref[...] = jnp.zeros_like(ys_ref)


def _expert_ffn(xs, w_in, w_out, block_expert, block_src):
    n_slots = xs.shape[0]
    return pl.pallas_call(
        _expert_kernel,
        out_shape=jax.ShapeDtypeStruct((n_slots, D_MODEL), F32),
        grid_spec=pltpu.PrefetchScalarGridSpec(
            num_scalar_prefetch=2,
            grid=(n_slots // MOE_TM,),
            in_specs=[pl.BlockSpec((MOE_TM, D_MODEL), lambda i, be, bx: (bx[i], 0)),
                      pl.BlockSpec((1, D_MODEL, 2 * D_FF), lambda i, be, bx: (be[i], 0, 0)),
                      pl.BlockSpec((1, D_FF, D_MODEL), lambda i, be, bx: (be[i], 0, 0))],
            out_specs=pl.BlockSpec((MOE_TM, D_MODEL), lambda i, be, bx: (i, 0))),
        compiler_params=_cparams("arbitrary"),
        name="expert_ffn",
    )(block_expert, block_src, xs, w_in, w_out)


ROUTE_E1, ROUTE_E2, ROUTE_R1, ROUTE_R2, ROUTE_G1, ROUTE_G2 = range(6)
ROUTE_ROWS = 8


def _router_kernel(s_ref, c_ref, xin_ref, wo_ref, ga_ref, ba_ref, w_ref, b_ref,
                   x_ref, route_ref, route_t_ref, count_ref, carry_ref):
    @pl.when(pl.program_id(0) == 0)
    def _():
        carry_ref[...] = jnp.zeros_like(carry_ref)

    x = _attn_out_ln(s_ref, c_ref, xin_ref, wo_ref, ga_ref, ba_ref)
    x_ref[...] = x
    xs, ws = _split3(x), _split3(w_ref[...])
    logits = (_dot(xs[1], ws[0]) + _dot(xs[0], ws[1])) + _dot(xs[0], ws[0])
    tm = logits.shape[0]
    lane = lax.broadcasted_iota(jnp.int32, (tm, LANES), 1).astype(F32)
    logits = jnp.where(lane < N_EXPERTS, logits + b_ref[...], -jnp.inf)
    m1 = jnp.max(logits, axis=-1, keepdims=True)
    e1 = jnp.min(jnp.where(logits == m1, lane, float(LANES)), axis=-1, keepdims=True)
    rest = jnp.where(lane == e1, -jnp.inf, logits)
    m2 = jnp.max(rest, axis=-1, keepdims=True)
    e2 = jnp.min(jnp.where(rest == m2, lane, float(LANES)), axis=-1, keepdims=True)
    ex = jnp.exp(m2 - m1)
    g1 = 1.0 / (1.0 + ex)
    g2 = ex * g1
    sel1, sel2 = lane == e1, lane == e2
    onehot = jnp.where(sel1 | sel2, 1.0, 0.0)
    tri = (lax.broadcasted_iota(jnp.int32, (tm, tm), 0)
           > lax.broadcasted_iota(jnp.int32, (tm, tm), 1)).astype(F32).astype(BF16)
    rank = _dot(tri, onehot.astype(BF16)) + carry_ref[...]
    r1 = jnp.sum(jnp.where(sel1, rank, 0.0), axis=-1, keepdims=True)
    r2 = jnp.sum(jnp.where(sel2, rank, 0.0), axis=-1, keepdims=True)
    total = carry_ref[...] + jnp.sum(onehot, axis=0, keepdims=True)
    carry_ref[...] = total
    count_ref[...] = total
    route = jnp.zeros((tm, LANES), F32)
    for ln, val in ((ROUTE_E1, e1), (ROUTE_E2, e2), (ROUTE_R1, r1), (ROUTE_R2, r2),
                    (ROUTE_G1, g1), (ROUTE_G2, g2)):
        route = jnp.where(lane == float(ln), val, route)
    route_ref[...] = route
    route_t_ref[...] = route.T[:ROUTE_ROWS, :]


def _attn_out_router(self_out, cross_out, xf, w_o, depth_idx, g_attn, b_attn, w_pad, b_pad):
    t = xf.shape[0]
    tm = min(t, ROUTER_TM)
    return pl.pallas_call(
        _router_kernel,
        out_shape=(jax.ShapeDtypeStruct((t, D_MODEL), F32), jax.ShapeDtypeStruct((t, LANES), F32),
                   jax.ShapeDtypeStruct((ROUTE_ROWS, t), F32), jax.ShapeDtypeStruct((1, LANES), F32)),
        grid=(t // tm,),
        in_specs=_attn_out_specs(tm, depth_idx) + [
                  pl.BlockSpec((D_MODEL, LANES), lambda i: (0, 0)),
                  pl.BlockSpec((1, LANES), lambda i: (0, 0))],
        out_specs=(pl.BlockSpec((tm, D_MODEL), lambda i: (i, 0)),
                   pl.BlockSpec((tm, LANES), lambda i: (i, 0)),
                   pl.BlockSpec((ROUTE_ROWS, tm), lambda i: (0, i)),
                   pl.BlockSpec((1, LANES), lambda i: (0, 0))),
        scratch_shapes=[pltpu.VMEM((1, LANES), F32)],
        compiler_params=_cparams("arbitrary"),
        name="router",
    )(self_out, cross_out, xf, w_o, g_attn, b_attn, w_pad, b_pad)


def _start_rows(n, make_copy):
    for r in range(n):
        make_copy(r, 0).start(priority=0)
        make_copy(r, 1).start(priority=1)


def _wait_rows(n, make_copy):
    for r in range(n):
        make_copy(r, 0).wait()
        make_copy(r, 1).wait()


def _for_slot(slot, fn):
    for static_slot in range(2):
        @pl.when(slot == static_slot)
        def _():
            fn(static_slot)


def _dispatch_kernel(dest_ref, pend_ref, x_ref, xs_hbm, zero_ref, stage_ref, sems, zsem):
    step, n_steps = pl.program_id(0), pl.num_programs(0)
    n_tok = dest_ref.shape[0] // 2

    @pl.when(pl.program_id(0) == 0)
    def _():
        zero_ref[...] = jnp.zeros_like(zero_ref)

        def zero_copy(e):
            start = pl.multiple_of(pend_ref[e + 1] - MOE_TM, MOE_TM)
            return pltpu.make_async_copy(zero_ref, xs_hbm.at[pl.ds(start, MOE_TM)], zsem)

        def tail_copy(blk):
            return pltpu.make_async_copy(
                zero_ref, xs_hbm.at[pl.ds(pl.multiple_of(blk * MOE_TM, MOE_TM), MOE_TM)], zsem)

        first_tail = pend_ref[N_EXPERTS] // MOE_TM
        n_blocks = xs_hbm.shape[0] // MOE_TM
        for e in range(N_EXPERTS):
            @pl.when(pend_ref[e + 1] > pend_ref[e])
            def _():
                zero_copy(e).start()
        lax.fori_loop(first_tail, n_blocks, lambda blk, c: (tail_copy(blk).start(), c)[1], 0)
        for e in range(N_EXPERTS):
            @pl.when(pend_ref[e + 1] > pend_ref[e])
            def _():
                zero_copy(e).wait()
        lax.fori_loop(first_tail, n_blocks, lambda blk, c: (tail_copy(blk).wait(), c)[1], 0)

    def copies(of_step, slot):
        base = of_step * ROW_TM

        def make_copy(r, k):
            return pltpu.make_async_copy(stage_ref.at[slot, pl.ds(r, 1)],
                                         xs_hbm.at[pl.ds(dest_ref[k * n_tok + base + r], 1)],
                                         sems.at[slot])
        return make_copy

    def issue(slot):
        stage_ref[slot] = x_ref[...]
        _start_rows(ROW_TM, copies(step, slot))

    _for_slot(step % 2, issue)

    @pl.when(step > 0)
    def _():
        _for_slot((step - 1) % 2, lambda slot: _wait_rows(ROW_TM, copies(step - 1, slot)))

    @pl.when(step == n_steps - 1)
    def _():
        _for_slot(step % 2, lambda slot: _wait_rows(ROW_TM, copies(step, slot)))


def _dispatch(xf, dest, seg_ends, n_slots):
    t = xf.shape[0]
    return pl.pallas_call(
        _dispatch_kernel,
        out_shape=jax.ShapeDtypeStruct((n_slots, D_MODEL), F32),
        grid_spec=pltpu.PrefetchScalarGridSpec(
            num_scalar_prefetch=2,
            grid=(t // ROW_TM,),
            in_specs=[pl.BlockSpec((ROW_TM, D_MODEL), lambda i, d, p: (i, 0))],
            out_specs=pl.BlockSpec(memory_space=pl.ANY),
            scratch_shapes=[pltpu.VMEM((MOE_TM, D_MODEL), F32),
                            pltpu.VMEM((2, ROW_TM, D_MODEL), F32),
                            pltpu.SemaphoreType.DMA((2,)), pltpu.SemaphoreType.DMA(())]),
        compiler_params=_cparams("arbitrary"),
        name="moe_dispatch",
    )(dest, seg_ends, xf)


def _combine_kernel(dest_ref, x_ref, route_ref, g_ref, b_ref, ys_hbm, o_ref, ya, yb, sems):
    step, n_steps = pl.program_id(0), pl.num_programs(0)
    n_tok = dest_ref.shape[0] // 2

    def copies(of_step, slot):
        base = of_step * ROW_TM

        def make_copy(r, k):
            dst = ya if k == 0 else yb
            return pltpu.make_async_copy(ys_hbm.at[pl.ds(dest_ref[k * n_tok + base + r], 1)],
                                         dst.at[slot, pl.ds(r, 1)], sems.at[slot])
        return make_copy

    @pl.when(step == 0)
    def _():
        _start_rows(ROW_TM, copies(0, 0))

    @pl.when(step + 1 < n_steps)
    def _():
        _for_slot((step + 1) % 2, lambda slot: _start_rows(ROW_TM, copies(step + 1, slot)))

    def finish(slot):
        _wait_rows(ROW_TM, copies(step, slot))
        route = route_ref[...]
        g1 = route[:, ROUTE_G1:ROUTE_G1 + 1]
        g2 = route[:, ROUTE_G2:ROUTE_G2 + 1]
        y = ya[slot] * g1 + yb[slot] * g2
        o_ref[...] = _layer_norm(DN_ALPHA * x_ref[...] + y, g_ref[...], b_ref[...])

    _for_slot(step % 2, finish)


def _combine_ln(xf, route, ys, dest, g, b):
    t = xf.shape[0]
    row = lambda n: pl.BlockSpec((ROW_TM, n), lambda i, d: (i, 0))
    const = pl.BlockSpec((1, D_MODEL), lambda i, d: (0, 0))
    return pl.pallas_call(
        _combine_kernel,
        out_shape=jax.ShapeDtypeStruct((t, D_MODEL), F32),
        grid_spec=pltpu.PrefetchScalarGridSpec(
            num_scalar_prefetch=1,
            grid=(t // ROW_TM,),
            in_specs=[row(D_MODEL), row(LANES), const, const, pl.BlockSpec(memory_space=pl.ANY)],
            out_specs=row(D_MODEL),
            scratch_shapes=[pltpu.VMEM((2, ROW_TM, D_MODEL), F32),
                            pltpu.VMEM((2, ROW_TM, D_MODEL), F32),
                            pltpu.SemaphoreType.DMA((2,))]),
        compiler_params=_cparams("arbitrary"),
        name="moe_combine_ln",
    )(dest, xf, route, g, b, ys)


def _moe_layer_tail(self_out, cross_out, xf, w_o, depth_idx, g_attn, b_attn,
                    w_router, b_router, w_in, w_out, g, b):
    t = xf.shape[0]
    w_pad = jnp.zeros((D_MODEL, LANES), F32).at[:, :N_EXPERTS].set(w_router)
    b_pad = jnp.zeros((1, LANES), F32).at[0, :N_EXPERTS].set(b_router)
    xf, route, route_t, counts = _attn_out_router(self_out, cross_out, xf, w_o, depth_idx,
                                                  g_attn, b_attn, w_pad, b_pad)
    counts = counts[0, :N_EXPERTS].astype(jnp.int32)
    pcounts = (counts + MOE_TM - 1) // MOE_TM * MOE_TM
    pends = jnp.cumsum(pcounts)
    pstarts = pends - pcounts
    e = route_t[ROUTE_E1:ROUTE_E2 + 1].astype(jnp.int32)
    r = route_t[ROUTE_R1:ROUTE_R2 + 1].astype(jnp.int32)
    seg_start = sum(jnp.where(e == k, pstarts[k], 0) for k in range(N_EXPERTS))
    dest = (seg_start + r).reshape(-1)
    n_blocks = 2 * t // MOE_TM + N_EXPERTS
    blk_start = jnp.arange(n_blocks, dtype=jnp.int32) * MOE_TM
    block_expert = jnp.minimum(jnp.sum(blk_start[:, None] >= pends[None, :], axis=1),
                               N_EXPERTS - 1).astype(jnp.int32)
    n_used = pends[-1] // MOE_TM
    block_src = jnp.minimum(jnp.arange(n_blocks, dtype=jnp.int32), n_used - 1)
    seg_ends = jnp.concatenate([jnp.zeros((1,), jnp.int32), pends.astype(jnp.int32)])
    xs = _dispatch(xf, dest, seg_ends, n_blocks * MOE_TM)
    ys = _expert_ffn(xs, w_in, w_out, block_expert, block_src)
    return _combine_ln(xf, route, ys, dest, g, b)


def _dup_heads(w):
    d, n = w.shape
    w = w.reshape(d, n // HEAD_DIM, 1, HEAD_DIM)
    return jnp.broadcast_to(w, (d, n // HEAD_DIM, 2, HEAD_DIM)).reshape(d, 2 * n)


def kernel(x, mem, positions, w_in_swa, attn_sinks, w_in_fox, b_forget, w_mem_kv, w_out,
           ln_attn_g, ln_attn_b, w_ffn_in, w_ffn_out, w_router, b_router, w_exp_in,
           w_exp_out, ln_ffn_g, ln_ffn_b):
    b, s, d = x.shape
    t = b * s
    hkv = SWA_KV_HEADS * HEAD_DIM
    cos_t, sin_t = _rope_tables(positions)
    mem2d = mem.reshape(b * MEM_LEN, d)
    xf = x.reshape(t, d)
    w_out_b = w_out.astype(BF16)
    w_ffn_in_b, w_ffn_out_b = w_ffn_in.astype(BF16), w_ffn_out.astype(BF16)
    exp_w = None
    for i in range(DEPTH):
        j = i // 2
        kvm = _mem_kv(mem2d, w_mem_kv[i].astype(BF16)).reshape(b, MEM_LEN, 2 * HM)
        if i % 2 == 0:
            w = w_in_swa[j]
            w = jnp.concatenate([w[:, :HQ], _dup_heads(w[:, HQ:HQ + hkv]),
                                 _dup_heads(w[:, HQ + hkv:HQ + 2 * hkv]), w[:, HQ + 2 * hkv:]],
                                axis=1).astype(BF16)
            q, kk, vv, cross_out = _swa_proj(xf, w, cos_t, sin_t, kvm)
            self_out = _swa_attn(q.reshape(b, s, HQ), kk.reshape(b, s, HKV2),
                                 vv.reshape(b, s, HKV2), attn_sinks[j])
        else:
            w = jnp.pad(w_in_fox[j], ((0, 0), (0, FOX_COLS - w_in_fox.shape[-1]))).astype(BF16)
            bf_row = jnp.pad(b_forget[j], (0, LANES - SELF_HEADS))[None, :]
            q, k, v, cross_out, dcum = _fox_proj(xf.reshape(b, s, d), w, bf_row, kvm)
            self_out = _fox_attn(q, k, v, dcum)
        attn = (self_out.reshape(t, HQ), cross_out.reshape(t, HM), xf, w_out_b, i,
                ln_attn_g[i][None, :], ln_attn_b[i][None, :])
        g, bb = ln_ffn_g[i][None, :], ln_ffn_b[i][None, :]
        if i % 2 == 0:
            xf, *exp_w = _dense_layer_tail(*attn, w_ffn_in_b, w_ffn_out_b, j, g, bb,
                                           w_exp_in, w_exp_out)
        else:
            xf = _moe_layer_tail(*attn, w_router[j], b_router[j], exp_w[0], exp_w[1], g, bb)
    return xf.reshape(b, s, d)
```

```python
import jax
import jax.numpy as jnp
from jax import lax
from jax.experimental import pallas as pl
from jax.experimental.pallas import tpu as pltpu

D_MODEL = 1024
DEPTH = 4
CHUNK = 64
HEAD_DIM = 64
MEM_HEADS = 4
SELF_HEADS = 12
SWA_KV_HEADS = 3
SWA_GROUP = SELF_HEADS // SWA_KV_HEADS
WINDOW_CHUNKS = 2
ROT_DIM = HEAD_DIM // 4
ROT_HALF = ROT_DIM // 2
ROPE_THETA = 500000.0
MEM_LEN = 256
D_FF = 2816
N_EXPERTS = 8
LN_EPS = 1e-5
NEG = -1e30
ATTN_SCALE = HEAD_DIM ** -0.5
LOG2E = 1.4426950408889634
DN_ALPHA = (2.0 * DEPTH) ** 0.25

HQ = SELF_HEADS * HEAD_DIM
HM = MEM_HEADS * HEAD_DIM
HKV2 = 2 * SWA_KV_HEADS * HEAD_DIM
SWA_COLS = HQ + 2 * HKV2 + HM
FOX_COLS = 3 * HQ + HM + 128

LANES = 128
VMEM_LIMIT_BYTES = 56 * 1024 * 1024

PROJ_TM = 512
FOX_PROJ_TM = 512
SWA_TQ = 256
SWA_SUB = 128
FOX_TQ = 2048
FOX_TK = 512
FFN_TM = 256
MXU_DIM = 256
FF_CHUNKS = ((0, 6 * MXU_DIM), (6 * MXU_DIM, D_FF))
ROUTER_TM = 512
MOE_TM = 256
ROW_TM = 256

F32 = jnp.float32
BF16 = jnp.bfloat16


def _cparams(*sem):
    return pltpu.CompilerParams(dimension_semantics=sem, vmem_limit_bytes=VMEM_LIMIT_BYTES)


def _layer_norm(y, g, b):
    mu = jnp.mean(y, axis=-1, keepdims=True)
    d = y - mu
    var = jnp.mean(d * d, axis=-1, keepdims=True)
    return d * lax.rsqrt(var + LN_EPS) * g + b


def _split3(a):
    hi = a.astype(BF16)
    r1 = a - hi.astype(F32)
    mid = r1.astype(BF16)
    lo = (r1 - mid.astype(F32)).astype(BF16)
    return hi, mid, lo


def _dot(a, b):
    return jnp.dot(a, b, preferred_element_type=F32)


def _dot_nt(a, b):
    return lax.dot_general(a, b, (((1,), (1,)), ((), ())), preferred_element_type=F32)


def _rope_table_kernel(pos_ref, freq_ref, cos_ref, sin_ref):
    ang = pos_ref[...].astype(F32) * freq_ref[...]
    sub = lax.broadcasted_iota(jnp.int32, ang.shape, 1) & (HEAD_DIM - 1)
    c, s = jnp.cos(ang), jnp.sin(ang)
    cos_ref[...] = jnp.where(sub < ROT_DIM, c, 1.0)
    sin_ref[...] = jnp.where(sub < ROT_HALF, -s, jnp.where(sub < ROT_DIM, s, 0.0))


def _rope_tables(positions):
    t = positions.size
    tm = min(t, 2048)
    inv_freq = ROPE_THETA ** (-jnp.arange(0, ROT_DIM, 2, dtype=F32) / ROT_DIM)
    lane = jnp.arange(LANES) % HEAD_DIM % ROT_HALF
    freq = inv_freq[lane][None, :]
    out = jax.ShapeDtypeStruct((t, LANES), F32)
    return pl.pallas_call(
        _rope_table_kernel,
        out_shape=(out, out),
        grid=(t // tm,),
        in_specs=[pl.BlockSpec((tm, 1), lambda i: (i, 0)),
                  pl.BlockSpec((1, LANES), lambda i: (0, 0))],
        out_specs=(pl.BlockSpec((tm, LANES), lambda i: (i, 0)),
                   pl.BlockSpec((tm, LANES), lambda i: (i, 0))),
        compiler_params=_cparams("parallel"),
        name="rope_tables",
    )(positions.reshape(t, 1), freq)


def _mem_kv_kernel(m_ref, w_ref, o_ref):
    o_ref[...] = _dot(m_ref[...].astype(BF16), w_ref[...]).astype(BF16)


def _mem_kv(mem2d, w):
    n = mem2d.shape[0]
    tm = min(n, PROJ_TM)
    return pl.pallas_call(
        _mem_kv_kernel,
        out_shape=jax.ShapeDtypeStruct((n, 2 * HM), BF16),
        grid=(n // tm,),
        in_specs=[pl.BlockSpec((tm, D_MODEL), lambda i: (i, 0)),
                  pl.BlockSpec((D_MODEL, 2 * HM), lambda i: (0, 0))],
        out_specs=pl.BlockSpec((tm, 2 * HM), lambda i: (i, 0)),
        compiler_params=_cparams("parallel"),
        name="mem_kv_proj",
    )(mem2d, w)


def _swa_proj_kernel(x_ref, w_ref, cos_ref, sin_ref, kvm_ref, q_ref, kk_ref, vv_ref, cross_ref):
    acc = _dot(x_ref[...].astype(BF16), w_ref[...])
    cos, sin = cos_ref[...], sin_ref[...]
    first = (lax.broadcasted_iota(jnp.int32, cos.shape, 1) & (HEAD_DIM - 1)) < ROT_HALF

    def rope(t):
        fwd = pltpu.roll(t, LANES - ROT_HALF, axis=1)
        bwd = pltpu.roll(t, ROT_HALF, axis=1)
        return t * cos + jnp.where(first, fwd, bwd) * sin

    for g in range(HQ // LANES):
        sl = slice(g * LANES, (g + 1) * LANES)
        q_ref[:, sl] = (rope(acc[:, sl]) * ATTN_SCALE).astype(BF16)
    for g in range(HKV2 // LANES):
        src = slice(HQ + g * LANES, HQ + (g + 1) * LANES)
        kk_ref[:, g * LANES:(g + 1) * LANES] = rope(acc[:, src]).astype(BF16)
    vv_ref[...] = acc[:, HQ + HKV2:HQ + 2 * HKV2].astype(BF16)
    _memory_attention(acc[:, HQ + 2 * HKV2:] * ATTN_SCALE, kvm_ref.at[0], cross_ref)


def _swa_proj(xf, w, cos_t, sin_t, kvm):
    t = xf.shape[0]
    tm = min(t // kvm.shape[0], PROJ_TM)
    steps_per_batch = t // kvm.shape[0] // tm
    row = lambda n: pl.BlockSpec((tm, n), lambda i: (i, 0))
    return pl.pallas_call(
        _swa_proj_kernel,
        out_shape=(jax.ShapeDtypeStruct((t, HQ), BF16), jax.ShapeDtypeStruct((t, HKV2), BF16),
                   jax.ShapeDtypeStruct((t, HKV2), BF16), jax.ShapeDtypeStruct((t, HM), BF16)),
        grid=(t // tm,),
        in_specs=[row(D_MODEL), pl.BlockSpec((D_MODEL, SWA_COLS), lambda i: (0, 0)),
                  row(LANES), row(LANES),
                  pl.BlockSpec((1, MEM_LEN, 2 * HM), lambda i: (i // steps_per_batch, 0, 0))],
        out_specs=(row(HQ), row(HKV2), row(HKV2), row(HM)),
        compiler_params=_cparams("parallel"),
        name="swa_proj",
    )(xf, w, cos_t, sin_t, kvm)


def _swa_attn_kernel(sink_ref, q_ref, kp_ref, kc_ref, vp_ref, vc_ref, o_ref):
    first_tile = pl.program_id(1) == 0
    half = SWA_TQ - SWA_SUB
    lane = lax.broadcasted_iota(jnp.int32, (SWA_SUB, LANES), 1)
    shift = CHUNK.bit_length() - 1
    rc = lax.broadcasted_iota(jnp.int32, (SWA_SUB, 2 * SWA_SUB), 0) >> shift
    cc = lax.broadcasted_iota(jnp.int32, (SWA_SUB, 2 * SWA_SUB), 1) >> shift
    band = (cc >= rc) & (cc <= rc + WINDOW_CHUNKS)
    band_prev = band & ((cc >= SWA_SUB // CHUNK) | jnp.logical_not(first_tile))
    for sub in range(SWA_TQ // SWA_SUB):
        rows = slice(sub * SWA_SUB, (sub + 1) * SWA_SUB)
        mask = band_prev if sub == 0 else band
        for h in range(SWA_KV_HEADS):
            cols = slice(h * LANES, (h + 1) * LANES)
            if sub == 0:
                kcat = jnp.concatenate([kp_ref[0, half:, cols], kc_ref[0, :SWA_SUB, cols]], axis=0)
                vcat = jnp.concatenate([vp_ref[0, half:, cols], vc_ref[0, :SWA_SUB, cols]], axis=0)
            else:
                kcat = kc_ref[0, :, cols]
                vcat = vc_ref[0, :, cols]
            for pair in range(SWA_GROUP // 2):
                g = h * (SWA_GROUP // 2) + pair
                qcols = slice(g * LANES, (g + 1) * LANES)
                qf = q_ref[0, rows, qcols].astype(F32)
                outs = []
                for a in range(2):
                    keep = (lane < HEAD_DIM) if a == 0 else (lane >= HEAD_DIM)
                    qa = jnp.where(keep, qf, 0.0).astype(BF16)
                    s = jnp.where(mask, _dot_nt(qa, kcat), NEG)
                    sink = sink_ref[2 * g + a]
                    m = jnp.maximum(jnp.max(s, axis=-1, keepdims=True), sink)
                    p = jnp.exp(s - m)
                    den = jnp.sum(p, axis=-1, keepdims=True) + jnp.exp(sink - m)
                    probs = (p * (1.0 / den)).astype(BF16)
                    outs.append(_dot(probs, vcat))
                o = jnp.where(lane < HEAD_DIM, outs[0], outs[1])
                o_ref[0, rows, qcols] = o.astype(BF16)


def _swa_attn(q, kk, vv, sinks):
    b, s, _ = q.shape
    nt = s // SWA_TQ
    cur = lambda n: pl.BlockSpec((1, SWA_TQ, n), lambda bi, i: (bi, i, 0))
    prev = lambda n: pl.BlockSpec((1, SWA_TQ, n), lambda bi, i: (bi, jnp.maximum(i - 1, 0), 0))
    return pl.pallas_call(
        _swa_attn_kernel,
        out_shape=jax.ShapeDtypeStruct((b, s, HQ), BF16),
        grid=(b, nt),
        in_specs=[pl.BlockSpec(memory_space=pltpu.SMEM),
                  cur(HQ), prev(HKV2), cur(HKV2), prev(HKV2), cur(HKV2)],
        out_specs=cur(HQ),
        compiler_params=_cparams("parallel", "parallel"),
        name="swa_attn",
    )(sinks, q, kk, kk, vv, vv)


def _fox_proj_kernel(x_ref, w_ref, bf_ref, kvm_ref, q_ref, k_ref, v_ref, cross_ref, d_ref,
                     carry_ref):
    @pl.when(pl.program_id(1) == 0)
    def _():
        carry_ref[...] = jnp.zeros_like(carry_ref)

    acc = _dot(x_ref[0].astype(BF16), w_ref[...])
    q_ref[0] = (acc[:, :HQ] * (ATTN_SCALE * LOG2E)).astype(BF16)
    k_ref[0] = acc[:, HQ:2 * HQ].astype(BF16)
    v_ref[0] = acc[:, 2 * HQ:3 * HQ].astype(BF16)
    _memory_attention(acc[:, 3 * HQ:3 * HQ + HM] * ATTN_SCALE, kvm_ref.at[0], cross_ref.at[0])
    z = acc[:, 3 * HQ + HM:] + bf_ref[...]
    log_f = jnp.minimum(z, 0.0) - jnp.log(1.0 + jnp.exp(-jnp.abs(z)))
    tm = z.shape[0]
    tri = (lax.broadcasted_iota(jnp.int32, (tm, tm), 0)
           >= lax.broadcasted_iota(jnp.int32, (tm, tm), 1)).astype(F32).astype(BF16)
    hi, mid, lo = _split3(log_f)
    cum = (_dot(tri, lo) + _dot(tri, mid)) + _dot(tri, hi) + carry_ref[...]
    d_ref[0] = cum
    carry_ref[...] = cum[tm - 1:tm, :]


def _fox_proj(x3, w, bf_row, kvm):
    b, s, _ = x3.shape
    tm = min(s, FOX_PROJ_TM)
    row = lambda n: pl.BlockSpec((1, tm, n), lambda bi, i: (bi, i, 0))
    sds = lambda n, dt: jax.ShapeDtypeStruct((b, s, n), dt)
    return pl.pallas_call(
        _fox_proj_kernel,
        out_shape=(sds(HQ, BF16), sds(HQ, BF16), sds(HQ, BF16), sds(HM, BF16), sds(LANES, F32)),
        grid=(b, s // tm),
        in_specs=[row(D_MODEL), pl.BlockSpec((D_MODEL, FOX_COLS), lambda bi, i: (0, 0)),
                  pl.BlockSpec((1, LANES), lambda bi, i: (0, 0)),
                  pl.BlockSpec((1, MEM_LEN, 2 * HM), lambda bi, i: (bi, 0, 0))],
        out_specs=(row(HQ), row(HQ), row(HQ), row(HM), row(LANES)),
        scratch_shapes=[pltpu.VMEM((1, LANES), F32)],
        compiler_params=_cparams("arbitrary", "arbitrary"),
        name="fox_proj",
    )(x3, w, bf_row, kvm)


def _fox_attn_kernel(q_ref, k_ref, v_ref, d_ref, o_ref, kaug_ref, qaug_ref, vt_ref, qt_ref):
    s_len = q_ref.shape[1]
    tq, tk = qt_ref.shape[3], vt_ref.shape[2]
    pair = pl.program_id(1)
    lane = lax.broadcasted_iota(jnp.int32, (s_len, LANES), 1)
    prow = lax.broadcasted_iota(jnp.int32, (3 * LANES, 2 * LANES), 0)
    pcol = lax.broadcasted_iota(jnp.int32, (3 * LANES, 2 * LANES), 1)
    term = prow >> 7
    kf = k_ref[0].astype(F32)
    dsplit = jnp.concatenate(_split3(d_ref[0] * LOG2E), axis=1)
    for a in range(2):
        head = 2 * pair + a
        from_head = (prow & (LANES - 1)) == head
        place = (jnp.where(from_head & (pcol == term), 1.0, 0.0)
                 + jnp.where(from_head & (pcol == LANES + 3 + term), -1.0, 0.0)).astype(BF16)
        extra = _dot(dsplit, place)
        q_extra = extra[:, :LANES] + jnp.where((lane >= 3) & (lane < 6), 1.0, 0.0)
        k_extra = extra[:, LANES:] + jnp.where(lane < 3, 1.0, 0.0)
        keep = (lane < HEAD_DIM) if a == 0 else (lane >= HEAD_DIM)
        kaug_ref[a, :, :LANES] = jnp.where(keep, kf, 0.0).astype(BF16)
        kaug_ref[a, :, LANES:] = k_extra.astype(BF16)
        qaug_ref[a, :, :LANES] = q_ref[0]
        qaug_ref[a, :, LANES:] = q_extra.astype(BF16)
    for j in range(s_len // tk):
        vt_ref[j] = v_ref[0, j * tk:(j + 1) * tk, :].astype(F32).T.astype(BF16)

    nq = s_len // tq
    for qi in range(nq):
        for a in range(2):
            qt_ref[a, qi] = qaug_ref[a, qi * tq:(qi + 1) * tq, :].astype(F32).T.astype(BF16)

    def tile(kb, qi, a, state):
        q_off = max(0, kb * tk - qi * tq)
        width = tq - q_off
        masked = (kb + 1) * tk - 1 > qi * tq + q_off
        m0, l0, acc0 = state
        m, l, acc = m0[:, q_off:], l0[:, q_off:], acc0[:, q_off:]
        s = _dot(kaug_ref[a, kb * tk:(kb + 1) * tk, :], qt_ref[a, qi, :, q_off:])
        if masked:
            n_mask = min(width, pl.cdiv((kb + 1) * tk - (qi * tq + q_off), LANES) * LANES)
            k_pos = kb * tk + lax.broadcasted_iota(jnp.int32, (tk, n_mask), 0)
            q_pos = qi * tq + q_off + lax.broadcasted_iota(jnp.int32, (tk, n_mask), 1)
            head_cols = jnp.where(k_pos <= q_pos, s[:, :n_mask], NEG)
            s = head_cols if n_mask == width else jnp.concatenate([head_cols, s[:, n_mask:]], axis=1)
        m_new = jnp.maximum(m, jnp.max(s, axis=0, keepdims=True))
        p = jnp.exp2(s - m_new)
        l_new = jnp.sum(p, axis=0, keepdims=True)
        acc_new = _dot(vt_ref[kb], p.astype(BF16))
        if kb == 0:
            l, acc = l_new, acc_new
        else:
            alpha = jnp.exp2(m - m_new)
            l, acc = alpha * l + l_new, alpha * acc + acc_new
        if q_off:
            m_new = jnp.concatenate([m0[:, :q_off], m_new], axis=1)
            l = jnp.concatenate([l0[:, :q_off], l], axis=1)
            acc = jnp.concatenate([acc0[:, :q_off], acc], axis=1)
        return m_new, l, acc

    state = {(qi, a): (jnp.full((1, tq), NEG, F32), jnp.zeros((1, tq), F32),
                       jnp.zeros((LANES, tq), F32)) for qi in range(nq) for a in range(2)}
    for kb in range(s_len // tk):
        for qi in range(kb * tk // tq, nq):
            for a in range(2):
                state[qi, a] = tile(kb, qi, a, state[qi, a])
    orow = lax.broadcasted_iota(jnp.int32, (LANES, tq), 0)
    for qi in range(nq):
        (_, l0, acc0), (_, l1, acc1) = state[qi, 0], state[qi, 1]
        o_t = jnp.where(orow < HEAD_DIM, acc0 * (1.0 / l0), acc1 * (1.0 / l1))
        o_ref[0, qi * tq:(qi + 1) * tq, :] = o_t.T.astype(BF16)


def _fox_attn(q, k, v, dcum):
    b, s, _ = q.shape
    tq, tk = min(FOX_TQ, s), min(FOX_TK, s)
    seq = lambda: pl.BlockSpec((1, s, LANES), lambda bi, j: (bi, 0, j))
    return pl.pallas_call(
        _fox_attn_kernel,
        out_shape=jax.ShapeDtypeStruct((b, s, HQ), BF16),
        grid=(b, HQ // LANES),
        in_specs=[seq(), seq(), seq(), pl.BlockSpec((1, s, LANES), lambda bi, j: (bi, 0, 0))],
        out_specs=seq(),
        scratch_shapes=[pltpu.VMEM((2, s, 2 * LANES), BF16), pltpu.VMEM((2, s, 2 * LANES), BF16),
                        pltpu.VMEM((s // tk, LANES, tk), BF16),
                        pltpu.VMEM((2, s // tq, 2 * LANES, tq), BF16)],
        compiler_params=_cparams("parallel", "parallel"),
        name="fox_attn",
    )(q, k, v, dcum)


def _memory_attention(qc, kv_ref, out_ref):
    lane = lax.broadcasted_iota(jnp.int32, (qc.shape[0], LANES), 1)
    for pair in range(HM // LANES):
        cols = slice(pair * LANES, (pair + 1) * LANES)
        kblk = kv_ref[:, cols]
        vblk = kv_ref[:, HM + pair * LANES:HM + (pair + 1) * LANES]
        qf = qc[:, cols]
        outs = []
        for a in range(2):
            keep = (lane < HEAD_DIM) if a == 0 else (lane >= HEAD_DIM)
            s = _dot_nt(jnp.where(keep, qf, 0.0).astype(BF16), kblk)
            p = jnp.exp(s - jnp.max(s, axis=-1, keepdims=True))
            l = jnp.sum(p, axis=-1, keepdims=True)
            outs.append(_dot(p.astype(BF16), vblk) * (1.0 / l))
        out_ref[:, cols] = jnp.where(lane < HEAD_DIM, outs[0], outs[1]).astype(BF16)


def _attn_out_ln(s_ref, c_ref, x_ref, w_ref, g_ref, b_ref):
    mix = _dot(s_ref[...], w_ref[0, :HQ, :]) + _dot(c_ref[...], w_ref[0, HQ:, :])
    return _layer_norm(DN_ALPHA * x_ref[...] + mix, g_ref[...], b_ref[...])


def _attn_out_specs(tm, layer):
    row = lambda n: pl.BlockSpec((tm, n), lambda i: (i, 0))
    vec = pl.BlockSpec((1, D_MODEL), lambda i: (0, 0))
    return [row(HQ), row(HM), row(D_MODEL),
            pl.BlockSpec((1, D_MODEL, D_MODEL), lambda i: (layer, 0, 0),
                         pipeline_mode=pl.Buffered(1)), vec, vec]


def _swiglu(xb, w_in_ref, w_out_ref):
    acc = None
    for lo, hi in FF_CHUNKS:
        ha = _dot(xb, w_in_ref[:, lo:hi])
        hb = _dot(xb, w_in_ref[:, D_FF + lo:D_FF + hi])
        act = (ha * (1.0 / (1.0 + jnp.exp(-ha))) * hb).astype(BF16)
        part = _dot(act, w_out_ref[lo:hi, :])
        acc = part if acc is None else acc + part
    return acc


def _dense_ffn_kernel(s_ref, c_ref, xin_ref, wo_ref, ga_ref, ba_ref, w_in_ref, w_out_ref,
                      g_ref, b_ref, ein_ref, eout_ref, o_ref, ein_bf_ref, eout_bf_ref):
    x = _attn_out_ln(s_ref, c_ref, xin_ref, wo_ref, ga_ref, ba_ref)
    ffn = _swiglu(x.astype(BF16), w_in_ref.at[0], w_out_ref.at[0])
    o_ref[...] = _layer_norm(DN_ALPHA * x + ffn, g_ref[...], b_ref[...])
    ein_bf_ref[...] = ein_ref[...].astype(BF16)
    eout_bf_ref[...] = eout_ref[...].astype(BF16)


def _dense_layer_tail(self_out, cross_out, xf, w_o, depth_idx, g_attn, b_attn,
                      w_in, w_out, layer, g, b, w_exp_in, w_exp_out):
    t = xf.shape[0]
    tm = min(t, FFN_TM)
    steps = t // tm
    n_layers = w_exp_in.shape[0]
    ein = w_exp_in.reshape(n_layers * N_EXPERTS * D_MODEL, 2 * D_FF)
    eout = w_exp_out.reshape(n_layers * N_EXPERTS * D_FF, D_MODEL)
    rin, rout = N_EXPERTS * D_MODEL // steps, N_EXPERTS * D_FF // steps
    assert rin * steps == N_EXPERTS * D_MODEL and rout * steps == N_EXPERTS * D_FF
    assert rin % 16 == 0 and rout % 16 == 0
    row = pl.BlockSpec((tm, D_MODEL), lambda i: (i, 0))
    const = lambda r, c: pl.BlockSpec((r, c), lambda i: (0, 0))
    once = pl.Buffered(1)
    y, ein_bf, eout_bf = pl.pallas_call(
        _dense_ffn_kernel,
        out_shape=(jax.ShapeDtypeStruct((t, D_MODEL), F32),
                   jax.ShapeDtypeStruct((N_EXPERTS * D_MODEL, 2 * D_FF), BF16),
                   jax.ShapeDtypeStruct((N_EXPERTS * D_FF, D_MODEL), BF16)),
        grid=(steps,),
        in_specs=_attn_out_specs(tm, depth_idx) + [
                  pl.BlockSpec((1, D_MODEL, 2 * D_FF), lambda i: (layer, 0, 0), pipeline_mode=once),
                  pl.BlockSpec((1, D_FF, D_MODEL), lambda i: (layer, 0, 0), pipeline_mode=once),
                  const(1, D_MODEL), const(1, D_MODEL),
                  pl.BlockSpec((rin, 2 * D_FF), lambda i: (layer * steps + i, 0)),
                  pl.BlockSpec((rout, D_MODEL), lambda i: (layer * steps + i, 0))],
        out_specs=(row, pl.BlockSpec((rin, 2 * D_FF), lambda i: (i, 0)),
                   pl.BlockSpec((rout, D_MODEL), lambda i: (i, 0))),
        compiler_params=_cparams("parallel"),
        name="dense_ffn_ln",
    )(self_out, cross_out, xf, w_o, g_attn, b_attn, w_in, w_out, g, b, ein, eout)
    return (y, ein_bf.reshape(N_EXPERTS, D_MODEL, 2 * D_FF),
            eout_bf.reshape(N_EXPERTS, D_FF, D_MODEL))


def _expert_kernel(be_ref, bx_ref, xs_ref, w_in_ref, w_out_ref, ys_ref):
    del be_ref
    i = pl.program_id(0)
    used = bx_ref[i] == i

    @pl.when(used)
    def _():
        ys_ref[...] = _swiglu(xs_ref[...].astype(BF16), w_in_ref.at[0], w_out_ref.at[0])

    @pl.when(jnp.logical_not(used))
    def _():
        ys_ref[...] = jnp.zeros_like(ys_ref)


def _expert_ffn(xs, w_in, w_out, block_expert, block_src):
    n_slots = xs.shape[0]
    return pl.pallas_call(
        _expert_kernel,
        out_shape=jax.ShapeDtypeStruct((n_slots, D_MODEL), F32),
        grid_spec=pltpu.PrefetchScalarGridSpec(
            num_scalar_prefetch=2,
            grid=(n_slots // MOE_TM,),
            in_specs=[pl.BlockSpec((MOE_TM, D_MODEL), lambda i, be, bx: (bx[i], 0)),
                      pl.BlockSpec((1, D_MODEL, 2 * D_FF), lambda i, be, bx: (be[i], 0, 0)),
                      pl.BlockSpec((1, D_FF, D_MODEL), lambda i, be, bx: (be[i], 0, 0))],
            out_specs=pl.BlockSpec((MOE_TM, D_MODEL), lambda i, be, bx: (i, 0))),
        compiler_params=_cparams("arbitrary"),
        name="expert_ffn",
    )(block_expert, block_src, xs, w_in, w_out)


ROUTE_E1, ROUTE_E2, ROUTE_R1, ROUTE_R2, ROUTE_G1, ROUTE_G2 = range(6)
ROUTE_ROWS = 8


def _router_kernel(s_ref, c_ref, xin_ref, wo_ref, ga_ref, ba_ref, w_ref, b_ref,
                   x_ref, route_ref, route_t_ref, count_ref, carry_ref):
    @pl.when(pl.program_id(0) == 0)
    def _():
        carry_ref[...] = jnp.zeros_like(carry_ref)

    x = _attn_out_ln(s_ref, c_ref, xin_ref, wo_ref, ga_ref, ba_ref)
    x_ref[...] = x
    xs, ws = _split3(x), _split3(w_ref[...])
    lead = _dot(xs[0], jnp.concatenate([ws[0], ws[1]], axis=1))
    logits = (_dot(xs[1], ws[0]) + lead[:, LANES:]) + lead[:, :LANES]
    tm = logits.shape[0]
    lane = lax.broadcasted_iota(jnp.int32, (tm, LANES), 1).astype(F32)
    logits = jnp.where(lane < N_EXPERTS, logits + b_ref[...], -jnp.inf)
    m1 = jnp.max(logits, axis=-1, keepdims=True)
    e1 = jnp.min(jnp.where(logits == m1, lane, float(LANES)), axis=-1, keepdims=True)
    rest = jnp.where(lane == e1, -jnp.inf, logits)
    m2 = jnp.max(rest, axis=-1, keepdims=True)
    e2 = jnp.min(jnp.where(rest == m2, lane, float(LANES)), axis=-1, keepdims=True)
    ex = jnp.exp(m2 - m1)
    g1 = 1.0 / (1.0 + ex)
    g2 = ex * g1
    sel1, sel2 = lane == e1, lane == e2
    onehot = jnp.where(sel1 | sel2, 1.0, 0.0)
    tri = (lax.broadcasted_iota(jnp.int32, (tm, tm), 0)
           > lax.broadcasted_iota(jnp.int32, (tm, tm), 1)).astype(F32).astype(BF16)
    rank = _dot(tri, onehot.astype(BF16)) + carry_ref[...]
    r1 = jnp.sum(jnp.where(sel1, rank, 0.0), axis=-1, keepdims=True)
    r2 = jnp.sum(jnp.where(sel2, rank, 0.0), axis=-1, keepdims=True)
    total = carry_ref[...] + jnp.sum(onehot, axis=0, keepdims=True)
    carry_ref[...] = total
    count_ref[...] = total
    route = jnp.zeros((tm, LANES), F32)
    for ln, val in ((ROUTE_E1, e1), (ROUTE_E2, e2), (ROUTE_R1, r1), (ROUTE_R2, r2),
                    (ROUTE_G1, g1), (ROUTE_G2, g2)):
        route = jnp.where(lane == float(ln), val, route)
    route_ref[...] = route
    route_t_ref[...] = route.T[:ROUTE_ROWS, :]


def _attn_out_router(self_out, cross_out, xf, w_o, depth_idx, g_attn, b_attn, w_pad, b_pad):
    t = xf.shape[0]
    tm = min(t, ROUTER_TM)
    return pl.pallas_call(
        _router_kernel,
        out_shape=(jax.ShapeDtypeStruct((t, D_MODEL), F32), jax.ShapeDtypeStruct((t, LANES), F32),
                   jax.ShapeDtypeStruct((ROUTE_ROWS, t), F32), jax.ShapeDtypeStruct((1, LANES), F32)),
        grid=(t // tm,),
        in_specs=_attn_out_specs(tm, depth_idx) + [
                  pl.BlockSpec((D_MODEL, LANES), lambda i: (0, 0)),
                  pl.BlockSpec((1, LANES), lambda i: (0, 0))],
        out_specs=(pl.BlockSpec((tm, D_MODEL), lambda i: (i, 0)),
                   pl.BlockSpec((tm, LANES), lambda i: (i, 0)),
                   pl.BlockSpec((ROUTE_ROWS, tm), lambda i: (0, i)),
                   pl.BlockSpec((1, LANES), lambda i: (0, 0))),
        scratch_shapes=[pltpu.VMEM((1, LANES), F32)],
        compiler_params=_cparams("arbitrary"),
        name="router",
    )(self_out, cross_out, xf, w_o, g_attn, b_attn, w_pad, b_pad)


def _start_rows(n, make_copy):
    for r in range(n):
        make_copy(r, 0).start(priority=0)
        make_copy(r, 1).start(priority=1)


def _wait_rows(n, make_copy):
    for r in range(n):
        make_copy(r, 0).wait()
        make_copy(r, 1).wait()


def _for_slot(slot, fn):
    for static_slot in range(2):
        @pl.when(slot == static_slot)
        def _():
            fn(static_slot)


def _dispatch_kernel(dest_ref, pend_ref, x_ref, xs_hbm, zero_ref, stage_ref, sems, zsem):
    step, n_steps = pl.program_id(0), pl.num_programs(0)
    n_tok = dest_ref.shape[0] // 2

    @pl.when(pl.program_id(0) == 0)
    def _():
        zero_ref[...] = jnp.zeros_like(zero_ref)

        def zero_copy(e):
            start = pl.multiple_of(pend_ref[e + 1] - MOE_TM, MOE_TM)
            return pltpu.make_async_copy(zero_ref, xs_hbm.at[pl.ds(start, MOE_TM)], zsem)

        def tail_copy(blk):
            return pltpu.make_async_copy(
                zero_ref, xs_hbm.at[pl.ds(pl.multiple_of(blk * MOE_TM, MOE_TM), MOE_TM)], zsem)

        first_tail = pend_ref[N_EXPERTS] // MOE_TM
        n_blocks = xs_hbm.shape[0] // MOE_TM
        for e in range(N_EXPERTS):
            @pl.when(pend_ref[e + 1] > pend_ref[e])
            def _():
                zero_copy(e).start()
        lax.fori_loop(first_tail, n_blocks, lambda blk, c: (tail_copy(blk).start(), c)[1], 0)
        for e in range(N_EXPERTS):
            @pl.when(pend_ref[e + 1] > pend_ref[e])
            def _():
                zero_copy(e).wait()
        lax.fori_loop(first_tail, n_blocks, lambda blk, c: (tail_copy(blk).wait(), c)[1], 0)

    def copies(of_step, slot):
        base = of_step * ROW_TM

        def make_copy(r, k):
            return pltpu.make_async_copy(stage_ref.at[slot, pl.ds(r, 1)],
                                         xs_hbm.at[pl.ds(dest_ref[k * n_tok + base + r], 1)],
                                         sems.at[slot])
        return make_copy

    def issue(slot):
        stage_ref[slot] = x_ref[...]
        _start_rows(ROW_TM, copies(step, slot))

    _for_slot(step % 2, issue)

    @pl.when(step > 0)
    def _():
        _for_slot((step - 1) % 2, lambda slot: _wait_rows(ROW_TM, copies(step - 1, slot)))

    @pl.when(step == n_steps - 1)
    def _():
        _for_slot(step % 2, lambda slot: _wait_rows(ROW_TM, copies(step, slot)))


def _dispatch(xf, dest, seg_ends, n_slots):
    t = xf.shape[0]
    return pl.pallas_call(
        _dispatch_kernel,
        out_shape=jax.ShapeDtypeStruct((n_slots, D_MODEL), F32),
        grid_spec=pltpu.PrefetchScalarGridSpec(
            num_scalar_prefetch=2,
            grid=(t // ROW_TM,),
            in_specs=[pl.BlockSpec((ROW_TM, D_MODEL), lambda i, d, p: (i, 0))],
            out_specs=pl.BlockSpec(memory_space=pl.ANY),
            scratch_shapes=[pltpu.VMEM((MOE_TM, D_MODEL), F32),
                            pltpu.VMEM((2, ROW_TM, D_MODEL), F32),
                            pltpu.SemaphoreType.DMA((2,)), pltpu.SemaphoreType.DMA(())]),
        compiler_params=_cparams("arbitrary"),
        name="moe_dispatch",
    )(dest, seg_ends, xf)


def _combine_kernel(dest_ref, x_ref, route_ref, g_ref, b_ref, ys_hbm, o_ref, ya, yb, sems):
    step, n_steps = pl.program_id(0), pl.num_programs(0)
    n_tok = dest_ref.shape[0] // 2

    def copies(of_step, slot):
        base = of_step * ROW_TM

        def make_copy(r, k):
            dst = ya if k == 0 else yb
            return pltpu.make_async_copy(ys_hbm.at[pl.ds(dest_ref[k * n_tok + base + r], 1)],
                                         dst.at[slot, pl.ds(r, 1)], sems.at[slot])
        return make_copy

    @pl.when(step == 0)
    def _():
        _start_rows(ROW_TM, copies(0, 0))

    @pl.when(step + 1 < n_steps)
    def _():
        _for_slot((step + 1) % 2, lambda slot: _start_rows(ROW_TM, copies(step + 1, slot)))

    def finish(slot):
        _wait_rows(ROW_TM, copies(step, slot))
        route = route_ref[...]
        g1 = route[:, ROUTE_G1:ROUTE_G1 + 1]
        g2 = route[:, ROUTE_G2:ROUTE_G2 + 1]
        y = ya[slot] * g1 + yb[slot] * g2
        o_ref[...] = _layer_norm(DN_ALPHA * x_ref[...] + y, g_ref[...], b_ref[...])

    _for_slot(step % 2, finish)


def _combine_ln(xf, route, ys, dest, g, b):
    t = xf.shape[0]
    row = lambda n: pl.BlockSpec((ROW_TM, n), lambda i, d: (i, 0))
    const = pl.BlockSpec((1, D_MODEL), lambda i, d: (0, 0))
    return pl.pallas_call(
        _combine_kernel,
        out_shape=jax.ShapeDtypeStruct((t, D_MODEL), F32),
        grid_spec=pltpu.PrefetchScalarGridSpec(
            num_scalar_prefetch=1,
            grid=(t // ROW_TM,),
            in_specs=[row(D_MODEL), row(LANES), const, const, pl.BlockSpec(memory_space=pl.ANY)],
            out_specs=row(D_MODEL),
            scratch_shapes=[pltpu.VMEM((2, ROW_TM, D_MODEL), F32),
                            pltpu.VMEM((2, ROW_TM, D_MODEL), F32),
                            pltpu.SemaphoreType.DMA((2,))]),
        compiler_params=_cparams("arbitrary"),
        name="moe_combine_ln",
    )(dest, xf, route, g, b, ys)


def _moe_layer_tail(self_out, cross_out, xf, w_o, depth_idx, g_attn, b_attn,
                    w_router, b_router, w_in, w_out, g, b):
    t = xf.shape[0]
    w_pad = jnp.zeros((D_MODEL, LANES), F32).at[:, :N_EXPERTS].set(w_router)
    b_pad = jnp.zeros((1, LANES), F32).at[0, :N_EXPERTS].set(b_router)
    xf, route, route_t, counts = _attn_out_router(self_out, cross_out, xf, w_o, depth_idx,
                                                  g_attn, b_attn, w_pad, b_pad)
    counts = counts[0, :N_EXPERTS].astype(jnp.int32)
    pcounts = (counts + MOE_TM - 1) // MOE_TM * MOE_TM
    pends = jnp.cumsum(pcounts)
    pstarts = pends - pcounts
    e = route_t[ROUTE_E1:ROUTE_E2 + 1].astype(jnp.int32)
    r = route_t[ROUTE_R1:ROUTE_R2 + 1].astype(jnp.int32)
    seg_start = sum(jnp.where(e == k, pstarts[k], 0) for k in range(N_EXPERTS))
    dest = (seg_start + r).reshape(-1)
    n_blocks = 2 * t // MOE_TM + N_EXPERTS
    blk_start = jnp.arange(n_blocks, dtype=jnp.int32) * MOE_TM
    block_expert = jnp.minimum(jnp.sum(blk_start[:, None] >= pends[None, :], axis=1),
                               N_EXPERTS - 1).astype(jnp.int32)
    n_used = pends[-1] // MOE_TM
    block_src = jnp.minimum(jnp.arange(n_blocks, dtype=jnp.int32), n_used - 1)
    seg_ends = jnp.concatenate([jnp.zeros((1,), jnp.int32), pends.astype(jnp.int32)])
    xs = _dispatch(xf, dest, seg_ends, n_blocks * MOE_TM)
    ys = _expert_ffn(xs, w_in, w_out, block_expert, block_src)
    return _combine_ln(xf, route, ys, dest, g, b)


def _dup_heads(w):
    d, n = w.shape
    w = w.reshape(d, n // HEAD_DIM, 1, HEAD_DIM)
    return jnp.broadcast_to(w, (d, n // HEAD_DIM, 2, HEAD_DIM)).reshape(d, 2 * n)


def kernel(x, mem, positions, w_in_swa, attn_sinks, w_in_fox, b_forget, w_mem_kv, w_out,
           ln_attn_g, ln_attn_b, w_ffn_in, w_ffn_out, w_router, b_router, w_exp_in,
           w_exp_out, ln_ffn_g, ln_ffn_b):
    b, s, d = x.shape
    t = b * s
    hkv = SWA_KV_HEADS * HEAD_DIM
    cos_t, sin_t = _rope_tables(positions)
    mem2d = mem.reshape(b * MEM_LEN, d)
    xf = x.reshape(t, d)
    w_out_b = w_out.astype(BF16)
    w_ffn_in_b, w_ffn_out_b = w_ffn_in.astype(BF16), w_ffn_out.astype(BF16)
    exp_w = None
    for i in range(DEPTH):
        j = i // 2
        kvm = _mem_kv(mem2d, w_mem_kv[i].astype(BF16)).reshape(b, MEM_LEN, 2 * HM)
        if i % 2 == 0:
            w = w_in_swa[j]
            w = jnp.concatenate([w[:, :HQ], _dup_heads(w[:, HQ:HQ + hkv]),
                                 _dup_heads(w[:, HQ + hkv:HQ + 2 * hkv]), w[:, HQ + 2 * hkv:]],
                                axis=1).astype(BF16)
            q, kk, vv, cross_out = _swa_proj(xf, w, cos_t, sin_t, kvm)
            self_out = _swa_attn(q.reshape(b, s, HQ), kk.reshape(b, s, HKV2),
                                 vv.reshape(b, s, HKV2), attn_sinks[j])
        else:
            w = jnp.pad(w_in_fox[j], ((0, 0), (0, FOX_COLS - w_in_fox.shape[-1]))).astype(BF16)
            bf_row = jnp.pad(b_forget[j], (0, LANES - SELF_HEADS))[None, :]
            q, k, v, cross_out, dcum = _fox_proj(xf.reshape(b, s, d), w, bf_row, kvm)
            self_out = _fox_attn(q, k, v, dcum)
        attn = (self_out.reshape(t, HQ), cross_out.reshape(t, HM), xf, w_out_b, i,
                ln_attn_g[i][None, :], ln_attn_b[i][None, :])
        g, bb = ln_ffn_g[i][None, :], ln_ffn_b[i][None, :]
        if i % 2 == 0:
            xf, *exp_w = _dense_layer_tail(*attn, w_ffn_in_b, w_ffn_out_b, j, g, bb,
                                           w_exp_in, w_exp_out)
        else:
            xf = _moe_layer_tail(*attn, w_router[j], b_router[j], exp_w[0], exp_w[1], g, bb)
    return xf.reshape(b, s, d)
```

```python
import jax
import jax.numpy as jnp
from jax import lax
from jax.experimental import pallas as pl
from jax.experimental.pallas import tpu as pltpu

D_MODEL = 1024
DEPTH = 4
CHUNK = 64
HEAD_DIM = 64
MEM_HEADS = 4
SELF_HEADS = 12
SWA_KV_HEADS = 3
SWA_GROUP = SELF_HEADS // SWA_KV_HEADS
WINDOW_CHUNKS = 2
ROT_DIM = HEAD_DIM // 4
ROT_HALF = ROT_DIM // 2
ROPE_THETA = 500000.0
MEM_LEN = 256
D_FF = 2816
N_EXPERTS = 8
LN_EPS = 1e-5
NEG = -1e30
ATTN_SCALE = HEAD_DIM ** -0.5
LOG2E = 1.4426950408889634
DN_ALPHA = (2.0 * DEPTH) ** 0.25

HQ = SELF_HEADS * HEAD_DIM
HM = MEM_HEADS * HEAD_DIM
HKV2 = 2 * SWA_KV_HEADS * HEAD_DIM
SWA_COLS = HQ + 2 * HKV2 + HM
FOX_COLS = 3 * HQ + HM + 128

LANES = 128
VMEM_LIMIT_BYTES = 56 * 1024 * 1024

PROJ_TM = 512
FOX_PROJ_TM = 512
SWA_TQ = 512
SWA_SUB = 128
FOX_TQ = 2048
FOX_TK = 512
FFN_TM = 256
MXU_DIM = 256
FF_CHUNKS = ((0, 6 * MXU_DIM), (6 * MXU_DIM, D_FF))
ROUTER_TM = 512
MOE_TM = 256
ROW_TM = 256

F32 = jnp.float32
BF16 = jnp.bfloat16


def _cparams(*sem):
    return pltpu.CompilerParams(dimension_semantics=sem, vmem_limit_bytes=VMEM_LIMIT_BYTES)


def _layer_norm(y, g, b):
    mu = jnp.mean(y, axis=-1, keepdims=True)
    d = y - mu
    var = jnp.mean(d * d, axis=-1, keepdims=True)
    return d * lax.rsqrt(var + LN_EPS) * g + b


def _split3(a):
    hi = a.astype(BF16)
    r1 = a - hi.astype(F32)
    mid = r1.astype(BF16)
    lo = (r1 - mid.astype(F32)).astype(BF16)
    return hi, mid, lo


def _dot(a, b):
    return jnp.dot(a, b, preferred_element_type=F32)


def _dot_nt(a, b):
    return lax.dot_general(a, b, (((1,), (1,)), ((), ())), preferred_element_type=F32)


def _rope_table_kernel(pos_ref, freq_ref, cos_ref, sin_ref):
    ang = pos_ref[...].astype(F32) * freq_ref[...]
    sub = lax.broadcasted_iota(jnp.int32, ang.shape, 1) & (HEAD_DIM - 1)
    c, s = jnp.cos(ang), jnp.sin(ang)
    cos_ref[...] = jnp.where(sub < ROT_DIM, c, 1.0)
    sin_ref[...] = jnp.where(sub < ROT_HALF, -s, jnp.where(sub < ROT_DIM, s, 0.0))


def _rope_tables(positions):
    t = positions.size
    tm = min(t, 2048)
    inv_freq = ROPE_THETA ** (-jnp.arange(0, ROT_DIM, 2, dtype=F32) / ROT_DIM)
    lane = jnp.arange(LANES) % HEAD_DIM % ROT_HALF
    freq = inv_freq[lane][None, :]
    out = jax.ShapeDtypeStruct((t, LANES), F32)
    return pl.pallas_call(
        _rope_table_kernel,
        out_shape=(out, out),
        grid=(t // tm,),
        in_specs=[pl.BlockSpec((tm, 1), lambda i: (i, 0)),
                  pl.BlockSpec((1, LANES), lambda i: (0, 0))],
        out_specs=(pl.BlockSpec((tm, LANES), lambda i: (i, 0)),
                   pl.BlockSpec((tm, LANES), lambda i: (i, 0))),
        compiler_params=_cparams("parallel"),
        name="rope_tables",
    )(positions.reshape(t, 1), freq)


def _mem_kv_kernel(m_ref, w_ref, o_ref):
    o_ref[...] = _dot(m_ref[...].astype(BF16), w_ref[...]).astype(BF16)


def _mem_kv(mem2d, w):
    n = mem2d.shape[0]
    tm = min(n, PROJ_TM)
    return pl.pallas_call(
        _mem_kv_kernel,
        out_shape=jax.ShapeDtypeStruct((n, 2 * HM), BF16),
        grid=(n // tm,),
        in_specs=[pl.BlockSpec((tm, D_MODEL), lambda i: (i, 0)),
                  pl.BlockSpec((D_MODEL, 2 * HM), lambda i: (0, 0))],
        out_specs=pl.BlockSpec((tm, 2 * HM), lambda i: (i, 0)),
        compiler_params=_cparams("parallel"),
        name="mem_kv_proj",
    )(mem2d, w)


def _swa_proj_kernel(x_ref, w_ref, cos_ref, sin_ref, kvm_ref, q_ref, kk_ref, vv_ref, cross_ref):
    acc = _dot(x_ref[...].astype(BF16), w_ref[...])
    cos, sin = cos_ref[...], sin_ref[...]
    first = (lax.broadcasted_iota(jnp.int32, cos.shape, 1) & (HEAD_DIM - 1)) < ROT_HALF

    def rope(t):
        fwd = pltpu.roll(t, LANES - ROT_HALF, axis=1)
        bwd = pltpu.roll(t, ROT_HALF, axis=1)
        return t * cos + jnp.where(first, fwd, bwd) * sin

    for g in range(HQ // LANES):
        sl = slice(g * LANES, (g + 1) * LANES)
        q_ref[:, sl] = (rope(acc[:, sl]) * ATTN_SCALE).astype(BF16)
    for g in range(HKV2 // LANES):
        src = slice(HQ + g * LANES, HQ + (g + 1) * LANES)
        kk_ref[:, g * LANES:(g + 1) * LANES] = rope(acc[:, src]).astype(BF16)
    vv_ref[...] = acc[:, HQ + HKV2:HQ + 2 * HKV2].astype(BF16)
    _memory_attention(acc[:, HQ + 2 * HKV2:] * ATTN_SCALE, kvm_ref.at[0], cross_ref)


def _swa_proj(xf, w, cos_t, sin_t, kvm):
    t = xf.shape[0]
    tm = min(t // kvm.shape[0], PROJ_TM)
    steps_per_batch = t // kvm.shape[0] // tm
    row = lambda n: pl.BlockSpec((tm, n), lambda i: (i, 0))
    return pl.pallas_call(
        _swa_proj_kernel,
        out_shape=(jax.ShapeDtypeStruct((t, HQ), BF16), jax.ShapeDtypeStruct((t, HKV2), BF16),
                   jax.ShapeDtypeStruct((t, HKV2), BF16), jax.ShapeDtypeStruct((t, HM), BF16)),
        grid=(t // tm,),
        in_specs=[row(D_MODEL), pl.BlockSpec((D_MODEL, SWA_COLS), lambda i: (0, 0)),
                  row(LANES), row(LANES),
                  pl.BlockSpec((1, MEM_LEN, 2 * HM), lambda i: (i // steps_per_batch, 0, 0))],
        out_specs=(row(HQ), row(HKV2), row(HKV2), row(HM)),
        compiler_params=_cparams("parallel"),
        name="swa_proj",
    )(xf, w, cos_t, sin_t, kvm)


def _swa_attn_kernel(sink_ref, q_ref, kp_ref, kc_ref, vp_ref, vc_ref, o_ref):
    first_tile = pl.program_id(1) == 0
    half = SWA_TQ - SWA_SUB
    lane = lax.broadcasted_iota(jnp.int32, (SWA_SUB, LANES), 1)
    shift = CHUNK.bit_length() - 1
    rc = lax.broadcasted_iota(jnp.int32, (SWA_SUB, 2 * SWA_SUB), 0) >> shift
    cc = lax.broadcasted_iota(jnp.int32, (SWA_SUB, 2 * SWA_SUB), 1) >> shift
    band = (cc >= rc) & (cc <= rc + WINDOW_CHUNKS)
    band_prev = band & ((cc >= SWA_SUB // CHUNK) | jnp.logical_not(first_tile))
    for sub in range(SWA_TQ // SWA_SUB):
        rows = slice(sub * SWA_SUB, (sub + 1) * SWA_SUB)
        mask = band_prev if sub == 0 else band
        for h in range(SWA_KV_HEADS):
            cols = slice(h * LANES, (h + 1) * LANES)
            if sub == 0:
                kcat = jnp.concatenate([kp_ref[0, half:, cols], kc_ref[0, :SWA_SUB, cols]], axis=0)
                vcat = jnp.concatenate([vp_ref[0, half:, cols], vc_ref[0, :SWA_SUB, cols]], axis=0)
            else:
                krows = slice((sub - 1) * SWA_SUB, (sub + 1) * SWA_SUB)
                kcat = kc_ref[0, krows, cols]
                vcat = vc_ref[0, krows, cols]
            for pair in range(SWA_GROUP // 2):
                g = h * (SWA_GROUP // 2) + pair
                qcols = slice(g * LANES, (g + 1) * LANES)
                qf = q_ref[0, rows, qcols].astype(F32)
                outs = []
                for a in range(2):
                    keep = (lane < HEAD_DIM) if a == 0 else (lane >= HEAD_DIM)
                    qa = jnp.where(keep, qf, 0.0).astype(BF16)
                    s = jnp.where(mask, _dot_nt(qa, kcat), NEG)
                    sink = sink_ref[2 * g + a]
                    m = jnp.maximum(jnp.max(s, axis=-1, keepdims=True), sink)
                    p = jnp.exp(s - m)
                    den = jnp.sum(p, axis=-1, keepdims=True) + jnp.exp(sink - m)
                    probs = (p * (1.0 / den)).astype(BF16)
                    outs.append(_dot(probs, vcat))
                o = jnp.where(lane < HEAD_DIM, outs[0], outs[1])
                o_ref[0, rows, qcols] = o.astype(BF16)


def _swa_attn(q, kk, vv, sinks):
    b, s, _ = q.shape
    nt = s // SWA_TQ
    cur = lambda n: pl.BlockSpec((1, SWA_TQ, n), lambda bi, i: (bi, i, 0))
    prev = lambda n: pl.BlockSpec((1, SWA_TQ, n), lambda bi, i: (bi, jnp.maximum(i - 1, 0), 0))
    return pl.pallas_call(
        _swa_attn_kernel,
        out_shape=jax.ShapeDtypeStruct((b, s, HQ), BF16),
        grid=(b, nt),
        in_specs=[pl.BlockSpec(memory_space=pltpu.SMEM),
                  cur(HQ), prev(HKV2), cur(HKV2), prev(HKV2), cur(HKV2)],
        out_specs=cur(HQ),
        compiler_params=_cparams("parallel", "parallel"),
        name="swa_attn",
    )(sinks, q, kk, kk, vv, vv)


def _fox_proj_kernel(x_ref, w_ref, bf_ref, kvm_ref, q_ref, k_ref, v_ref, cross_ref, d_ref,
                     carry_ref):
    @pl.when(pl.program_id(1) == 0)
    def _():
        carry_ref[...] = jnp.zeros_like(carry_ref)

    acc = _dot(x_ref[0].astype(BF16), w_ref[...])
    q_ref[0] = (acc[:, :HQ] * (ATTN_SCALE * LOG2E)).astype(BF16)
    k_ref[0] = acc[:, HQ:2 * HQ].astype(BF16)
    v_ref[0] = acc[:, 2 * HQ:3 * HQ].astype(BF16)
    _memory_attention(acc[:, 3 * HQ:3 * HQ + HM] * ATTN_SCALE, kvm_ref.at[0], cross_ref.at[0])
    z = acc[:, 3 * HQ + HM:] + bf_ref[...]
    log_f = jnp.minimum(z, 0.0) - jnp.log(1.0 + jnp.exp(-jnp.abs(z)))
    tm = z.shape[0]
    tri = (lax.broadcasted_iota(jnp.int32, (tm, tm), 0)
           >= lax.broadcasted_iota(jnp.int32, (tm, tm), 1)).astype(F32).astype(BF16)
    hi, mid, lo = _split3(log_f)
    cum = (_dot(tri, lo) + _dot(tri, mid)) + _dot(tri, hi) + carry_ref[...]
    d_ref[0] = cum
    carry_ref[...] = cum[tm - 1:tm, :]


def _fox_proj(x3, w, bf_row, kvm):
    b, s, _ = x3.shape
    tm = min(s, FOX_PROJ_TM)
    row = lambda n: pl.BlockSpec((1, tm, n), lambda bi, i: (bi, i, 0))
    sds = lambda n, dt: jax.ShapeDtypeStruct((b, s, n), dt)
    return pl.pallas_call(
        _fox_proj_kernel,
        out_shape=(sds(HQ, BF16), sds(HQ, BF16), sds(HQ, BF16), sds(HM, BF16), sds(LANES, F32)),
        grid=(b, s // tm),
        in_specs=[row(D_MODEL), pl.BlockSpec((D_MODEL, FOX_COLS), lambda bi, i: (0, 0)),
                  pl.BlockSpec((1, LANES), lambda bi, i: (0, 0)),
                  pl.BlockSpec((1, MEM_LEN, 2 * HM), lambda bi, i: (bi, 0, 0))],
        out_specs=(row(HQ), row(HQ), row(HQ), row(HM), row(LANES)),
        scratch_shapes=[pltpu.VMEM((1, LANES), F32)],
        compiler_params=_cparams("arbitrary", "arbitrary"),
        name="fox_proj",
    )(x3, w, bf_row, kvm)


def _fox_attn_kernel(q_ref, k_ref, v_ref, d_ref, o_ref, kaug_ref, qaug_ref, vt_ref, qt_ref):
    s_len = q_ref.shape[1]
    tq, tk = qt_ref.shape[3], vt_ref.shape[2]
    pair = pl.program_id(1)
    lane = lax.broadcasted_iota(jnp.int32, (s_len, LANES), 1)
    prow = lax.broadcasted_iota(jnp.int32, (3 * LANES, 2 * LANES), 0)
    pcol = lax.broadcasted_iota(jnp.int32, (3 * LANES, 2 * LANES), 1)
    term = prow >> 7
    kf = k_ref[0].astype(F32)
    dsplit = jnp.concatenate(_split3(d_ref[0] * LOG2E), axis=1)
    for a in range(2):
        head = 2 * pair + a
        from_head = (prow & (LANES - 1)) == head
        place = (jnp.where(from_head & (pcol == term), 1.0, 0.0)
                 + jnp.where(from_head & (pcol == LANES + 3 + term), -1.0, 0.0)).astype(BF16)
        extra = _dot(dsplit, place)
        q_extra = extra[:, :LANES] + jnp.where((lane >= 3) & (lane < 6), 1.0, 0.0)
        k_extra = extra[:, LANES:] + jnp.where(lane < 3, 1.0, 0.0)
        keep = (lane < HEAD_DIM) if a == 0 else (lane >= HEAD_DIM)
        kaug_ref[a, :, :LANES] = jnp.where(keep, kf, 0.0).astype(BF16)
        kaug_ref[a, :, LANES:] = k_extra.astype(BF16)
        qaug_ref[a, :, :LANES] = q_ref[0]
        qaug_ref[a, :, LANES:] = q_extra.astype(BF16)
    for j in range(s_len // tk):
        vt_ref[j] = v_ref[0, j * tk:(j + 1) * tk, :].astype(F32).T.astype(BF16)

    nq = s_len // tq
    for qi in range(nq):
        for a in range(2):
            qt_ref[a, qi] = qaug_ref[a, qi * tq:(qi + 1) * tq, :].astype(F32).T.astype(BF16)

    def tile(kb, qi, a, state):
        q_off = max(0, kb * tk - qi * tq)
        width = tq - q_off
        masked = (kb + 1) * tk - 1 > qi * tq + q_off
        m0, l0, acc0 = state
        m, l, acc = m0[:, q_off:], l0[:, q_off:], acc0[:, q_off:]
        s = _dot(kaug_ref[a, kb * tk:(kb + 1) * tk, :], qt_ref[a, qi, :, q_off:])
        if masked:
            n_mask = min(width, pl.cdiv((kb + 1) * tk - (qi * tq + q_off), LANES) * LANES)
            k_pos = kb * tk + lax.broadcasted_iota(jnp.int32, (tk, n_mask), 0)
            q_pos = qi * tq + q_off + lax.broadcasted_iota(jnp.int32, (tk, n_mask), 1)
            head_cols = jnp.where(k_pos <= q_pos, s[:, :n_mask], NEG)
            s = head_cols if n_mask == width else jnp.concatenate([head_cols, s[:, n_mask:]], axis=1)
        m_new = jnp.maximum(m, jnp.max(s, axis=0, keepdims=True))
        p = jnp.exp2(s - m_new)
        l_new = jnp.sum(p, axis=0, keepdims=True)
        acc_new = _dot(vt_ref[kb], p.astype(BF16))
        if kb == 0:
            l, acc = l_new, acc_new
        else:
            alpha = jnp.exp2(m - m_new)
            l, acc = alpha * l + l_new, alpha * acc + acc_new
        if q_off:
            m_new = jnp.concatenate([m0[:, :q_off], m_new], axis=1)
            l = jnp.concatenate([l0[:, :q_off], l], axis=1)
            acc = jnp.concatenate([acc0[:, :q_off], acc], axis=1)
        return m_new, l, acc

    state = {(qi, a): (jnp.full((1, tq), NEG, F32), jnp.zeros((1, tq), F32),
                       jnp.zeros((LANES, tq), F32)) for qi in range(nq) for a in range(2)}
    for kb in range(s_len // tk):
        for qi in range(kb * tk // tq, nq):
            for a in range(2):
                state[qi, a] = tile(kb, qi, a, state[qi, a])
    orow = lax.broadcasted_iota(jnp.int32, (LANES, tq), 0)
    for qi in range(nq):
        (_, l0, acc0), (_, l1, acc1) = state[qi, 0], state[qi, 1]
        o_t = jnp.where(orow < HEAD_DIM, acc0 * (1.0 / l0), acc1 * (1.0 / l1))
        o_ref[0, qi * tq:(qi + 1) * tq, :] = o_t.T.astype(BF16)


def _fox_attn(q, k, v, dcum):
    b, s, _ = q.shape
    tq, tk = min(FOX_TQ, s), min(FOX_TK, s)
    seq = lambda: pl.BlockSpec((1, s, LANES), lambda bi, j: (bi, 0, j))
    return pl.pallas_call(
        _fox_attn_kernel,
        out_shape=jax.ShapeDtypeStruct((b, s, HQ), BF16),
        grid=(b, HQ // LANES),
        in_specs=[seq(), seq(), seq(), pl.BlockSpec((1, s, LANES), lambda bi, j: (bi, 0, 0))],
        out_specs=seq(),
        scratch_shapes=[pltpu.VMEM((2, s, 2 * LANES), BF16), pltpu.VMEM((2, s, 2 * LANES), BF16),
                        pltpu.VMEM((s // tk, LANES, tk), BF16),
                        pltpu.VMEM((2, s // tq, 2 * LANES, tq), BF16)],
        compiler_params=_cparams("parallel", "parallel"),
        name="fox_attn",
    )(q, k, v, dcum)


def _memory_attention(qc, kv_ref, out_ref):
    lane = lax.broadcasted_iota(jnp.int32, (qc.shape[0], LANES), 1)
    for pair in range(HM // LANES):
        cols = slice(pair * LANES, (pair + 1) * LANES)
        kblk = kv_ref[:, cols]
        vblk = kv_ref[:, HM + pair * LANES:HM + (pair + 1) * LANES]
        qf = qc[:, cols]
        outs = []
        for a in range(2):
            keep = (lane < HEAD_DIM) if a == 0 else (lane >= HEAD_DIM)
            s = _dot_nt(jnp.where(keep, qf, 0.0).astype(BF16), kblk)
            p = jnp.exp(s - jnp.max(s, axis=-1, keepdims=True))
            l = jnp.sum(p, axis=-1, keepdims=True)
            outs.append(_dot(p.astype(BF16), vblk) * (1.0 / l))
        out_ref[:, cols] = jnp.where(lane < HEAD_DIM, outs[0], outs[1]).astype(BF16)


def _attn_out_ln(s_ref, c_ref, x_ref, w_ref, g_ref, b_ref):
    mix = _dot(s_ref[...], w_ref[0, :HQ, :]) + _dot(c_ref[...], w_ref[0, HQ:, :])
    return _layer_norm(DN_ALPHA * x_ref[...] + mix, g_ref[...], b_ref[...])


def _attn_out_specs(tm, layer):
    row = lambda n: pl.BlockSpec((tm, n), lambda i: (i, 0))
    vec = pl.BlockSpec((1, D_MODEL), lambda i: (0, 0))
    return [row(HQ), row(HM), row(D_MODEL),
            pl.BlockSpec((1, D_MODEL, D_MODEL), lambda i: (layer, 0, 0),
                         pipeline_mode=pl.Buffered(1)), vec, vec]


def _swiglu(xb, w_in_ref, w_out_ref):
    acc = None
    for lo, hi in FF_CHUNKS:
        ha = _dot(xb, w_in_ref[:, lo:hi])
        hb = _dot(xb, w_in_ref[:, D_FF + lo:D_FF + hi])
        act = (ha * (1.0 / (1.0 + jnp.exp(-ha))) * hb).astype(BF16)
        part = _dot(act, w_out_ref[lo:hi, :])
        acc = part if acc is None else acc + part
    return acc


def _dense_ffn_kernel(s_ref, c_ref, xin_ref, wo_ref, ga_ref, ba_ref, w_in_ref, w_out_ref,
                      g_ref, b_ref, ein_ref, eout_ref, o_ref, ein_bf_ref, eout_bf_ref):
    x = _attn_out_ln(s_ref, c_ref, xin_ref, wo_ref, ga_ref, ba_ref)
    ffn = _swiglu(x.astype(BF16), w_in_ref.at[0], w_out_ref.at[0])
    o_ref[...] = _layer_norm(DN_ALPHA * x + ffn, g_ref[...], b_ref[...])
    ein_bf_ref[...] = ein_ref[...].astype(BF16)
    eout_bf_ref[...] = eout_ref[...].astype(BF16)


def _dense_layer_tail(self_out, cross_out, xf, w_o, depth_idx, g_attn, b_attn,
                      w_in, w_out, layer, g, b, w_exp_in, w_exp_out):
    t = xf.shape[0]
    tm = min(t, FFN_TM)
    steps = t // tm
    n_layers = w_exp_in.shape[0]
    ein = w_exp_in.reshape(n_layers * N_EXPERTS * D_MODEL, 2 * D_FF)
    eout = w_exp_out.reshape(n_layers * N_EXPERTS * D_FF, D_MODEL)
    rin, rout = N_EXPERTS * D_MODEL // steps, N_EXPERTS * D_FF // steps
    assert rin * steps == N_EXPERTS * D_MODEL and rout * steps == N_EXPERTS * D_FF
    assert rin % 16 == 0 and rout % 16 == 0
    row = pl.BlockSpec((tm, D_MODEL), lambda i: (i, 0))
    const = lambda r, c: pl.BlockSpec((r, c), lambda i: (0, 0))
    once = pl.Buffered(1)
    y, ein_bf, eout_bf = pl.pallas_call(
        _dense_ffn_kernel,
        out_shape=(jax.ShapeDtypeStruct((t, D_MODEL), F32),
                   jax.ShapeDtypeStruct((N_EXPERTS * D_MODEL, 2 * D_FF), BF16),
                   jax.ShapeDtypeStruct((N_EXPERTS * D_FF, D_MODEL), BF16)),
        grid=(steps,),
        in_specs=_attn_out_specs(tm, depth_idx) + [
                  pl.BlockSpec((1, D_MODEL, 2 * D_FF), lambda i: (layer, 0, 0), pipeline_mode=once),
                  pl.BlockSpec((1, D_FF, D_MODEL), lambda i: (layer, 0, 0), pipeline_mode=once),
                  const(1, D_MODEL), const(1, D_MODEL),
                  pl.BlockSpec((rin, 2 * D_FF), lambda i: (layer * steps + i, 0)),
                  pl.BlockSpec((rout, D_MODEL), lambda i: (layer * steps + i, 0))],
        out_specs=(row, pl.BlockSpec((rin, 2 * D_FF), lambda i: (i, 0)),
                   pl.BlockSpec((rout, D_MODEL), lambda i: (i, 0))),
        compiler_params=_cparams("parallel"),
        name="dense_ffn_ln",
    )(self_out, cross_out, xf, w_o, g_attn, b_attn, w_in, w_out, g, b, ein, eout)
    return (y, ein_bf.reshape(N_EXPERTS, D_MODEL, 2 * D_FF),
            eout_bf.reshape(N_EXPERTS, D_FF, D_MODEL))


def _expert_kernel(be_ref, bx_ref, xs_ref, w_in_ref, w_out_ref, ys_ref):
    del be_ref
    i = pl.program_id(0)
    used = bx_ref[i] == i

    @pl.when(used)
    def _():
        ys_ref[...] = _swiglu(xs_ref[...].astype(BF16), w_in_ref.at[0], w_out_ref.at[0])

    @pl.when(jnp.logical_not(used))
    def _():
        ys_ref[...] = jnp.zeros_like(ys_ref)


def _expert_ffn(xs, w_in, w_out, block_expert, block_src):
    n_slots = xs.shape[0]
    return pl.pallas_call(
        _expert_kernel,
        out_shape=jax.ShapeDtypeStruct((n_slots, D_MODEL), F32),
        grid_spec=pltpu.PrefetchScalarGridSpec(
            num_scalar_prefetch=2,
            grid=(n_slots // MOE_TM,),
            in_specs=[pl.BlockSpec((MOE_TM, D_MODEL), lambda i, be, bx: (bx[i], 0)),
                      pl.BlockSpec((1, D_MODEL, 2 * D_FF), lambda i, be, bx: (be[i], 0, 0)),
                      pl.BlockSpec((1, D_FF, D_MODEL), lambda i, be, bx: (be[i], 0, 0))],
            out_specs=pl.BlockSpec((MOE_TM, D_MODEL), lambda i, be, bx: (i, 0))),
        compiler_params=_cparams("arbitrary"),
        name="expert_ffn",
    )(block_expert, block_src, xs, w_in, w_out)


ROUTE_E1, ROUTE_E2, ROUTE_R1, ROUTE_R2, ROUTE_G1, ROUTE_G2 = range(6)
ROUTE_ROWS = 8


def _router_kernel(s_ref, c_ref, xin_ref, wo_ref, ga_ref, ba_ref, w_ref, b_ref,
                   x_ref, route_ref, route_t_ref, count_ref, carry_ref):
    @pl.when(pl.program_id(0) == 0)
    def _():
        carry_ref[...] = jnp.zeros_like(carry_ref)

    x = _attn_out_ln(s_ref, c_ref, xin_ref, wo_ref, ga_ref, ba_ref)
    x_ref[...] = x
    xs, ws = _split3(x), _split3(w_ref[...])
    lead = _dot(xs[0], jnp.concatenate([ws[0], ws[1]], axis=1))
    logits = (_dot(xs[1], ws[0]) + lead[:, LANES:]) + lead[:, :LANES]
    tm = logits.shape[0]
    lane = lax.broadcasted_iota(jnp.int32, (tm, LANES), 1).astype(F32)
    logits = jnp.where(lane < N_EXPERTS, logits + b_ref[...], -jnp.inf)
    m1 = jnp.max(logits, axis=-1, keepdims=True)
    e1 = jnp.min(jnp.where(logits == m1, lane, float(LANES)), axis=-1, keepdims=True)
    rest = jnp.where(lane == e1, -jnp.inf, logits)
    m2 = jnp.max(rest, axis=-1, keepdims=True)
    e2 = jnp.min(jnp.where(rest == m2, lane, float(LANES)), axis=-1, keepdims=True)
    ex = jnp.exp(m2 - m1)
    g1 = 1.0 / (1.0 + ex)
    g2 = ex * g1
    sel1, sel2 = lane == e1, lane == e2
    onehot = jnp.where(sel1 | sel2, 1.0, 0.0)
    tri = (lax.broadcasted_iota(jnp.int32, (tm, tm), 0)
           > lax.broadcasted_iota(jnp.int32, (tm, tm), 1)).astype(F32).astype(BF16)
    rank = _dot(tri, onehot.astype(BF16)) + carry_ref[...]
    r1 = jnp.sum(jnp.where(sel1, rank, 0.0), axis=-1, keepdims=True)
    r2 = jnp.sum(jnp.where(sel2, rank, 0.0), axis=-1, keepdims=True)
    total = carry_ref[...] + jnp.sum(onehot, axis=0, keepdims=True)
    carry_ref[...] = total
    count_ref[...] = total
    route = jnp.zeros((tm, LANES), F32)
    for ln, val in ((ROUTE_E1, e1), (ROUTE_E2, e2), (ROUTE_R1, r1), (ROUTE_R2, r2),
                    (ROUTE_G1, g1), (ROUTE_G2, g2)):
        route = jnp.where(lane == float(ln), val, route)
    route_ref[...] = route
    route_t_ref[...] = route.T[:ROUTE_ROWS, :]


def _attn_out_router(self_out, cross_out, xf, w_o, depth_idx, g_attn, b_attn, w_pad, b_pad):
    t = xf.shape[0]
    tm = min(t, ROUTER_TM)
    return pl.pallas_call(
        _router_kernel,
        out_shape=(jax.ShapeDtypeStruct((t, D_MODEL), F32), jax.ShapeDtypeStruct((t, LANES), F32),
                   jax.ShapeDtypeStruct((ROUTE_ROWS, t), F32), jax.ShapeDtypeStruct((1, LANES), F32)),
        grid=(t // tm,),
        in_specs=_attn_out_specs(tm, depth_idx) + [
                  pl.BlockSpec((D_MODEL, LANES), lambda i: (0, 0)),
                  pl.BlockSpec((1, LANES), lambda i: (0, 0))],
        out_specs=(pl.BlockSpec((tm, D_MODEL), lambda i: (i, 0)),
                   pl.BlockSpec((tm, LANES), lambda i: (i, 0)),
                   pl.BlockSpec((ROUTE_ROWS, tm), lambda i: (0, i)),
                   pl.BlockSpec((1, LANES), lambda i: (0, 0))),
        scratch_shapes=[pltpu.VMEM((1, LANES), F32)],
        compiler_params=_cparams("arbitrary"),
        name="router",
    )(self_out, cross_out, xf, w_o, g_attn, b_attn, w_pad, b_pad)


def _start_rows(n, make_copy):
    for r in range(n):
        make_copy(r, 0).start(priority=0)
        make_copy(r, 1).start(priority=1)


def _wait_rows(n, make_copy):
    for r in range(n):
        make_copy(r, 0).wait()
        make_copy(r, 1).wait()


def _for_slot(slot, fn):
    for static_slot in range(2):
        @pl.when(slot == static_slot)
        def _():
            fn(static_slot)


def _dispatch_kernel(dest_ref, pend_ref, x_ref, xs_hbm, zero_ref, stage_ref, sems, zsem):
    step, n_steps = pl.program_id(0), pl.num_programs(0)
    n_tok = dest_ref.shape[0] // 2

    @pl.when(pl.program_id(0) == 0)
    def _():
        zero_ref[...] = jnp.zeros_like(zero_ref)

        def zero_copy(e):
            start = pl.multiple_of(pend_ref[e + 1] - MOE_TM, MOE_TM)
            return pltpu.make_async_copy(zero_ref, xs_hbm.at[pl.ds(start, MOE_TM)], zsem)

        def tail_copy(blk):
            return pltpu.make_async_copy(
                zero_ref, xs_hbm.at[pl.ds(pl.multiple_of(blk * MOE_TM, MOE_TM), MOE_TM)], zsem)

        first_tail = pend_ref[N_EXPERTS] // MOE_TM
        n_blocks = xs_hbm.shape[0] // MOE_TM
        for e in range(N_EXPERTS):
            @pl.when(pend_ref[e + 1] > pend_ref[e])
            def _():
                zero_copy(e).start()
        lax.fori_loop(first_tail, n_blocks, lambda blk, c: (tail_copy(blk).start(), c)[1], 0)
        for e in range(N_EXPERTS):
            @pl.when(pend_ref[e + 1] > pend_ref[e])
            def _():
                zero_copy(e).wait()
        lax.fori_loop(first_tail, n_blocks, lambda blk, c: (tail_copy(blk).wait(), c)[1], 0)

    def copies(of_step, slot):
        base = of_step * ROW_TM

        def make_copy(r, k):
            return pltpu.make_async_copy(stage_ref.at[slot, pl.ds(r, 1)],
                                         xs_hbm.at[pl.ds(dest_ref[k * n_tok + base + r], 1)],
                                         sems.at[slot])
        return make_copy

    def issue(slot):
        stage_ref[slot] = x_ref[...]
        _start_rows(ROW_TM, copies(step, slot))

    _for_slot(step % 2, issue)

    @pl.when(step > 0)
    def _():
        _for_slot((step - 1) % 2, lambda slot: _wait_rows(ROW_TM, copies(step - 1, slot)))

    @pl.when(step == n_steps - 1)
    def _():
        _for_slot(step % 2, lambda slot: _wait_rows(ROW_TM, copies(step, slot)))


def _dispatch(xf, dest, seg_ends, n_slots):
    t = xf.shape[0]
    return pl.pallas_call(
        _dispatch_kernel,
        out_shape=jax.ShapeDtypeStruct((n_slots, D_MODEL), F32),
        grid_spec=pltpu.PrefetchScalarGridSpec(
            num_scalar_prefetch=2,
            grid=(t // ROW_TM,),
            in_specs=[pl.BlockSpec((ROW_TM, D_MODEL), lambda i, d, p: (i, 0))],
            out_specs=pl.BlockSpec(memory_space=pl.ANY),
            scratch_shapes=[pltpu.VMEM((MOE_TM, D_MODEL), F32),
                            pltpu.VMEM((2, ROW_TM, D_MODEL), F32),
                            pltpu.SemaphoreType.DMA((2,)), pltpu.SemaphoreType.DMA(())]),
        compiler_params=_cparams("arbitrary"),
        name="moe_dispatch",
    )(dest, seg_ends, xf)


def _combine_kernel(dest_ref, x_ref, route_ref, g_ref, b_ref, ys_hbm, o_ref, ya, yb, sems):
    step, n_steps = pl.program_id(0), pl.num_programs(0)
    n_tok = dest_ref.shape[0] // 2

    def copies(of_step, slot):
        base = of_step * ROW_TM

        def make_copy(r, k):
            dst = ya if k == 0 else yb
            return pltpu.make_async_copy(ys_hbm.at[pl.ds(dest_ref[k * n_tok + base + r], 1)],
                                         dst.at[slot, pl.ds(r, 1)], sems.at[slot])
        return make_copy

    @pl.when(step == 0)
    def _():
        _start_rows(ROW_TM, copies(0, 0))

    @pl.when(step + 1 < n_steps)
    def _():
        _for_slot((step + 1) % 2, lambda slot: _start_rows(ROW_TM, copies(step + 1, slot)))

    def finish(slot):
        _wait_rows(ROW_TM, copies(step, slot))
        route = route_ref[...]
        g1 = route[:, ROUTE_G1:ROUTE_G1 + 1]
        g2 = route[:, ROUTE_G2:ROUTE_G2 + 1]
        y = ya[slot] * g1 + yb[slot] * g2
        o_ref[...] = _layer_norm(DN_ALPHA * x_ref[...] + y, g_ref[...], b_ref[...])

    _for_slot(step % 2, finish)


def _combine_ln(xf, route, ys, dest, g, b):
    t = xf.shape[0]
    row = lambda n: pl.BlockSpec((ROW_TM, n), lambda i, d: (i, 0))
    const = pl.BlockSpec((1, D_MODEL), lambda i, d: (0, 0))
    return pl.pallas_call(
        _combine_kernel,
        out_shape=jax.ShapeDtypeStruct((t, D_MODEL), F32),
        grid_spec=pltpu.PrefetchScalarGridSpec(
            num_scalar_prefetch=1,
            grid=(t // ROW_TM,),
            in_specs=[row(D_MODEL), row(LANES), const, const, pl.BlockSpec(memory_space=pl.ANY)],
            out_specs=row(D_MODEL),
            scratch_shapes=[pltpu.VMEM((2, ROW_TM, D_MODEL), F32),
                            pltpu.VMEM((2, ROW_TM, D_MODEL), F32),
                            pltpu.SemaphoreType.DMA((2,))]),
        compiler_params=_cparams("arbitrary"),
        name="moe_combine_ln",
    )(dest, xf, route, g, b, ys)


def _moe_layer_tail(self_out, cross_out, xf, w_o, depth_idx, g_attn, b_attn,
                    w_router, b_router, w_in, w_out, g, b):
    t = xf.shape[0]
    w_pad = jnp.zeros((D_MODEL, LANES), F32).at[:, :N_EXPERTS].set(w_router)
    b_pad = jnp.zeros((1, LANES), F32).at[0, :N_EXPERTS].set(b_router)
    xf, route, route_t, counts = _attn_out_router(self_out, cross_out, xf, w_o, depth_idx,
                                                  g_attn, b_attn, w_pad, b_pad)
    counts = counts[0, :N_EXPERTS].astype(jnp.int32)
    pcounts = (counts + MOE_TM - 1) // MOE_TM * MOE_TM
    pends = jnp.cumsum(pcounts)
    pstarts = pends - pcounts
    e = route_t[ROUTE_E1:ROUTE_E2 + 1].astype(jnp.int32)
    r = route_t[ROUTE_R1:ROUTE_R2 + 1].astype(jnp.int32)
    seg_start = sum(jnp.where(e == k, pstarts[k], 0) for k in range(N_EXPERTS))
    dest = (seg_start + r).reshape(-1)
    n_blocks = 2 * t // MOE_TM + N_EXPERTS
    blk_start = jnp.arange(n_blocks, dtype=jnp.int32) * MOE_TM
    block_expert = jnp.minimum(jnp.sum(blk_start[:, None] >= pends[None, :], axis=1),
                               N_EXPERTS - 1).astype(jnp.int32)
    n_used = pends[-1] // MOE_TM
    block_src = jnp.minimum(jnp.arange(n_blocks, dtype=jnp.int32), n_used - 1)
    seg_ends = jnp.concatenate([jnp.zeros((1,), jnp.int32), pends.astype(jnp.int32)])
    xs = _dispatch(xf, dest, seg_ends, n_blocks * MOE_TM)
    ys = _expert_ffn(xs, w_in, w_out, block_expert, block_src)
    return _combine_ln(xf, route, ys, dest, g, b)


def _dup_heads(w):
    d, n = w.shape
    w = w.reshape(d, n // HEAD_DIM, 1, HEAD_DIM)
    return jnp.broadcast_to(w, (d, n // HEAD_DIM, 2, HEAD_DIM)).reshape(d, 2 * n)


def kernel(x, mem, positions, w_in_swa, attn_sinks, w_in_fox, b_forget, w_mem_kv, w_out,
           ln_attn_g, ln_attn_b, w_ffn_in, w_ffn_out, w_router, b_router, w_exp_in,
           w_exp_out, ln_ffn_g, ln_ffn_b):
    b, s, d = x.shape
    t = b * s
    hkv = SWA_KV_HEADS * HEAD_DIM
    cos_t, sin_t = _rope_tables(positions)
    mem2d = mem.reshape(b * MEM_LEN, d)
    xf = x.reshape(t, d)
    w_out_b = w_out.astype(BF16)
    w_ffn_in_b, w_ffn_out_b = w_ffn_in.astype(BF16), w_ffn_out.astype(BF16)
    exp_w = None
    for i in range(DEPTH):
        j = i // 2
        kvm = _mem_kv(mem2d, w_mem_kv[i].astype(BF16)).reshape(b, MEM_LEN, 2 * HM)
        if i % 2 == 0:
            w = w_in_swa[j]
            w = jnp.concatenate([w[:, :HQ], _dup_heads(w[:, HQ:HQ + hkv]),
                                 _dup_heads(w[:, HQ + hkv:HQ + 2 * hkv]), w[:, HQ + 2 * hkv:]],
                                axis=1).astype(BF16)
            q, kk, vv, cross_out = _swa_proj(xf, w, cos_t, sin_t, kvm)
            self_out = _swa_attn(q.reshape(b, s, HQ), kk.reshape(b, s, HKV2),
                                 vv.reshape(b, s, HKV2), attn_sinks[j])
        else:
            w = jnp.pad(w_in_fox[j], ((0, 0), (0, FOX_COLS - w_in_fox.shape[-1]))).astype(BF16)
            bf_row = jnp.pad(b_forget[j], (0, LANES - SELF_HEADS))[None, :]
            q, k, v, cross_out, dcum = _fox_proj(xf.reshape(b, s, d), w, bf_row, kvm)
            self_out = _fox_attn(q, k, v, dcum)
        attn = (self_out.reshape(t, HQ), cross_out.reshape(t, HM), xf, w_out_b, i,
                ln_attn_g[i][None, :], ln_attn_b[i][None, :])
        g, bb = ln_ffn_g[i][None, :], ln_ffn_b[i][None, :]
        if i % 2 == 0:
            xf, *exp_w = _dense_layer_tail(*attn, w_ffn_in_b, w_ffn_out_b, j, g, bb,
                                           w_exp_in, w_exp_out)
        else:
            xf = _moe_layer_tail(*attn, w_router[j], b_router[j], exp_w[0], exp_w[1], g, bb)
    return xf.reshape(b, s, d)
```
